```python
import math
import jax, jax.numpy as jnp
from jax import lax
import numpy as np

D_MODEL = 4096
BATCH = 1
SEQ = 16384
DEPTH = 4

HEAD_DIM = 128
N_MIXERS = 2
NSA_HEADS = D_MODEL // HEAD_DIM
NSA_KV_GROUPS = 4
NSA_HG = NSA_HEADS // NSA_KV_GROUPS
NSA_KV_DIM = NSA_KV_GROUPS * HEAD_DIM
CMP_BLOCK = 32
CMP_STRIDE = 16
SEL_BLOCK = 64
SEL_TOPK = 16
WINDOW = 512
NSA_IN = NSA_HEADS * HEAD_DIM + 6 * NSA_KV_DIM + 3 * NSA_HEADS
FOX_HEADS = D_MODEL // HEAD_DIM
FOX_IN = 3 * FOX_HEADS * HEAD_DIM + FOX_HEADS
D_FF = -(-8 * D_MODEL // (3 * 256)) * 256
REL_BUCKETS = 32
REL_MAX_DIST = 128
Q_BLOCK = 128
N_NSA_LAYERS = (DEPTH + 1) // 2
N_FOX_LAYERS = DEPTH // 2
RMS_EPS = 1e-6
NEG_INF = -1e30
FORCED_SCORE = 1e6

kernel_name = 'hybrid_nsa_fox_swiglu'


def rmsnorm(x, g):
    x32 = x.astype(jnp.float32)
    y = x32 * lax.rsqrt(jnp.mean(x32 * x32, axis=-1, keepdims=True) + RMS_EPS)
    return (y * g.astype(jnp.float32)).astype(x.dtype)


def rel_bucket(dist):
    n = jnp.maximum(dist, 0)
    max_exact = REL_BUCKETS // 2
    nf = jnp.maximum(n, max_exact).astype(jnp.float32)
    large = max_exact + (jnp.log(nf / max_exact) / math.log(REL_MAX_DIST / max_exact)
                         * (REL_BUCKETS - max_exact)).astype(jnp.int32)
    large = jnp.minimum(large, REL_BUCKETS - 1)
    return jnp.where(n < max_exact, n, large)


def masked_softmax(s, mask):
    return jax.nn.softmax(jnp.where(mask, s, NEG_INF), axis=-1)


def compress(k, pos, w1, w2):
    B, T, G, HD = k.shape
    kb = k.reshape(B, T // CMP_STRIDE, CMP_STRIDE, G, HD)
    blocks = jnp.concatenate([kb[:, :-1], kb[:, 1:]], axis=2)
    blocks = blocks + pos[None, None, :, None, :].astype(k.dtype)
    nc = blocks.shape[1]
    flat = blocks.transpose(0, 1, 3, 2, 4).reshape(B, nc, G, CMP_BLOCK * HD)
    return jax.nn.gelu(flat @ w1) @ w2


def nsa_mixer(h, w_in, w_o, cmp_pos, cmp_w1, cmp_w2, rel_table):
    B, T, _ = h.shape
    G, HG, HD = NSA_KV_GROUPS, NSA_HG, HEAD_DIM
    NQ = T // Q_BLOCK
    NS = T // SEL_BLOCK
    NC = T // CMP_STRIDE - 1
    k_top = min(SEL_TOPK, NS)
    scale = HD ** -0.5
    f32 = jnp.float32

    proj = h @ w_in
    qd = NSA_HEADS * HD
    q = proj[..., :qd].reshape(B, T, G, HG, HD)
    kv = proj[..., qd:qd + 6 * NSA_KV_DIM].reshape(B, T, 6, G, HD)
    k_c, v_c, k_s, v_s, k_w, v_w = [kv[:, :, j] for j in range(6)]
    gates = jax.nn.sigmoid(proj[..., qd + 6 * NSA_KV_DIM:].astype(f32)).reshape(B, T, G, HG, 3)

    kc = compress(k_c, cmp_pos[0], cmp_w1[0], cmp_w2[0])
    vc = compress(v_c, cmp_pos[1], cmp_w1[1], cmp_w2[1])
    c_start = jnp.arange(NC) * CMP_STRIDE
    cmp_end = c_start + CMP_BLOCK - 1
    s_start = jnp.arange(NS) * SEL_BLOCK
    overlap = ((c_start[:, None] < s_start[None, :] + SEL_BLOCK)
               & (c_start[:, None] + CMP_BLOCK > s_start[None, :])).astype(f32)

    ks_blk = k_s.reshape(B, NS, SEL_BLOCK, G, HD).transpose(0, 3, 1, 2, 4)
    vs_blk = v_s.reshape(B, NS, SEL_BLOCK, G, HD).transpose(0, 3, 1, 2, 4)
    b_ix = jnp.arange(B)[:, None, None, None]
    g_ix = jnp.arange(G)[None, None, :, None]
    table = rel_table.astype(f32)
    tab_g = table.reshape(REL_BUCKETS, G, HG).transpose(1, 0, 2)

    def head_bias(bkt):
        return table[bkt].reshape(bkt.shape + (G, HG)).transpose(2, 3, 0, 1)

    WL = WINDOW + Q_BLOCK
    kw_pad = jnp.pad(k_w, ((0, 0), (WINDOW, 0), (0, 0), (0, 0)))
    vw_pad = jnp.pad(v_w, ((0, 0), (WINDOW, 0), (0, 0), (0, 0)))
    win_rel = jnp.arange(Q_BLOCK)[:, None] - jnp.arange(WL)[None, :] + WINDOW
    win_bias = head_bias(rel_bucket(win_rel))
    win_band = (win_rel >= 0) & (win_rel < WINDOW)
    sel_off = jnp.arange(SEL_BLOCK)
    blk_ids = jnp.arange(NS)

    def block(i):
        t0 = i * Q_BLOCK
        tpos = t0 + jnp.arange(Q_BLOCK)
        qb = lax.dynamic_slice_in_dim(q, t0, Q_BLOCK, 1)
        gb = lax.dynamic_slice_in_dim(gates, t0, Q_BLOCK, 1)

        s_c = (jnp.einsum('btghd,bngd->bghtn', qb, kc).astype(f32) * scale
               + head_bias(rel_bucket(tpos[:, None] - cmp_end[None, :])))
        p_c = masked_softmax(s_c, cmp_end[None, :] <= tpos[:, None])
        p_c = p_c * (tpos >= CMP_BLOCK - 1).astype(f32)[:, None]
        o_c = jnp.einsum('bghtn,bngd->btghd', p_c.astype(vc.dtype), vc)

        imp = jnp.einsum('bghtn,ns->btgs', p_c, overlap)
        cur = tpos // SEL_BLOCK
        forced = ((blk_ids[None, :] == 0) | (blk_ids[None, :] == cur[:, None])
                  | (blk_ids[None, :] == cur[:, None] - 1))
        causal_blk = blk_ids[None, :] * SEL_BLOCK <= tpos[:, None]
        score = jnp.where(forced[None, :, None, :], FORCED_SCORE,
                          jnp.where(causal_blk[None, :, None, :], imp, -1.0))
        _, idx = lax.top_k(score, k_top)
        k_g = ks_blk[b_ix, g_ix, idx].reshape(B, Q_BLOCK, G, k_top * SEL_BLOCK, HD)
        v_g = vs_blk[b_ix, g_ix, idx].reshape(B, Q_BLOCK, G, k_top * SEL_BLOCK, HD)
        spos = (idx[..., None] * SEL_BLOCK + sel_off).reshape(B, Q_BLOCK, G, k_top * SEL_BLOCK)
        bias_s = tab_g[g_ix, rel_bucket(tpos[None, :, None, None] - spos)].transpose(0, 2, 4, 1, 3)
        s_s = jnp.einsum('btghd,btgkd->bghtk', qb, k_g).astype(f32) * scale + bias_s
        mask_s = (spos <= tpos[None, :, None, None]).transpose(0, 2, 1, 3)[:, :, None]
        p_s = masked_softmax(s_s, mask_s)
        o_s = jnp.einsum('bghtk,btgkd->btghd', p_s.astype(v_g.dtype), v_g)

        kwin = lax.dynamic_slice_in_dim(kw_pad, t0, WL, 1)
        vwin = lax.dynamic_slice_in_dim(vw_pad, t0, WL, 1)
        s_w = jnp.einsum('btghd,bsgd->bghts', qb, kwin).astype(f32) * scale + win_bias
        mask_w = win_band & ((t0 - WINDOW + jnp.arange(WL)) >= 0)[None, :]
        p_w = masked_softmax(s_w, mask_w)
        o_w = jnp.einsum('bghts,bsgd->btghd', p_w.astype(vwin.dtype), vwin)

        o = (gb[..., 0:1] * o_c.astype(f32) + gb[..., 1:2] * o_s.astype(f32)
             + gb[..., 2:3] * o_w.astype(f32))
        return o.astype(h.dtype)

    o = lax.map(block, jnp.arange(NQ))
    o = jnp.moveaxis(o, 0, 1).reshape(B, T, NSA_HEADS * HD)
    return o @ w_o


def fox_mixer(h, w_in, b_f, w_o):
    B, T, _ = h.shape
    H, HD = FOX_HEADS, HEAD_DIM
    NQ = T // Q_BLOCK
    scale = HD ** -0.5
    f32 = jnp.float32
    proj = h @ w_in
    qkv = proj[..., :3 * H * HD].reshape(B, T, 3, H, HD)
    q, k, v = qkv[:, :, 0], qkv[:, :, 1], qkv[:, :, 2]
    log_f = jax.nn.log_sigmoid(proj[..., 3 * H * HD:].astype(f32) + b_f.astype(f32))
    cum = jnp.cumsum(log_f, axis=1).transpose(0, 2, 1)
    kpos = jnp.arange(T)

    def block(i):
        t0 = i * Q_BLOCK
        tpos = t0 + jnp.arange(Q_BLOCK)
        qb = lax.dynamic_slice_in_dim(q, t0, Q_BLOCK, 1)
        cq = lax.dynamic_slice_in_dim(cum, t0, Q_BLOCK, 2)
        s = (jnp.einsum('bthd,bshd->bhts', qb, k).astype(f32) * scale
             + (cq[..., None] - cum[:, :, None, :]))
        p = masked_softmax(s, kpos[None, :] <= tpos[:, None])
        return jnp.einsum('bhts,bshd->bthd', p.astype(v.dtype), v)

    o = lax.map(block, jnp.arange(NQ))
    o = jnp.moveaxis(o, 0, 1).reshape(B, T, H * HD)
    return o @ w_o


def swiglu(h, w_gate, w_up, w_down):
    return (jax.nn.silu(h @ w_gate) * (h @ w_up)) @ w_down


def setup_inputs(seed: int = 0) -> dict:
    key = jax.random.key(seed)
    ks = jax.random.split(key, 16)
    f32 = jnp.float32

    def nrm(k, shape, fan):
        return jax.random.normal(k, shape, f32) * (fan ** -0.5)

    x = jax.random.normal(ks[0], (BATCH, SEQ, D_MODEL), f32)
    norm_mix = 1.0 + 0.01 * jax.random.normal(ks[1], (DEPTH, D_MODEL), f32)
    norm_ffn = 1.0 + 0.01 * jax.random.normal(ks[2], (DEPTH, D_MODEL), f32)
    norm_final = 1.0 + 0.01 * jax.random.normal(ks[3], (D_MODEL,), f32)
    rel_table = 0.5 * jax.random.normal(ks[4], (REL_BUCKETS, NSA_HEADS), f32)
    nsa_w_in = nrm(ks[5], (N_NSA_LAYERS, D_MODEL, NSA_IN), D_MODEL)
    nsa_w_o = nrm(ks[6], (N_NSA_LAYERS, NSA_HEADS * HEAD_DIM, D_MODEL), NSA_HEADS * HEAD_DIM)
    nsa_cmp_pos = 0.1 * jax.random.normal(ks[7], (N_NSA_LAYERS, 2, CMP_BLOCK, HEAD_DIM), f32)
    nsa_cmp_w1 = nrm(ks[8], (N_NSA_LAYERS, 2, CMP_BLOCK * HEAD_DIM, HEAD_DIM), CMP_BLOCK * HEAD_DIM)
    nsa_cmp_w2 = nrm(ks[9], (N_NSA_LAYERS, 2, HEAD_DIM, HEAD_DIM), HEAD_DIM)
    fox_w_in = nrm(ks[10], (N_FOX_LAYERS, D_MODEL, FOX_IN), D_MODEL)
    fox_b_f = jax.random.uniform(ks[11], (N_FOX_LAYERS, FOX_HEADS), f32, 1.0, 6.0)
    fox_w_o = nrm(ks[12], (N_FOX_LAYERS, FOX_HEADS * HEAD_DIM, D_MODEL), FOX_HEADS * HEAD_DIM)
    ffn_w_gate = nrm(ks[13], (DEPTH, D_MODEL, D_FF), D_MODEL)
    ffn_w_up = nrm(ks[14], (DEPTH, D_MODEL, D_FF), D_MODEL)
    ffn_w_down = nrm(ks[15], (DEPTH, D_FF, D_MODEL), D_FF)
    return {'x': x, 'norm_mix': norm_mix, 'norm_ffn': norm_ffn, 'norm_final': norm_final,
            'rel_table': rel_table, 'nsa_w_in': nsa_w_in, 'nsa_w_o': nsa_w_o,
            'nsa_cmp_pos': nsa_cmp_pos, 'nsa_cmp_w1': nsa_cmp_w1, 'nsa_cmp_w2': nsa_cmp_w2,
            'fox_w_in': fox_w_in, 'fox_b_f': fox_b_f, 'fox_w_o': fox_w_o,
            'ffn_w_gate': ffn_w_gate, 'ffn_w_up': ffn_w_up, 'ffn_w_down': ffn_w_down}


def reference(x, norm_mix, norm_ffn, norm_final, rel_table, nsa_w_in, nsa_w_o, nsa_cmp_pos,
              nsa_cmp_w1, nsa_cmp_w2, fox_w_in, fox_b_f, fox_w_o, ffn_w_gate, ffn_w_up, ffn_w_down):
    h = x
    for i in range(DEPTH):
        a = rmsnorm(h, norm_mix[i])
        j = i // N_MIXERS
        if i % N_MIXERS == 0:
            mix = nsa_mixer(a, nsa_w_in[j], nsa_w_o[j], nsa_cmp_pos[j], nsa_cmp_w1[j],
                            nsa_cmp_w2[j], rel_table)
        else:
            mix = fox_mixer(a, fox_w_in[j], fox_b_f[j], fox_w_o[j])
        h = h + mix
        h = h + swiglu(rmsnorm(h, norm_ffn[i]), ffn_w_gate[i], ffn_w_up[i], ffn_w_down[i])
    return rmsnorm(h, norm_final)
```

```python
import functools
import math

import jax
import jax.numpy as jnp
from jax import lax
from jax.experimental import pallas as pl
from jax.experimental.pallas import tpu as pltpu

F32 = jnp.float32
BF16 = jnp.bfloat16

D_MODEL = 4096
HEAD_DIM = 128
N_HEADS = D_MODEL // HEAD_DIM
NSA_GROUPS = 4
NSA_HG = N_HEADS // NSA_GROUPS
NSA_KV_DIM = NSA_GROUPS * HEAD_DIM
CMP_BLOCK = 32
CMP_STRIDE = 16
SEL_BLOCK = 64
SEL_TOPK = 16
WINDOW = 512
REL_BUCKETS = 32
REL_MAX_DIST = 128
Q_BLOCK = 128
RMS_EPS = 1e-6
NEG_INF = -1e30
FORCED_SCORE = 1e6
SCALE = HEAD_DIM ** -0.5

LANE = 128
SEL_COLS = 256
CMP_PAD = 128
CMP_NEAR = 128
VMEM_LIMIT = 56 * 1024 * 1024


def _cparams(sem):
    return pltpu.CompilerParams(dimension_semantics=sem, vmem_limit_bytes=VMEM_LIMIT)


def _dot(a, b):
    return jnp.dot(a, b, preferred_element_type=F32)


def _dot_nt(a, b):
    return lax.dot_general(a, b, (((1,), (1,)), ((), ())), preferred_element_type=F32)


def _rmsnorm_kernel(x_ref, g_ref, o_ref):
    x = x_ref[...]
    ms = jnp.mean(x * x, axis=-1, keepdims=True)
    o_ref[...] = (x * lax.rsqrt(ms + RMS_EPS) * g_ref[...]).astype(o_ref.dtype)


def _rmsnorm(x, g, out_dtype):
    t, d = x.shape
    tm = min(256, t)
    return pl.pallas_call(
        _rmsnorm_kernel,
        grid=(t // tm,),
        in_specs=[pl.BlockSpec((tm, d), lambda i: (i, 0)),
                  pl.BlockSpec((1, d), lambda i: (0, 0))],
        out_specs=pl.BlockSpec((tm, d), lambda i: (i, 0)),
        out_shape=jax.ShapeDtypeStruct((t, d), out_dtype),
        compiler_params=_cparams(("parallel",)),
        name="rmsnorm",
    )(x, g.reshape(1, d))


def _proj_kernel(x_ref, w_ref, o_ref, *, head_major):
    r = _dot(x_ref[...], w_ref[...])
    if head_major:
        for s in range(o_ref.shape[0]):
            o_ref[s] = r[:, s * LANE:(s + 1) * LANE].astype(o_ref.dtype)
    else:
        o_ref[...] = r.astype(o_ref.dtype)


def _proj(x, w, *, out_dtype, head_major, tn):
    t, k = x.shape
    n = w.shape[1]
    tm = min(1024, t)
    tn = min(tn, n)
    if head_major:
        out_shape = jax.ShapeDtypeStruct((n // LANE, t, LANE), out_dtype)
        out_spec = pl.BlockSpec((tn // LANE, tm, LANE), lambda i, j: (j, i, 0))
    else:
        out_shape = jax.ShapeDtypeStruct((t, n), out_dtype)
        out_spec = pl.BlockSpec((tm, tn), lambda i, j: (i, j))
    return pl.pallas_call(
        functools.partial(_proj_kernel, head_major=head_major),
        grid=(t // tm, n // tn),
        in_specs=[pl.BlockSpec((tm, k), lambda i, j: (i, 0)),
                  pl.BlockSpec((k, tn), lambda i, j: (0, j))],
        out_specs=out_spec,
        out_shape=out_shape,
        compiler_params=_cparams(("parallel", "parallel")),
        name="proj",
    )(x, w)


def _mm_res_kernel(x_ref, w_ref, r_ref, o_ref):
    o_ref[...] = r_ref[...] + _dot(x_ref[...], w_ref[...])


def _mm_res(x, w, res, *, tm, tn):
    t, k = x.shape
    n = w.shape[1]
    tm = min(tm, t)
    return pl.pallas_call(
        _mm_res_kernel,
        grid=(t // tm, n // tn),
        in_specs=[pl.BlockSpec((tm, k), lambda i, j: (i, 0)),
                  pl.BlockSpec((k, tn), lambda i, j: (0, j)),
                  pl.BlockSpec((tm, tn), lambda i, j: (i, j))],
        out_specs=pl.BlockSpec((tm, tn), lambda i, j: (i, j)),
        out_shape=jax.ShapeDtypeStruct((t, n), F32),
        compiler_params=_cparams(("parallel", "parallel")),
        name="mm_res",
    )(x, w, res)


def _gateup_kernel(x_ref, wg_ref, wu_ref, o_ref):
    x = x_ref[...]
    g = _dot(x, wg_ref[...])
    u = _dot(x, wu_ref[...])
    o_ref[...] = (g * jax.nn.sigmoid(g) * u).astype(o_ref.dtype)


def _gateup(x, wg, wu, *, tn):
    t, k = x.shape
    n = wg.shape[1]
    tm = min(1024, t)
    return pl.pallas_call(
        _gateup_kernel,
        grid=(t // tm, n // tn),
        in_specs=[pl.BlockSpec((tm, k), lambda i, j: (i, 0)),
                  pl.BlockSpec((k, tn), lambda i, j: (0, j)),
                  pl.BlockSpec((k, tn), lambda i, j: (0, j))],
        out_specs=pl.BlockSpec((tm, tn), lambda i, j: (i, j)),
        out_shape=jax.ShapeDtypeStruct((t, n), BF16),
        compiler_params=_cparams(("parallel", "parallel")),
        name="gateup",
    )(x, wg, wu)


def _ffn(h, g_norm, w_gate, w_up, w_down):
    a = _rmsnorm(h, g_norm, BF16)
    h1 = _gateup(a, w_gate.astype(BF16), w_up.astype(BF16), tn=256)
    return _mm_res(h1, w_down.astype(BF16), h, tm=512, tn=256)


def _cum_kernel(f_ref, b_ref, tri_ref, o_ref, carry_ref):
    @pl.when(pl.program_id(0) == 0)
    def _():
        carry_ref[...] = jnp.zeros_like(carry_ref)

    x = f_ref[...] + b_ref[...]
    log_f = jnp.minimum(x, 0.0) - jnp.log1p(jnp.exp(-jnp.abs(x)))
    c = jnp.dot(tri_ref[...], log_f, precision=lax.Precision.HIGHEST,
                preferred_element_type=F32) + carry_ref[...]
    o_ref[...] = c
    carry_ref[...] = c[c.shape[0] - 1:, :]


def _cum_log_forget(f, b):
    t, n = f.shape
    tb = min(512, t)
    tri = jnp.tril(jnp.ones((tb, tb), F32))
    return pl.pallas_call(
        _cum_kernel,
        grid=(t // tb,),
        in_specs=[pl.BlockSpec((tb, n), lambda i: (i, 0)),
                  pl.BlockSpec((1, n), lambda i: (0, 0)),
                  pl.BlockSpec((tb, tb), lambda i: (0, 0))],
        out_specs=pl.BlockSpec((tb, n), lambda i: (i, 0)),
        out_shape=jax.ShapeDtypeStruct((t, n), F32),
        scratch_shapes=[pltpu.VMEM((1, n), F32)],
        compiler_params=_cparams(("arbitrary",)),
        name="cum_log_forget",
    )(f, b, tri)


def _flash_update(s, v, m_ref, l_ref, acc_ref):
    m_prev = m_ref[...]
    m_new = jnp.maximum(m_prev, jnp.max(s, axis=-1, keepdims=True))
    alpha = jnp.exp(m_prev - m_new)
    p = jnp.exp(s - m_new)
    l_ref[...] = alpha * l_ref[...] + jnp.sum(p, axis=-1, keepdims=True)
    acc_ref[...] = alpha * acc_ref[...] + _dot(p.astype(BF16), v)
    m_ref[...] = m_new


def _flash_init(m_ref, l_ref, acc_ref):
    m_ref[...] = jnp.full(m_ref.shape, NEG_INF, F32)
    l_ref[...] = jnp.zeros(l_ref.shape, F32)
    acc_ref[...] = jnp.zeros(acc_ref.shape, F32)


def _fox_kernel(q_ref, k_ref, v_ref, cq_ref, ck_ref, o_ref, m_ref, l_ref, acc_ref, *, tq):
    i = pl.program_id(1)
    q = q_ref[...]
    cq = cq_ref[...]
    _flash_init(m_ref, l_ref, acc_ref)

    def scores(c):
        off = pl.multiple_of(c * tq, tq)
        k = k_ref[pl.ds(off, tq), :]
        v = v_ref[pl.ds(off, tq), :]
        s = _dot_nt(q, k) * SCALE + (cq - ck_ref[:, pl.ds(off, tq)])
        return s, v

    def body(c, carry):
        s, v = scores(c)
        _flash_update(s, v, m_ref, l_ref, acc_ref)
        return carry

    lax.fori_loop(0, i, body, 0)
    s, v = scores(i)
    row = lax.broadcasted_iota(jnp.int32, s.shape, 0)
    col = lax.broadcasted_iota(jnp.int32, s.shape, 1)
    _flash_update(jnp.where(col <= row, s, NEG_INF), v, m_ref, l_ref, acc_ref)
    o_ref[...] = (acc_ref[...] / l_ref[...]).astype(o_ref.dtype)


def _fox_attention(qkvh, cum_col, cum_row):
    t = qkvh.shape[1]
    tq = min(512, t)
    return pl.pallas_call(
        functools.partial(_fox_kernel, tq=tq),
        grid=(N_HEADS, t // tq),
        in_specs=[pl.BlockSpec((None, tq, HEAD_DIM), lambda h, i: (h, i, 0)),
                  pl.BlockSpec((None, t, HEAD_DIM), lambda h, i: (N_HEADS + h, 0, 0)),
                  pl.BlockSpec((None, t, HEAD_DIM), lambda h, i: (2 * N_HEADS + h, 0, 0)),
                  pl.BlockSpec((None, tq, 1), lambda h, i: (h, i, 0)),
                  pl.BlockSpec((None, 1, t), lambda h, i: (h, 0, 0))],
        out_specs=pl.BlockSpec((tq, HEAD_DIM), lambda h, i: (i, h)),
        out_shape=jax.ShapeDtypeStruct((t, D_MODEL), BF16),
        scratch_shapes=[pltpu.VMEM((tq, 1), F32), pltpu.VMEM((tq, 1), F32),
                        pltpu.VMEM((tq, HEAD_DIM), F32)],
        compiler_params=_cparams(("parallel", "arbitrary")),
        name="fox_attention",
    )(qkvh, qkvh, qkvh, cum_col, cum_row)


def _fox_layer(h, g_norm, w_in, b_f, w_o):
    t = h.shape[0]
    a = _rmsnorm(h, g_norm, BF16)
    n_qkv = 3 * D_MODEL
    qkvh = _proj(a, w_in[:, :n_qkv].astype(BF16), out_dtype=BF16, head_major=True, tn=512)
    w_f = jnp.pad(w_in[:, n_qkv:], ((0, 0), (0, LANE - N_HEADS))).astype(BF16)
    f = _proj(a, w_f, out_dtype=F32, head_major=False, tn=LANE)
    b = jnp.pad(b_f.astype(F32), (0, LANE - N_HEADS)).reshape(1, LANE)
    cum = _cum_log_forget(f, b)[:, :N_HEADS].T
    o = _fox_attention(qkvh, cum.reshape(N_HEADS, t, 1), cum.reshape(N_HEADS, 1, t))
    return _mm_res(o, w_o.astype(BF16), h, tm=1024, tn=512)


def _rel_bucket_const(dist):
    n = jnp.maximum(dist, 0)
    max_exact = REL_BUCKETS // 2
    nf = jnp.maximum(n, max_exact).astype(F32)
    large = max_exact + (jnp.log(nf / max_exact) / math.log(REL_MAX_DIST / max_exact)
                         * (REL_BUCKETS - max_exact)).astype(jnp.int32)
    large = jnp.minimum(large, REL_BUCKETS - 1)
    return jnp.where(n < max_exact, n, large)


def _bias_kernel(tab_ref, bkt_ref, o_ref):
    h = pl.program_id(0)
    bkt = bkt_ref[...]
    acc = jnp.zeros(bkt.shape, F32)
    for b in range(REL_BUCKETS):
        acc = jnp.where(bkt == b, tab_ref[b, h], acc)
    o_ref[...] = acc


def _bias_template(rel_table, bkt):
    r, c = bkt.shape
    return pl.pallas_call(
        _bias_kernel,
        grid=(N_HEADS,),
        in_specs=[pl.BlockSpec(memory_space=pltpu.SMEM),
                  pl.BlockSpec((r, c), lambda h: (0, 0))],
        out_specs=pl.BlockSpec((None, r, c), lambda h: (h, 0, 0)),
        out_shape=jax.ShapeDtypeStruct((N_HEADS, r, c), F32),
        compiler_params=_cparams(("arbitrary",)),
        name="bias_template",
    )(rel_table, bkt)


def _nsa_bias_templates(rel_table):
    tl = jnp.arange(Q_BLOCK)[:, None]
    bd = _bias_template(rel_table, _rel_bucket_const(tl + Q_BLOCK - jnp.arange(2 * Q_BLOCK)[None, :]))
    m = jnp.arange(CMP_NEAR)[None, :]
    bc = _bias_template(rel_table, _rel_bucket_const(
        tl - CMP_STRIDE * (m - (CMP_NEAR - 8)) - (CMP_BLOCK - 1)))
    return bd, bc


def _gelu_tanh(x):
    return 0.5 * x * (1.0 + jnp.tanh(math.sqrt(2.0 / math.pi) * (x + 0.044715 * (x * x * x))))


def _cmp_kernel(kb_ref, pos_ref, w1_ref, w2_ref, o_ref, *, nb):
    half = CMP_STRIDE * HEAD_DIM
    kb = kb_ref[...].astype(F32)
    xa = (kb + pos_ref[0:1, :]).astype(BF16)
    xb = (kb + pos_ref[1:2, :]).astype(BF16)
    a = _dot(xa, w1_ref[0:half, :])
    b = _dot(xb, w1_ref[half:2 * half, :])
    pre = a + pltpu.roll(b, nb - 1, axis=0)
    out = _dot(_gelu_tanh(pre).astype(BF16), w2_ref[...])
    row = lax.broadcasted_iota(jnp.int32, out.shape, 0)
    out = jnp.where(row < nb - 1, out, 0.0)
    o_ref[0:CMP_PAD, :] = jnp.zeros((CMP_PAD, HEAD_DIM), F32)
    o_ref[CMP_PAD:CMP_PAD + nb, :] = out


def _compress(kvh, cmp_pos, cmp_w1, cmp_w2):
    t = kvh.shape[1]
    nb = t // CMP_STRIDE
    kb = kvh[:2 * NSA_GROUPS].reshape(2 * NSA_GROUPS, nb, CMP_STRIDE * HEAD_DIM)
    pos = cmp_pos.astype(F32).reshape(2, 2, CMP_STRIDE * HEAD_DIM)
    return pl.pallas_call(
        functools.partial(_cmp_kernel, nb=nb),
        grid=(2 * NSA_GROUPS,),
        in_specs=[pl.BlockSpec((None, nb, CMP_STRIDE * HEAD_DIM), lambda j: (j, 0, 0)),
                  pl.BlockSpec((None, 2, CMP_STRIDE * HEAD_DIM), lambda j: (j // NSA_GROUPS, 0, 0)),
                  pl.BlockSpec((None, CMP_BLOCK * HEAD_DIM, HEAD_DIM), lambda j: (j // NSA_GROUPS, 0, 0)),
                  pl.BlockSpec((None, HEAD_DIM, HEAD_DIM), lambda j: (j // NSA_GROUPS, 0, 0))],
        out_specs=pl.BlockSpec((None, CMP_PAD + nb, HEAD_DIM), lambda j: (j, 0, 0)),
        out_shape=jax.ShapeDtypeStruct((2 * NSA_GROUPS, CMP_PAD + nb, HEAD_DIM), F32),
        compiler_params=_cparams(("parallel",)),
        name="compress",
    )(kb, pos, cmp_w1.astype(BF16), cmp_w2.astype(BF16))


def _split3(x):
    hi = x.astype(BF16)
    r = x - hi.astype(F32)
    mid = r.astype(BF16)
    lo = (r - mid.astype(F32)).astype(BF16)
    return hi, mid, lo


def _dot_split(x, w):
    hi, mid, lo = _split3(x)
    return _dot(hi, w) + _dot(mid, w) + _dot(lo, w)


def _nsa_kernel(tab_ref, q_ref, g_ref, kc_ref, vc_ref, ksa_ref, vs_ref, kw_ref, vw_ref,
                bd_ref, bc_ref, ovl_ref, o_ref,
                qa_ref, m_ref, l_ref, acc_ref, out_ref, *, ncp):
    g = pl.program_id(0)
    i = pl.program_id(1)
    rows = NSA_HG * Q_BLOCK
    q = q_ref[...].reshape(rows, HEAD_DIM)
    tl_col = lax.broadcasted_iota(jnp.int32, (rows, 1), 0) % Q_BLOCK
    cl = lax.broadcasted_iota(jnp.int32, (1, Q_BLOCK), 1)
    t_far = jnp.concatenate(
        [jnp.full((Q_BLOCK, 1), tab_ref[REL_BUCKETS - 1, g * NSA_HG + hg], F32) for hg in range(NSA_HG)], axis=0)
    bd = bd_ref[...].reshape(rows, 2 * Q_BLOCK)
    sig = jax.nn.sigmoid(g_ref[...])

    def gate(branch):
        return jnp.concatenate([sig[:, 3 * hg + branch:3 * hg + branch + 1] for hg in range(NSA_HG)], axis=0)

    def head_sum(p):
        acc = p[0:Q_BLOCK]
        for hg in range(1, NSA_HG):
            acc = acc + p[hg * Q_BLOCK:(hg + 1) * Q_BLOCK]
        return acc

    kcf = kc_ref[CMP_PAD:CMP_PAD + ncp, :].astype(BF16)
    vcf = vc_ref[CMP_PAD:CMP_PAD + ncp, :].astype(BF16)
    n_first_near = 8 * i - (CMP_NEAR - 8)
    far_ok = lax.broadcasted_iota(jnp.int32, (1, ncp), 1) < n_first_near
    s_far = jnp.where(far_ok, _dot_nt(q, kcf) * SCALE + t_far, NEG_INF)
    start = pl.multiple_of(8 * i + 8 + CMP_PAD - CMP_NEAR, 8)
    kcn = kc_ref[pl.ds(start, CMP_NEAR), :].astype(BF16)
    vcn = vc_ref[pl.ds(start, CMP_NEAR), :].astype(BF16)
    d_near = tl_col - CMP_STRIDE * (cl - (CMP_NEAR - 8)) - (CMP_BLOCK - 1)
    near_ok = (d_near >= 0) & (n_first_near + cl >= 0)
    s_near = jnp.where(near_ok, _dot_nt(q, kcn) * SCALE + bc_ref[...].reshape(rows, CMP_NEAR), NEG_INF)
    mx = jnp.maximum(jnp.max(s_far, axis=-1, keepdims=True), jnp.max(s_near, axis=-1, keepdims=True))
    e_far = jnp.where(far_ok, jnp.exp(s_far - mx), 0.0)
    e_near = jnp.where(near_ok, jnp.exp(s_near - mx), 0.0)
    den = jnp.sum(e_far, axis=-1, keepdims=True) + jnp.sum(e_near, axis=-1, keepdims=True)
    inv = jnp.where(den > 0.0, 1.0 / den, 0.0)
    p_far = e_far * inv
    p_near = e_near * inv
    o_c = _dot(p_far.astype(BF16), vcf) + _dot(p_near.astype(BF16), vcn)
    out_ref[...] = gate(0) * o_c

    nn = n_first_near + lax.broadcasted_iota(jnp.int32, (CMP_NEAR, SEL_COLS), 0)
    sb = lax.broadcasted_iota(jnp.int32, (CMP_NEAR, SEL_COLS), 1)
    ovl_near = jnp.where((nn >= 4 * sb - 1) & (nn <= 4 * sb + 3) & (nn >= 0), 1.0, 0.0).astype(BF16)
    imp = _dot_split(head_sum(p_far), ovl_ref[...]) + _dot_split(head_sum(p_near), ovl_near)
    blk = lax.broadcasted_iota(jnp.int32, (Q_BLOCK, SEL_COLS), 1)
    tl = lax.broadcasted_iota(jnp.int32, (Q_BLOCK, SEL_COLS), 0)
    cur = 2 * i + (tl >= SEL_BLOCK).astype(jnp.int32)
    forced = (blk == 0) | (blk == cur) | (blk == cur - 1)
    causal_blk = blk * SEL_BLOCK <= Q_BLOCK * i + tl
    work = jnp.where(forced, FORCED_SCORE, jnp.where(causal_blk, imp, -1.0))
    blk_f = blk.astype(F32)
    sel = jnp.zeros((Q_BLOCK, SEL_COLS), F32)
    for _ in range(SEL_TOPK):
        best = jnp.max(work, axis=-1, keepdims=True)
        first = jnp.min(jnp.where(work == best, blk_f, float(SEL_COLS)), axis=-1, keepdims=True)
        pick = blk_f == first
        sel = jnp.where(pick, 1.0, sel)
        work = jnp.where(pick, -2.0, work)
    amask = jnp.where(sel > 0.0, 0.0, NEG_INF).astype(BF16)
    for half in range(SEL_COLS // LANE):
        qa_ref[half, :, 0:HEAD_DIM] = q
        qa_ref[half, :, HEAD_DIM:2 * HEAD_DIM] = jnp.concatenate(
            [amask[:, half * LANE:(half + 1) * LANE]] * NSA_HG, axis=0)

    def sel_scores(c):
        off = pl.multiple_of(c * Q_BLOCK, Q_BLOCK)
        ks = ksa_ref[pl.ds(off, Q_BLOCK), :]
        v = vs_ref[pl.ds(off, Q_BLOCK), :]
        return _dot_nt(qa_ref[c // (LANE // 2)], ks) * SCALE, v

    _flash_init(m_ref, l_ref, acc_ref)
    s, v = sel_scores(i)
    _flash_update(jnp.where(cl <= tl_col, s + bd[:, Q_BLOCK:], NEG_INF), v, m_ref, l_ref, acc_ref)
    s, v = sel_scores(jnp.maximum(i - 1, 0))
    _flash_update(jnp.where(i >= 1, s + bd[:, :Q_BLOCK], NEG_INF), v, m_ref, l_ref, acc_ref)

    def far_body(c, carry):
        s, v = sel_scores(c)
        _flash_update(s + t_far, v, m_ref, l_ref, acc_ref)
        return carry

    lax.fori_loop(0, jnp.maximum(i - 1, 0), far_body, 0)
    out_ref[...] += gate(1) * (acc_ref[...] / l_ref[...])

    def win_scores(c):
        off = pl.multiple_of(jnp.maximum(c, 0) * Q_BLOCK, Q_BLOCK)
        return _dot_nt(q, kw_ref[pl.ds(off, Q_BLOCK), :]) * SCALE, vw_ref[pl.ds(off, Q_BLOCK), :]

    _flash_init(m_ref, l_ref, acc_ref)
    s, v = win_scores(i)
    _flash_update(jnp.where(cl <= tl_col, s + bd[:, Q_BLOCK:], NEG_INF), v, m_ref, l_ref, acc_ref)
    s, v = win_scores(i - 1)
    _flash_update(jnp.where(i >= 1, s + bd[:, :Q_BLOCK], NEG_INF), v, m_ref, l_ref, acc_ref)
    for back in (2, 3):
        s, v = win_scores(i - back)
        _flash_update(jnp.where(i >= back, s + t_far, NEG_INF), v, m_ref, l_ref, acc_ref)
    s, v = win_scores(i - 4)
    _flash_update(jnp.where((cl > tl_col) & (i >= 4), s + t_far, NEG_INF), v, m_ref, l_ref, acc_ref)
    o = (out_ref[...] + gate(2) * (acc_ref[...] / l_ref[...])).astype(o_ref.dtype)
    for hg in range(NSA_HG):
        o_ref[:, hg * HEAD_DIM:(hg + 1) * HEAD_DIM] = o[hg * Q_BLOCK:(hg + 1) * Q_BLOCK]


def _nsa_attention(rel_table, qkvh, gates, kcv, ksa, bd, bc, ovl):
    t = qkvh.shape[1]
    ncp = t // CMP_STRIDE
    rows = NSA_HG * Q_BLOCK
    kv0 = N_HEADS
    resident = functools.partial(pl.BlockSpec, pipeline_mode=pl.Buffered(1))
    return pl.pallas_call(
        functools.partial(_nsa_kernel, ncp=ncp),
        grid=(NSA_GROUPS, t // Q_BLOCK),
        in_specs=[pl.BlockSpec(memory_space=pltpu.SMEM),
                  pl.BlockSpec((NSA_HG, Q_BLOCK, HEAD_DIM), lambda g, i: (g, i, 0)),
                  pl.BlockSpec((None, Q_BLOCK, LANE), lambda g, i: (g, i, 0)),
                  resident((None, CMP_PAD + ncp, HEAD_DIM), lambda g, i: (g, 0, 0)),
                  resident((None, CMP_PAD + ncp, HEAD_DIM), lambda g, i: (NSA_GROUPS + g, 0, 0)),
                  resident((None, t, 2 * HEAD_DIM), lambda g, i: (g, 0, 0)),
                  resident((None, t, HEAD_DIM), lambda g, i: (kv0 + 3 * NSA_GROUPS + g, 0, 0)),
                  resident((None, t, HEAD_DIM), lambda g, i: (kv0 + 4 * NSA_GROUPS + g, 0, 0)),
                  resident((None, t, HEAD_DIM), lambda g, i: (kv0 + 5 * NSA_GROUPS + g, 0, 0)),
                  resident((NSA_HG, Q_BLOCK, 2 * Q_BLOCK), lambda g, i: (g, 0, 0)),
                  resident((NSA_HG, Q_BLOCK, CMP_NEAR), lambda g, i: (g, 0, 0)),
                  resident((ncp, SEL_COLS), lambda g, i: (0, 0))],
        out_specs=pl.BlockSpec((Q_BLOCK, NSA_HG * HEAD_DIM), lambda g, i: (i, g)),
        out_shape=jax.ShapeDtypeStruct((t, D_MODEL), BF16),
        scratch_shapes=[pltpu.VMEM((SEL_COLS // LANE, rows, 2 * HEAD_DIM), BF16),
                        pltpu.VMEM((rows, 1), F32), pltpu.VMEM((rows, 1), F32),
                        pltpu.VMEM((rows, HEAD_DIM), F32), pltpu.VMEM((rows, HEAD_DIM), F32)],
        compiler_params=_cparams(("parallel", "arbitrary")),
        name="nsa_attention",
    )(rel_table, qkvh, gates, kcv, kcv, ksa, qkvh, qkvh, qkvh, bd, bc, ovl)


def _nsa_layer(h, g_norm, w_in, w_o, cmp_pos, cmp_w1, cmp_w2, rel_table, bd, bc):
    t = h.shape[0]
    assert t // SEL_BLOCK <= SEL_COLS
    a = _rmsnorm(h, g_norm, BF16)
    n_qkv = D_MODEL + 6 * NSA_KV_DIM
    qkvh = _proj(a, w_in[:, :n_qkv].astype(BF16), out_dtype=BF16, head_major=True, tn=512)
    w_g = w_in[:, n_qkv:].reshape(D_MODEL, NSA_GROUPS, 3 * NSA_HG)
    w_g = jnp.pad(w_g, ((0, 0), (0, 0), (0, LANE - 3 * NSA_HG))).reshape(D_MODEL, NSA_GROUPS * LANE)
    gates = _proj(a, w_g.astype(BF16), out_dtype=F32, head_major=True, tn=NSA_GROUPS * LANE)
    kvh = qkvh[N_HEADS:]
    kcv = _compress(kvh, cmp_pos, cmp_w1, cmp_w2)
    key_blk = (jnp.arange(t) // SEL_BLOCK) % LANE
    onehot = (key_blk[:, None] == jnp.arange(LANE)[None, :]).astype(BF16)
    ksa = jnp.concatenate([kvh[2 * NSA_GROUPS:3 * NSA_GROUPS],
                           jnp.broadcast_to(onehot, (NSA_GROUPS, t, LANE))], axis=-1)
    n = jnp.arange(t // CMP_STRIDE)[:, None]
    sblk = jnp.arange(SEL_COLS)[None, :]
    ovl = ((n >= 4 * sblk - 1) & (n <= 4 * sblk + 3)).astype(BF16)
    o = _nsa_attention(rel_table.astype(F32), qkvh, gates, kcv, ksa, bd, bc, ovl)
    return _mm_res(o, w_o.astype(BF16), h, tm=1024, tn=512)


def kernel(x, norm_mix, norm_ffn, norm_final, rel_table, nsa_w_in, nsa_w_o, nsa_cmp_pos, nsa_cmp_w1,
           nsa_cmp_w2, fox_w_in, fox_b_f, fox_w_o, ffn_w_gate, ffn_w_up, ffn_w_down):
    b, t, d = x.shape
    depth = norm_mix.shape[0]
    bd, bc = _nsa_bias_templates(rel_table.astype(F32))
    outs = []
    for bi in range(b):
        h = x[bi]
        for i in range(depth):
            j = i // 2
            if i % 2 == 0:
                h = _nsa_layer(h, norm_mix[i], nsa_w_in[j], nsa_w_o[j], nsa_cmp_pos[j], nsa_cmp_w1[j],
                               nsa_cmp_w2[j], rel_table, bd, bc)
            else:
                h = _fox_layer(h, norm_mix[i], fox_w_in[j], fox_b_f[j], fox_w_o[j])
            h = _ffn(h, norm_ffn[i], ffn_w_gate[i], ffn_w_up[i], ffn_w_down[i])
        outs.append(_rmsnorm(h, norm_final, F32))
    return jnp.stack(outs, axis=0)
```

```python
import functools
import math

import jax
import jax.numpy as jnp
from jax import lax
from jax.experimental import pallas as pl
from jax.experimental.pallas import tpu as pltpu

F32 = jnp.float32
BF16 = jnp.bfloat16

D_MODEL = 4096
HEAD_DIM = 128
N_HEADS = D_MODEL // HEAD_DIM
NSA_GROUPS = 4
NSA_HG = N_HEADS // NSA_GROUPS
NSA_KV_DIM = NSA_GROUPS * HEAD_DIM
CMP_BLOCK = 32
CMP_STRIDE = 16
SEL_BLOCK = 64
SEL_TOPK = 16
WINDOW = 512
REL_BUCKETS = 32
REL_MAX_DIST = 128
Q_BLOCK = 128
RMS_EPS = 1e-6
NEG_INF = -1e30
FORCED_SCORE = 1e6
SCALE = HEAD_DIM ** -0.5

LANE = 128
SEL_COLS = 256
CMP_PAD = 128
CMP_NEAR = 128
NSA_ROWS = NSA_HG * Q_BLOCK
VMEM_LIMIT = 56 * 1024 * 1024


def _cparams(sem):
    return pltpu.CompilerParams(dimension_semantics=sem, vmem_limit_bytes=VMEM_LIMIT)


def _dot(a, b):
    return jnp.dot(a, b, preferred_element_type=F32)


def _dot_nt(a, b):
    return lax.dot_general(a, b, (((1,), (1,)), ((), ())), preferred_element_type=F32)


def _rmsnorm_kernel(x_ref, g_ref, o_ref):
    x = x_ref[...]
    ms = jnp.mean(x * x, axis=-1, keepdims=True)
    o_ref[...] = (x * lax.rsqrt(ms + RMS_EPS) * g_ref[...]).astype(o_ref.dtype)


def _rmsnorm(x, g, out_dtype):
    t, d = x.shape
    tm = min(256, t)
    return pl.pallas_call(
        _rmsnorm_kernel,
        grid=(t // tm,),
        in_specs=[pl.BlockSpec((tm, d), lambda i: (i, 0)),
                  pl.BlockSpec((1, d), lambda i: (0, 0))],
        out_specs=pl.BlockSpec((tm, d), lambda i: (i, 0)),
        out_shape=jax.ShapeDtypeStruct((t, d), out_dtype),
        compiler_params=_cparams(("parallel",)),
        name="rmsnorm",
    )(x, g.reshape(1, d))


def _proj_kernel(x_ref, w_ref, o_ref, *, head_major, n_scaled_blocks):
    r = _dot(x_ref[...], w_ref[...])
    if n_scaled_blocks:
        r = r * jnp.where(pl.program_id(1) < n_scaled_blocks, SCALE, 1.0)
    if head_major:
        for s in range(o_ref.shape[0]):
            o_ref[s] = r[:, s * LANE:(s + 1) * LANE].astype(o_ref.dtype)
    else:
        o_ref[...] = r.astype(o_ref.dtype)


def _proj(x, w, *, out_dtype, head_major, tn, n_scaled=0):
    t, k = x.shape
    n = w.shape[1]
    tm = min(1024, t)
    tn = min(tn, n)
    assert n_scaled % tn == 0
    if head_major:
        out_shape = jax.ShapeDtypeStruct((n // LANE, t, LANE), out_dtype)
        out_spec = pl.BlockSpec((tn // LANE, tm, LANE), lambda i, j: (j, i, 0))
    else:
        out_shape = jax.ShapeDtypeStruct((t, n), out_dtype)
        out_spec = pl.BlockSpec((tm, tn), lambda i, j: (i, j))
    return pl.pallas_call(
        functools.partial(_proj_kernel, head_major=head_major, n_scaled_blocks=n_scaled // tn),
        grid=(t // tm, n // tn),
        in_specs=[pl.BlockSpec((tm, k), lambda i, j: (i, 0)),
                  pl.BlockSpec((k, tn), lambda i, j: (0, j))],
        out_specs=out_spec,
        out_shape=out_shape,
        compiler_params=_cparams(("parallel", "parallel")),
        name="proj",
    )(x, w)


def _mm_res_kernel(x_ref, w_ref, r_ref, o_ref):
    o_ref[...] = r_ref[...] + _dot(x_ref[...], w_ref[...])


def _mm_res(x, w, res, *, tm, tn):
    t, k = x.shape
    n = w.shape[1]
    tm = min(tm, t)
    return pl.pallas_call(
        _mm_res_kernel,
        grid=(t // tm, n // tn),
        in_specs=[pl.BlockSpec((tm, k), lambda i, j: (i, 0)),
                  pl.BlockSpec((k, tn), lambda i, j: (0, j)),
                  pl.BlockSpec((tm, tn), lambda i, j: (i, j))],
        out_specs=pl.BlockSpec((tm, tn), lambda i, j: (i, j)),
        out_shape=jax.ShapeDtypeStruct((t, n), F32),
        compiler_params=_cparams(("parallel", "parallel")),
        name="mm_res",
    )(x, w, res)


def _gateup_kernel(x_ref, wg_ref, wu_ref, o_ref):
    x = x_ref[...]
    g = _dot(x, wg_ref[...])
    u = _dot(x, wu_ref[...])
    o_ref[...] = (g * jax.nn.sigmoid(g) * u).astype(o_ref.dtype)


def _gateup(x, wg, wu, *, tn):
    t, k = x.shape
    n = wg.shape[1]
    tm = min(1024, t)
    return pl.pallas_call(
        _gateup_kernel,
        grid=(t // tm, n // tn),
        in_specs=[pl.BlockSpec((tm, k), lambda i, j: (i, 0)),
                  pl.BlockSpec((k, tn), lambda i, j: (0, j)),
                  pl.BlockSpec((k, tn), lambda i, j: (0, j))],
        out_specs=pl.BlockSpec((tm, tn), lambda i, j: (i, j)),
        out_shape=jax.ShapeDtypeStruct((t, n), BF16),
        compiler_params=_cparams(("parallel", "parallel")),
        name="gateup",
    )(x, wg, wu)


def _ffn(h, g_norm, w_gate, w_up, w_down):
    a = _rmsnorm(h, g_norm, BF16)
    h1 = _gateup(a, w_gate.astype(BF16), w_up.astype(BF16), tn=256)
    return _mm_res(h1, w_down.astype(BF16), h, tm=512, tn=256)


def _flash_init(m_ref, l_ref, acc_ref):
    m_ref[...] = jnp.full(m_ref.shape, NEG_INF, F32)
    l_ref[...] = jnp.zeros(l_ref.shape, F32)
    acc_ref[...] = jnp.zeros(acc_ref.shape, F32)


def _flash_update(s_t, v_t, m_ref, l_ref, acc_ref):
    m_prev = m_ref[...]
    m_new = jnp.maximum(m_prev, jnp.max(s_t, axis=0, keepdims=True))
    alpha = jnp.exp(m_prev - m_new)
    p = jnp.exp(s_t - m_new)
    l_ref[...] = alpha * l_ref[...] + jnp.sum(p, axis=0, keepdims=True)
    acc_ref[...] = alpha * acc_ref[...] + _dot(v_t, p.astype(BF16))
    m_ref[...] = m_new


def _split3(x):
    hi = x.astype(BF16)
    r = x - hi.astype(F32)
    mid = r.astype(BF16)
    lo = (r - mid.astype(F32)).astype(BF16)
    return hi, mid, lo


def _split3_trunc(x):
    def top(v):
        bits = lax.bitcast_convert_type(v, jnp.uint32) & jnp.uint32(0xFFFF0000)
        return lax.bitcast_convert_type(bits, F32)
    hi = top(x)
    mid = top(x - hi)
    lo = x - hi - mid
    return hi.astype(BF16), mid.astype(BF16), lo.astype(BF16)


def _cum_kernel(f_ref, b_ref, tri_ref, o_ref, carry_ref):
    @pl.when(pl.program_id(0) == 0)
    def _():
        carry_ref[...] = jnp.zeros_like(carry_ref)

    x = f_ref[...] + b_ref[...]
    log_f = jnp.minimum(x, 0.0) - jnp.log1p(jnp.exp(-jnp.abs(x)))
    c = jnp.dot(tri_ref[...], log_f, precision=lax.Precision.HIGHEST,
                preferred_element_type=F32) + carry_ref[...]
    o_ref[...] = c
    carry_ref[...] = c[c.shape[0] - 1:, :]


def _cum_log_forget(f, b):
    t, n = f.shape
    tb = min(512, t)
    tri = jnp.tril(jnp.ones((tb, tb), F32))
    return pl.pallas_call(
        _cum_kernel,
        grid=(t // tb,),
        in_specs=[pl.BlockSpec((tb, n), lambda i: (i, 0)),
                  pl.BlockSpec((1, n), lambda i: (0, 0)),
                  pl.BlockSpec((tb, tb), lambda i: (0, 0))],
        out_specs=pl.BlockSpec((tb, n), lambda i: (i, 0)),
        out_shape=jax.ShapeDtypeStruct((t, n), F32),
        scratch_shapes=[pltpu.VMEM((1, n), F32)],
        compiler_params=_cparams(("arbitrary",)),
        name="cum_log_forget",
    )(f, b, tri)


def _fox_kernel(q_ref, qx_ref, ka_ref, vt_ref, o_ref, m_ref, l_ref, acc_ref, *, tq, tk):
    i = pl.program_id(1)
    qa = jnp.concatenate([q_ref[...], qx_ref[...]], axis=1)
    _flash_init(m_ref, l_ref, acc_ref)

    def chunk(c):
        off = pl.multiple_of(c * tk, tk)
        return _dot_nt(ka_ref[pl.ds(off, tk), :], qa), vt_ref[:, pl.ds(off, tk)]

    def body(c, carry):
        s_t, v_t = chunk(c)
        _flash_update(s_t, v_t, m_ref, l_ref, acc_ref)
        return carry

    n_full = i * (tq // tk)
    lax.fori_loop(0, n_full, body, 0)
    key = lax.broadcasted_iota(jnp.int32, (tk, tq), 0)
    qry = lax.broadcasted_iota(jnp.int32, (tk, tq), 1)
    for d in range(tq // tk):
        s_t, v_t = chunk(n_full + d)
        _flash_update(jnp.where(key + d * tk <= qry, s_t, NEG_INF), v_t, m_ref, l_ref, acc_ref)
    o_ref[...] = (acc_ref[...] / l_ref[...]).T.astype(o_ref.dtype)


def _fox_attention(qkvh, qx, ka, vt):
    t = qkvh.shape[1]
    tq = min(1024, t)
    tk = min(512, t)
    return pl.pallas_call(
        functools.partial(_fox_kernel, tq=tq, tk=tk),
        grid=(N_HEADS, t // tq),
        in_specs=[pl.BlockSpec((None, tq, HEAD_DIM), lambda h, i: (h, i, 0)),
                  pl.BlockSpec((None, tq, LANE), lambda h, i: (h, i, 0)),
                  pl.BlockSpec((None, t, 2 * HEAD_DIM), lambda h, i: (h, 0, 0)),
                  pl.BlockSpec((None, HEAD_DIM, t), lambda h, i: (h, 0, 0))],
        out_specs=pl.BlockSpec((tq, HEAD_DIM), lambda h, i: (i, h)),
        out_shape=jax.ShapeDtypeStruct((t, D_MODEL), BF16),
        scratch_shapes=[pltpu.VMEM((1, tq), F32), pltpu.VMEM((1, tq), F32),
                        pltpu.VMEM((HEAD_DIM, tq), F32)],
        compiler_params=_cparams(("parallel", "arbitrary")),
        name="fox_attention",
    )(qkvh, qx, ka, vt)


def _fox_layer(h, g_norm, w_in, b_f, w_o):
    t = h.shape[0]
    a = _rmsnorm(h, g_norm, BF16)
    n_qkv = 3 * D_MODEL
    qkvh = _proj(a, w_in[:, :n_qkv].astype(BF16), out_dtype=BF16, head_major=True, tn=512, n_scaled=D_MODEL)
    w_f = jnp.pad(w_in[:, n_qkv:], ((0, 0), (0, LANE - N_HEADS))).astype(BF16)
    f = _proj(a, w_f, out_dtype=F32, head_major=False, tn=LANE)
    b = jnp.pad(b_f.astype(F32), (0, LANE - N_HEADS)).reshape(1, LANE)
    cum = _cum_log_forget(f, b)[:, :N_HEADS].T
    pieces = jnp.stack(_split3_trunc(cum), axis=-1)
    ones = jnp.ones_like(pieces)
    pad = jnp.zeros((N_HEADS, t, LANE - 6), BF16)
    qx = jnp.concatenate([pieces, -ones, pad], axis=-1)
    kx = jnp.concatenate([ones, pieces, pad], axis=-1)
    ka = jnp.concatenate([qkvh[N_HEADS:2 * N_HEADS], kx], axis=-1)
    vt = jnp.swapaxes(qkvh[2 * N_HEADS:], 1, 2)
    o = _fox_attention(qkvh, qx, ka, vt)
    return _mm_res(o, w_o.astype(BF16), h, tm=1024, tn=512)


def _rel_bucket_const(dist):
    n = jnp.maximum(dist, 0)
    max_exact = REL_BUCKETS // 2
    nf = jnp.maximum(n, max_exact).astype(F32)
    large = max_exact + (jnp.log(nf / max_exact) / math.log(REL_MAX_DIST / max_exact)
                         * (REL_BUCKETS - max_exact)).astype(jnp.int32)
    large = jnp.minimum(large, REL_BUCKETS - 1)
    return jnp.where(n < max_exact, n, large)


def _bias_kernel(tab_ref, bkt_ref, o_ref):
    h = pl.program_id(0)
    bkt = bkt_ref[...]
    acc = jnp.zeros(bkt.shape, F32)
    for b in range(REL_BUCKETS):
        acc = jnp.where(bkt == b, tab_ref[b, h], acc)
    o_ref[...] = acc - tab_ref[REL_BUCKETS - 1, h]


def _bias_template(rel_table, bkt):
    r, c = bkt.shape
    return pl.pallas_call(
        _bias_kernel,
        grid=(N_HEADS,),
        in_specs=[pl.BlockSpec(memory_space=pltpu.SMEM),
                  pl.BlockSpec((r, c), lambda h: (0, 0))],
        out_specs=pl.BlockSpec((r, c), lambda h: (0, h)),
        out_shape=jax.ShapeDtypeStruct((r, N_HEADS * c), F32),
        compiler_params=_cparams(("arbitrary",)),
        name="bias_template",
    )(rel_table, bkt)


def _nsa_bias_templates(rel_table):
    tl = jnp.arange(Q_BLOCK)[None, :]
    bd = _bias_template(rel_table, _rel_bucket_const(tl + Q_BLOCK - jnp.arange(2 * Q_BLOCK)[:, None]))
    m = jnp.arange(CMP_NEAR)[:, None]
    bc = _bias_template(rel_table, _rel_bucket_const(
        tl - CMP_STRIDE * (m - (CMP_NEAR - 8)) - (CMP_BLOCK - 1)))
    return bd, bc


def _gelu_tanh(x):
    return 0.5 * x * (1.0 + jnp.tanh(math.sqrt(2.0 / math.pi) * (x + 0.044715 * (x * x * x))))


def _cmp_kernel(kb_ref, pos_ref, w1_ref, w2_ref, o_ref, *, nb):
    half = CMP_STRIDE * HEAD_DIM
    kb = kb_ref[...].astype(F32)
    xa = (kb + pos_ref[0:1, :]).astype(BF16)
    xb = (kb + pos_ref[1:2, :]).astype(BF16)
    a = _dot(xa, w1_ref[0:half, :])
    b = _dot(xb, w1_ref[half:2 * half, :])
    pre = a + pltpu.roll(b, nb - 1, axis=0)
    out = _dot(_gelu_tanh(pre).astype(BF16), w2_ref[...])
    row = lax.broadcasted_iota(jnp.int32, out.shape, 0)
    out = jnp.where(row < nb - 1, out, 0.0)
    o_ref[0:CMP_PAD, :] = jnp.zeros((CMP_PAD, HEAD_DIM), F32)
    o_ref[CMP_PAD:CMP_PAD + nb, :] = out


def _compress(kvh, cmp_pos, cmp_w1, cmp_w2):
    t = kvh.shape[1]
    nb = t // CMP_STRIDE
    kb = kvh[:2 * NSA_GROUPS].reshape(2 * NSA_GROUPS, nb, CMP_STRIDE * HEAD_DIM)
    pos = cmp_pos.astype(F32).reshape(2, 2, CMP_STRIDE * HEAD_DIM)
    return pl.pallas_call(
        functools.partial(_cmp_kernel, nb=nb),
        grid=(2 * NSA_GROUPS,),
        in_specs=[pl.BlockSpec((None, nb, CMP_STRIDE * HEAD_DIM), lambda j: (j, 0, 0)),
                  pl.BlockSpec((None, 2, CMP_STRIDE * HEAD_DIM), lambda j: (j // NSA_GROUPS, 0, 0)),
                  pl.BlockSpec((None, CMP_BLOCK * HEAD_DIM, HEAD_DIM), lambda j: (j // NSA_GROUPS, 0, 0)),
                  pl.BlockSpec((None, HEAD_DIM, HEAD_DIM), lambda j: (j // NSA_GROUPS, 0, 0))],
        out_specs=pl.BlockSpec((None, CMP_PAD + nb, HEAD_DIM), lambda j: (j, 0, 0)),
        out_shape=jax.ShapeDtypeStruct((2 * NSA_GROUPS, CMP_PAD + nb, HEAD_DIM), F32),
        compiler_params=_cparams(("parallel",)),
        name="compress",
    )(kb, pos, cmp_w1.astype(BF16), cmp_w2.astype(BF16))


def _dot_split_rhs(w, x):
    hi, mid, lo = _split3(x)
    return _dot(w, hi) + _dot(w, mid) + _dot(w, lo)


def _nsa_kernel(q_ref, g_ref, kc_ref, vc_ref, vct_ref, ksa_ref, vst_ref, kw_ref, vwt_ref,
                bd_ref, bc_ref, ovl_ref, o_ref,
                qa_ref, m_ref, l_ref, acc_ref, out_ref, *, ncp):
    i = pl.program_id(1)
    rows = NSA_ROWS
    q = q_ref[...].reshape(rows, HEAD_DIM)
    tl_lane = lax.broadcasted_iota(jnp.int32, (Q_BLOCK, rows), 1) % Q_BLOCK
    key_row = lax.broadcasted_iota(jnp.int32, (Q_BLOCK, rows), 0)
    sig_t = jax.nn.sigmoid(g_ref[...]).T

    def gate(branch):
        return jnp.concatenate([sig_t[3 * hg + branch:3 * hg + branch + 1, :] for hg in range(NSA_HG)], axis=1)

    def head_sum(p):
        acc = p[:, 0:Q_BLOCK]
        for hg in range(1, NSA_HG):
            acc = acc + p[:, hg * Q_BLOCK:(hg + 1) * Q_BLOCK]
        return acc

    n_first_near = 8 * i - (CMP_NEAR - 8)
    kcf = kc_ref[CMP_PAD:CMP_PAD + ncp, :].astype(BF16)
    far_ok = lax.broadcasted_iota(jnp.int32, (ncp, rows), 0) < n_first_near
    s_far = jnp.where(far_ok, _dot_nt(kcf, q), NEG_INF)
    start = pl.multiple_of(8 * i + 8 + CMP_PAD - CMP_NEAR, 8)
    kcn = kc_ref[pl.ds(start, CMP_NEAR), :].astype(BF16)
    vcn_t = vc_ref[pl.ds(start, CMP_NEAR), :].T.astype(BF16)
    d_near = tl_lane - CMP_STRIDE * (key_row - (CMP_NEAR - 8)) - (CMP_BLOCK - 1)
    near_ok = (d_near >= 0) & (n_first_near + key_row >= 0)
    s_near = jnp.where(near_ok, _dot_nt(kcn, q) + bc_ref[...], NEG_INF)
    mx = jnp.maximum(jnp.max(s_far, axis=0, keepdims=True), jnp.max(s_near, axis=0, keepdims=True))
    e_far = jnp.exp(s_far - mx)
    e_near = jnp.exp(s_near - mx)
    den = jnp.sum(e_far, axis=0, keepdims=True) + jnp.sum(e_near, axis=0, keepdims=True)
    inv = jnp.where(mx > 0.5 * NEG_INF, 1.0 / den, 0.0)
    p_far = e_far * inv
    p_near = e_near * inv
    o_c = (_dot(vct_ref[:, CMP_PAD:CMP_PAD + ncp].astype(BF16), p_far.astype(BF16))
           + _dot(vcn_t, p_near.astype(BF16)))
    out_ref[...] = gate(0) * o_c

    sb = lax.broadcasted_iota(jnp.int32, (SEL_COLS, CMP_NEAR), 0)
    nn = n_first_near + lax.broadcasted_iota(jnp.int32, (SEL_COLS, CMP_NEAR), 1)
    ovl_near = jnp.where((nn >= 4 * sb - 1) & (nn <= 4 * sb + 3) & (nn >= 0), 1.0, 0.0).astype(BF16)
    imp = _dot_split_rhs(ovl_ref[...], head_sum(p_far)) + _dot_split_rhs(ovl_near, head_sum(p_near))
    blk = lax.broadcasted_iota(jnp.int32, (SEL_COLS, Q_BLOCK), 0)
    tl = lax.broadcasted_iota(jnp.int32, (SEL_COLS, Q_BLOCK), 1)
    cur = 2 * i + (tl >= SEL_BLOCK).astype(jnp.int32)
    forced = (blk == 0) | (blk == cur) | (blk == cur - 1)
    causal_blk = blk * SEL_BLOCK <= Q_BLOCK * i + tl
    work = jnp.where(forced, FORCED_SCORE, jnp.where(causal_blk, imp, -1.0))
    blk_f = blk.astype(F32)
    sel = jnp.zeros((SEL_COLS, Q_BLOCK), F32)
    for _ in range(SEL_TOPK):
        best = jnp.max(work, axis=0, keepdims=True)
        first = jnp.min(jnp.where(work == best, blk_f, float(SEL_COLS)), axis=0, keepdims=True)
        pick = blk_f == first
        sel = jnp.where(pick, 1.0, sel)
        work = jnp.where(pick, -2.0, work)
    amask = jnp.where(sel > 0.0, 0.0, NEG_INF)
    for half in range(SEL_COLS // LANE):
        a_t = amask[half * LANE:(half + 1) * LANE, :].T.astype(BF16)
        qa_ref[half, :, 0:HEAD_DIM] = q
        qa_ref[half, :, HEAD_DIM:2 * HEAD_DIM] = jnp.concatenate([a_t] * NSA_HG, axis=0)

    def sel_chunk(tile, n_keys):
        off = pl.multiple_of(tile * Q_BLOCK, Q_BLOCK)
        qa = qa_ref[tile // (LANE // 2)]
        return _dot_nt(ksa_ref[pl.ds(off, n_keys), :], qa), vst_ref[:, pl.ds(off, n_keys)]

    _flash_init(m_ref, l_ref, acc_ref)
    s_t, v_t = sel_chunk(i, Q_BLOCK)
    _flash_update(jnp.where(key_row <= tl_lane, s_t + bd_ref[Q_BLOCK:, :], NEG_INF), v_t, m_ref, l_ref, acc_ref)
    s_t, v_t = sel_chunk(jnp.maximum(i - 1, 0), Q_BLOCK)
    _flash_update(jnp.where(i >= 1, s_t + bd_ref[:Q_BLOCK, :], NEG_INF), v_t, m_ref, l_ref, acc_ref)
    n_far = jnp.maximum(i - 1, 0)
    s_t, v_t = sel_chunk(jnp.maximum(i - 2, 0), Q_BLOCK)
    _flash_update(jnp.where(n_far % 2 == 1, s_t, NEG_INF), v_t, m_ref, l_ref, acc_ref)

    def far_body(c, carry):
        s_t, v_t = sel_chunk(2 * c, 2 * Q_BLOCK)
        _flash_update(s_t, v_t, m_ref, l_ref, acc_ref)
        return carry

    lax.fori_loop(0, n_far // 2, far_body, 0)
    out_ref[...] += gate(1) * (acc_ref[...] / l_ref[...])

    def win_chunk(tile):
        off = pl.multiple_of(jnp.maximum(tile, 0) * Q_BLOCK, Q_BLOCK)
        return _dot_nt(kw_ref[pl.ds(off, Q_BLOCK), :], q), vwt_ref[:, pl.ds(off, Q_BLOCK)]

    _flash_init(m_ref, l_ref, acc_ref)
    s_t, v_t = win_chunk(i)
    _flash_update(jnp.where(key_row <= tl_lane, s_t + bd_ref[Q_BLOCK:, :], NEG_INF), v_t, m_ref, l_ref, acc_ref)
    s_t, v_t = win_chunk(i - 1)
    _flash_update(jnp.where(i >= 1, s_t + bd_ref[:Q_BLOCK, :], NEG_INF), v_t, m_ref, l_ref, acc_ref)
    for back in (2, 3):
        s_t, v_t = win_chunk(i - back)
        _flash_update(jnp.where(i >= back, s_t, NEG_INF), v_t, m_ref, l_ref, acc_ref)
    s_t, v_t = win_chunk(i - 4)
    _flash_update(jnp.where((key_row > tl_lane) & (i >= 4), s_t, NEG_INF), v_t, m_ref, l_ref, acc_ref)
    o = out_ref[...] + gate(2) * (acc_ref[...] / l_ref[...])
    for hg in range(NSA_HG):
        o_ref[:, hg * HEAD_DIM:(hg + 1) * HEAD_DIM] = o[:, hg * Q_BLOCK:(hg + 1) * Q_BLOCK].T.astype(o_ref.dtype)


def _nsa_attention(qkvh, gates, kcv, kcv_t, ksa, vst, vwt, bd, bc, ovl):
    t = qkvh.shape[1]
    ncp = t // CMP_STRIDE
    kv0 = N_HEADS
    resident = functools.partial(pl.BlockSpec, pipeline_mode=pl.Buffered(1))
    return pl.pallas_call(
        functools.partial(_nsa_kernel, ncp=ncp),
        grid=(NSA_GROUPS, t // Q_BLOCK),
        in_specs=[pl.BlockSpec((NSA_HG, Q_BLOCK, HEAD_DIM), lambda g, i: (g, i, 0)),
                  pl.BlockSpec((None, Q_BLOCK, LANE), lambda g, i: (g, i, 0)),
                  resident((None, CMP_PAD + ncp, HEAD_DIM), lambda g, i: (g, 0, 0)),
                  resident((None, CMP_PAD + ncp, HEAD_DIM), lambda g, i: (NSA_GROUPS + g, 0, 0)),
                  resident((None, HEAD_DIM, CMP_PAD + ncp), lambda g, i: (NSA_GROUPS + g, 0, 0)),
                  resident((None, t, 2 * HEAD_DIM), lambda g, i: (g, 0, 0)),
                  resident((None, HEAD_DIM, t), lambda g, i: (g, 0, 0)),
                  resident((None, t, HEAD_DIM), lambda g, i: (kv0 + 4 * NSA_GROUPS + g, 0, 0)),
                  resident((None, HEAD_DIM, t), lambda g, i: (g, 0, 0)),
                  resident((2 * Q_BLOCK, NSA_ROWS), lambda g, i: (0, g)),
                  resident((CMP_NEAR, NSA_ROWS), lambda g, i: (0, g)),
                  resident((SEL_COLS, ncp), lambda g, i: (0, 0))],
        out_specs=pl.BlockSpec((Q_BLOCK, NSA_HG * HEAD_DIM), lambda g, i: (i, g)),
        out_shape=jax.ShapeDtypeStruct((t, D_MODEL), BF16),
        scratch_shapes=[pltpu.VMEM((SEL_COLS // LANE, NSA_ROWS, 2 * HEAD_DIM), BF16),
                        pltpu.VMEM((1, NSA_ROWS), F32), pltpu.VMEM((1, NSA_ROWS), F32),
                        pltpu.VMEM((HEAD_DIM, NSA_ROWS), F32), pltpu.VMEM((HEAD_DIM, NSA_ROWS), F32)],
        compiler_params=_cparams(("parallel", "arbitrary")),
        name="nsa_attention",
    )(qkvh, gates, kcv, kcv, kcv_t, ksa, vst, qkvh, vwt, bd, bc, ovl)


def _nsa_layer(h, g_norm, w_in, w_o, cmp_pos, cmp_w1, cmp_w2, bd, bc):
    t = h.shape[0]
    assert t // SEL_BLOCK <= SEL_COLS
    a = _rmsnorm(h, g_norm, BF16)
    n_qkv = D_MODEL + 6 * NSA_KV_DIM
    qkvh = _proj(a, w_in[:, :n_qkv].astype(BF16), out_dtype=BF16, head_major=True, tn=512,
                 n_scaled=D_MODEL)
    w_g = w_in[:, n_qkv:].reshape(D_MODEL, NSA_GROUPS, 3 * NSA_HG)
    w_g = jnp.pad(w_g, ((0, 0), (0, 0), (0, LANE - 3 * NSA_HG))).reshape(D_MODEL, NSA_GROUPS * LANE)
    gates = _proj(a, w_g.astype(BF16), out_dtype=F32, head_major=True, tn=NSA_GROUPS * LANE)
    kvh = qkvh[N_HEADS:]
    kcv = _compress(kvh, cmp_pos, cmp_w1, cmp_w2)
    kcv_t = jnp.swapaxes(kcv, 1, 2)
    key_blk = (jnp.arange(t) // SEL_BLOCK) % LANE
    onehot = (key_blk[:, None] == jnp.arange(LANE)[None, :]).astype(BF16)
    ksa = jnp.concatenate([kvh[2 * NSA_GROUPS:3 * NSA_GROUPS],
                           jnp.broadcast_to(onehot, (NSA_GROUPS, t, LANE))], axis=-1)
    vst = jnp.swapaxes(kvh[3 * NSA_GROUPS:4 * NSA_GROUPS], 1, 2)
    vwt = jnp.swapaxes(kvh[5 * NSA_GROUPS:6 * NSA_GROUPS], 1, 2)
    n = jnp.arange(t // CMP_STRIDE)[None, :]
    sblk = jnp.arange(SEL_COLS)[:, None]
    ovl = ((n >= 4 * sblk - 1) & (n <= 4 * sblk + 3)).astype(BF16)
    o = _nsa_attention(qkvh, gates, kcv, kcv_t, ksa, vst, vwt, bd, bc, ovl)
    return _mm_res(o, w_o.astype(BF16), h, tm=1024, tn=512)


def kernel(x, norm_mix, norm_ffn, norm_final, rel_table, nsa_w_in, nsa_w_o, nsa_cmp_pos, nsa_cmp_w1,
           nsa_cmp_w2, fox_w_in, fox_b_f, fox_w_o, ffn_w_gate, ffn_w_up, ffn_w_down):
    b, t, d = x.shape
    depth = norm_mix.shape[0]
    bd, bc = _nsa_bias_templates(rel_table.astype(F32))
    outs = []
    for bi in range(b):
        h = x[bi]
        for i in range(depth):
            j = i // 2
            if i % 2 == 0:
                h = _nsa_layer(h, norm_mix[i], nsa_w_in[j], nsa_w_o[j], nsa_cmp_pos[j], nsa_cmp_w1[j],
                               nsa_cmp_w2[j], bd, bc)
            else:
                h = _fox_layer(h, norm_mix[i], fox_w_in[j], fox_b_f[j], fox_w_o[j])
            h = _ffn(h, norm_ffn[i], ffn_w_gate[i], ffn_w_up[i], ffn_w_down[i])
        outs.append(_rmsnorm(h, norm_final, F32))
    return jnp.stack(outs, axis=0)
```

```python
import functools
import math

import jax
import jax.numpy as jnp
from jax import lax
from jax.experimental import pallas as pl
from jax.experimental.pallas import tpu as pltpu

F32 = jnp.float32
BF16 = jnp.bfloat16

D_MODEL = 4096
HEAD_DIM = 128
N_HEADS = D_MODEL // HEAD_DIM
NSA_GROUPS = 4
NSA_HG = N_HEADS // NSA_GROUPS
NSA_KV_DIM = NSA_GROUPS * HEAD_DIM
CMP_BLOCK = 32
CMP_STRIDE = 16
SEL_BLOCK = 64
SEL_TOPK = 16
WINDOW = 512
REL_BUCKETS = 32
REL_MAX_DIST = 128
Q_BLOCK = 128
RMS_EPS = 1e-6
NEG_INF = -1e30
FORCED_SCORE = 1e6
LOG2E = 1.4426950408889634
Q_SCALE = HEAD_DIM ** -0.5 * LOG2E

LANE = 128
SEL_COLS = 256
CMP_PAD = 128
CMP_NEAR = 128
NSA_ROWS = NSA_HG * Q_BLOCK
VMEM_LIMIT = 56 * 1024 * 1024


def _cparams(sem):
    return pltpu.CompilerParams(dimension_semantics=sem, vmem_limit_bytes=VMEM_LIMIT)


def _dot(a, b):
    return jnp.dot(a, b, preferred_element_type=F32)


def _dot_nt(a, b):
    return lax.dot_general(a, b, (((1,), (1,)), ((), ())), preferred_element_type=F32)


def _rmsnorm_kernel(x_ref, g_ref, o_ref):
    x = x_ref[...]
    ms = jnp.mean(x * x, axis=-1, keepdims=True)
    o_ref[...] = (x * lax.rsqrt(ms + RMS_EPS) * g_ref[...]).astype(o_ref.dtype)


def _rmsnorm(x, g, out_dtype):
    t, d = x.shape
    tm = min(256, t)
    return pl.pallas_call(
        _rmsnorm_kernel,
        grid=(t // tm,),
        in_specs=[pl.BlockSpec((tm, d), lambda i: (i, 0)),
                  pl.BlockSpec((1, d), lambda i: (0, 0))],
        out_specs=pl.BlockSpec((tm, d), lambda i: (i, 0)),
        out_shape=jax.ShapeDtypeStruct((t, d), out_dtype),
        compiler_params=_cparams(("parallel",)),
        name="rmsnorm",
    )(x, g.reshape(1, d))


def _proj_kernel(x_ref, w_ref, o_ref, *, head_major, n_scaled_blocks):
    r = _dot(x_ref[...], w_ref[...])
    if n_scaled_blocks:
        r = r * jnp.where(pl.program_id(1) < n_scaled_blocks, Q_SCALE, 1.0)
    if head_major:
        for s in range(o_ref.shape[0]):
            o_ref[s] = r[:, s * LANE:(s + 1) * LANE].astype(o_ref.dtype)
    else:
        o_ref[...] = r.astype(o_ref.dtype)


def _proj(x, w, *, out_dtype, head_major, tn, n_scaled=0):
    t, k = x.shape
    n = w.shape[1]
    tm = min(1024, t)
    tn = min(tn, n)
    assert n_scaled % tn == 0
    if head_major:
        out_shape = jax.ShapeDtypeStruct((n // LANE, t, LANE), out_dtype)
        out_spec = pl.BlockSpec((tn // LANE, tm, LANE), lambda i, j: (j, i, 0))
    else:
        out_shape = jax.ShapeDtypeStruct((t, n), out_dtype)
        out_spec = pl.BlockSpec((tm, tn), lambda i, j: (i, j))
    return pl.pallas_call(
        functools.partial(_proj_kernel, head_major=head_major, n_scaled_blocks=n_scaled // tn),
        grid=(t // tm, n // tn),
        in_specs=[pl.BlockSpec((tm, k), lambda i, j: (i, 0)),
                  pl.BlockSpec((k, tn), lambda i, j: (0, j))],
        out_specs=out_spec,
        out_shape=out_shape,
        compiler_params=_cparams(("parallel", "parallel")),
        name="proj",
    )(x, w)


def _mm_res_kernel(x_ref, w_ref, r_ref, o_ref):
    o_ref[...] = r_ref[...] + _dot(x_ref[...], w_ref[...])


def _mm_res(x, w, res, *, tm, tn):
    t, k = x.shape
    n = w.shape[1]
    tm = min(tm, t)
    return pl.pallas_call(
        _mm_res_kernel,
        grid=(t // tm, n // tn),
        in_specs=[pl.BlockSpec((tm, k), lambda i, j: (i, 0)),
                  pl.BlockSpec((k, tn), lambda i, j: (0, j)),
                  pl.BlockSpec((tm, tn), lambda i, j: (i, j))],
        out_specs=pl.BlockSpec((tm, tn), lambda i, j: (i, j)),
        out_shape=jax.ShapeDtypeStruct((t, n), F32),
        compiler_params=_cparams(("parallel", "parallel")),
        name="mm_res",
    )(x, w, res)


def _gateup_kernel(x_ref, wg_ref, wu_ref, o_ref):
    x = x_ref[...]
    g = _dot(x, wg_ref[...])
    u = _dot(x, wu_ref[...])
    o_ref[...] = (g * jax.nn.sigmoid(g) * u).astype(o_ref.dtype)


def _gateup(x, wg, wu, *, tn):
    t, k = x.shape
    n = wg.shape[1]
    tm = min(1024, t)
    return pl.pallas_call(
        _gateup_kernel,
        grid=(t // tm, n // tn),
        in_specs=[pl.BlockSpec((tm, k), lambda i, j: (i, 0)),
                  pl.BlockSpec((k, tn), lambda i, j: (0, j)),
                  pl.BlockSpec((k, tn), lambda i, j: (0, j))],
        out_specs=pl.BlockSpec((tm, tn), lambda i, j: (i, j)),
        out_shape=jax.ShapeDtypeStruct((t, n), BF16),
        compiler_params=_cparams(("parallel", "parallel")),
        name="gateup",
    )(x, wg, wu)


def _ffn(h, g_norm, w_gate, w_up, w_down):
    a = _rmsnorm(h, g_norm, BF16)
    h1 = _gateup(a, w_gate.astype(BF16), w_up.astype(BF16), tn=256)
    return _mm_res(h1, w_down.astype(BF16), h, tm=512, tn=256)


def _flash_init(m_ref, l_ref, acc_ref):
    m_ref[...] = jnp.full(m_ref.shape, NEG_INF, F32)
    l_ref[...] = jnp.zeros(l_ref.shape, F32)
    acc_ref[...] = jnp.zeros(acc_ref.shape, F32)


def _flash_update(s_t, v_t, m_ref, l_ref, acc_ref):
    m_prev = m_ref[...]
    m_new = jnp.maximum(m_prev, jnp.max(s_t, axis=0, keepdims=True))
    alpha = jnp.exp2(m_prev - m_new)
    p = jnp.exp2(s_t - m_new)
    l_ref[...] = alpha * l_ref[...] + jnp.sum(p, axis=0, keepdims=True)
    acc_ref[...] = alpha * acc_ref[...] + _dot(v_t, p.astype(BF16))
    m_ref[...] = m_new


def _split3(x):
    hi = x.astype(BF16)
    r = x - hi.astype(F32)
    mid = r.astype(BF16)
    lo = (r - mid.astype(F32)).astype(BF16)
    return hi, mid, lo


def _split3_trunc(x):
    def top(v):
        bits = lax.bitcast_convert_type(v, jnp.uint32) & jnp.uint32(0xFFFF0000)
        return lax.bitcast_convert_type(bits, F32)
    hi = top(x)
    mid = top(x - hi)
    lo = x - hi - mid
    return hi.astype(BF16), mid.astype(BF16), lo.astype(BF16)


def _cum_kernel(f_ref, b_ref, tri_ref, o_ref, carry_ref):
    @pl.when(pl.program_id(0) == 0)
    def _():
        carry_ref[...] = jnp.zeros_like(carry_ref)

    x = f_ref[...] + b_ref[...]
    log_f = jnp.minimum(x, 0.0) - jnp.log1p(jnp.exp(-jnp.abs(x)))
    c = jnp.dot(tri_ref[...], log_f, precision=lax.Precision.HIGHEST,
                preferred_element_type=F32) + carry_ref[...]
    o_ref[...] = c
    carry_ref[...] = c[c.shape[0] - 1:, :]


def _cum_log_forget(f, b):
    t, n = f.shape
    tb = min(512, t)
    tri = jnp.tril(jnp.ones((tb, tb), F32))
    return pl.pallas_call(
        _cum_kernel,
        grid=(t // tb,),
        in_specs=[pl.BlockSpec((tb, n), lambda i: (i, 0)),
                  pl.BlockSpec((1, n), lambda i: (0, 0)),
                  pl.BlockSpec((tb, tb), lambda i: (0, 0))],
        out_specs=pl.BlockSpec((tb, n), lambda i: (i, 0)),
        out_shape=jax.ShapeDtypeStruct((t, n), F32),
        scratch_shapes=[pltpu.VMEM((1, n), F32)],
        compiler_params=_cparams(("arbitrary",)),
        name="cum_log_forget",
    )(f, b, tri)


def _pipelined_pairs(n_pairs, scores, consume, s_ref):
    def body(c2, carry):
        c = 2 * c2
        s_next = scores(c + 1)
        consume(c, s_ref[0])
        s_ref[1] = s_next
        s_next = scores(c + 2)
        consume(c + 1, s_ref[1])
        s_ref[0] = s_next
        return carry

    lax.fori_loop(0, n_pairs, body, 0)


def _fox_kernel(q_ref, qx_ref, ka_ref, vt_ref, o_ref, qa_ref, s_ref, m_ref, l_ref, acc_ref, *, tq, tk):
    i = pl.program_id(1)
    assert tq == 2 * tk
    qa_ref[:, 0:HEAD_DIM] = q_ref[...]
    qa_ref[:, HEAD_DIM:] = qx_ref[...]
    _flash_init(m_ref, l_ref, acc_ref)

    def scores(c):
        off = pl.multiple_of(c * tk, tk)
        return _dot_nt(ka_ref[pl.ds(off, tk), :], qa_ref[...])

    def consume(c, s_t):
        off = pl.multiple_of(c * tk, tk)
        _flash_update(s_t, vt_ref[:, pl.ds(off, tk)], m_ref, l_ref, acc_ref)

    s_ref[0] = scores(0)
    _pipelined_pairs(i, scores, consume, s_ref)
    key = lax.broadcasted_iota(jnp.int32, (tk, tq), 0)
    qry = lax.broadcasted_iota(jnp.int32, (tk, tq), 1)
    s_last = scores(2 * i + 1)
    consume(2 * i, jnp.where(key <= qry, s_ref[0], NEG_INF))
    consume(2 * i + 1, jnp.where(key + tk <= qry, s_last, NEG_INF))
    o_ref[...] = (acc_ref[...] / l_ref[...]).T.astype(o_ref.dtype)


def _fox_attention(qkvh, qx, ka, vt):
    t = qkvh.shape[1]
    tq = min(1024, t)
    tk = min(512, t)
    return pl.pallas_call(
        functools.partial(_fox_kernel, tq=tq, tk=tk),
        grid=(N_HEADS, t // tq),
        in_specs=[pl.BlockSpec((None, tq, HEAD_DIM), lambda h, i: (h, i, 0)),
                  pl.BlockSpec((None, tq, LANE), lambda h, i: (h, i, 0)),
                  pl.BlockSpec((None, t, 2 * HEAD_DIM), lambda h, i: (h, 0, 0)),
                  pl.BlockSpec((None, HEAD_DIM, t), lambda h, i: (h, 0, 0))],
        out_specs=pl.BlockSpec((tq, HEAD_DIM), lambda h, i: (i, h)),
        out_shape=jax.ShapeDtypeStruct((t, D_MODEL), BF16),
        scratch_shapes=[pltpu.VMEM((tq, 2 * HEAD_DIM), BF16), pltpu.VMEM((2, tk, tq), F32),
                        pltpu.VMEM((1, tq), F32), pltpu.VMEM((1, tq), F32),
                        pltpu.VMEM((HEAD_DIM, tq), F32)],
        compiler_params=_cparams(("parallel", "arbitrary")),
        name="fox_attention",
    )(qkvh, qx, ka, vt)


def _fox_layer(h, g_norm, w_in, b_f, w_o):
    t = h.shape[0]
    a = _rmsnorm(h, g_norm, BF16)
    n_qkv = 3 * D_MODEL
    qkvh = _proj(a, w_in[:, :n_qkv].astype(BF16), out_dtype=BF16, head_major=True, tn=512, n_scaled=D_MODEL)
    w_f = jnp.pad(w_in[:, n_qkv:], ((0, 0), (0, LANE - N_HEADS))).astype(BF16)
    f = _proj(a, w_f, out_dtype=F32, head_major=False, tn=LANE)
    b = jnp.pad(b_f.astype(F32), (0, LANE - N_HEADS)).reshape(1, LANE)
    cum = _cum_log_forget(f, b)[:, :N_HEADS].T
    pieces = jnp.stack(_split3_trunc(cum * LOG2E), axis=-1)
    ones = jnp.ones_like(pieces)
    pad = jnp.zeros((N_HEADS, t, LANE - 6), BF16)
    qx = jnp.concatenate([pieces, -ones, pad], axis=-1)
    kx = jnp.concatenate([ones, pieces, pad], axis=-1)
    ka = jnp.concatenate([qkvh[N_HEADS:2 * N_HEADS], kx], axis=-1)
    vt = jnp.swapaxes(qkvh[2 * N_HEADS:], 1, 2)
    o = _fox_attention(qkvh, qx, ka, vt)
    return _mm_res(o, w_o.astype(BF16), h, tm=1024, tn=512)


def _rel_bucket_const(dist):
    n = jnp.maximum(dist, 0)
    max_exact = REL_BUCKETS // 2
    nf = jnp.maximum(n, max_exact).astype(F32)
    large = max_exact + (jnp.log(nf / max_exact) / math.log(REL_MAX_DIST / max_exact)
                         * (REL_BUCKETS - max_exact)).astype(jnp.int32)
    large = jnp.minimum(large, REL_BUCKETS - 1)
    return jnp.where(n < max_exact, n, large)


def _bias_kernel(tab_ref, bkt_ref, o_ref):
    h = pl.program_id(0)
    bkt = bkt_ref[...]
    acc = jnp.zeros(bkt.shape, F32)
    for b in range(REL_BUCKETS):
        acc = jnp.where(bkt == b, tab_ref[b, h], acc)
    o_ref[...] = (acc - tab_ref[REL_BUCKETS - 1, h]) * LOG2E


def _bias_template(rel_table, bkt):
    r, c = bkt.shape
    return pl.pallas_call(
        _bias_kernel,
        grid=(N_HEADS,),
        in_specs=[pl.BlockSpec(memory_space=pltpu.SMEM),
                  pl.BlockSpec((r, c), lambda h: (0, 0))],
        out_specs=pl.BlockSpec((r, c), lambda h: (0, h)),
        out_shape=jax.ShapeDtypeStruct((r, N_HEADS * c), F32),
        compiler_params=_cparams(("arbitrary",)),
        name="bias_template",
    )(rel_table, bkt)


def _nsa_bias_templates(rel_table):
    tl = jnp.arange(Q_BLOCK)[None, :]
    bd = _bias_template(rel_table, _rel_bucket_const(tl + Q_BLOCK - jnp.arange(2 * Q_BLOCK)[:, None]))
    m = jnp.arange(CMP_NEAR)[:, None]
    bc = _bias_template(rel_table, _rel_bucket_const(
        tl - CMP_STRIDE * (m - (CMP_NEAR - 8)) - (CMP_BLOCK - 1)))
    return bd, bc


def _gelu_tanh(x):
    return 0.5 * x * (1.0 + jnp.tanh(math.sqrt(2.0 / math.pi) * (x + 0.044715 * (x * x * x))))


def _cmp_kernel(kb_ref, pos_ref, w1_ref, w2_ref, o_ref, *, nb):
    half = CMP_STRIDE * HEAD_DIM
    kb = kb_ref[...].astype(F32)
    xa = (kb + pos_ref[0:1, :]).astype(BF16)
    xb = (kb + pos_ref[1:2, :]).astype(BF16)
    a = _dot(xa, w1_ref[0:half, :])
    b = _dot(xb, w1_ref[half:2 * half, :])
    pre = a + pltpu.roll(b, nb - 1, axis=0)
    out = _dot(_gelu_tanh(pre).astype(BF16), w2_ref[...])
    row = lax.broadcasted_iota(jnp.int32, out.shape, 0)
    out = jnp.where(row < nb - 1, out, 0.0)
    o_ref[0:CMP_PAD, :] = jnp.zeros((CMP_PAD, HEAD_DIM), F32)
    o_ref[CMP_PAD:CMP_PAD + nb, :] = out


def _compress(kvh, cmp_pos, cmp_w1, cmp_w2):
    t = kvh.shape[1]
    nb = t // CMP_STRIDE
    kb = kvh[:2 * NSA_GROUPS].reshape(2 * NSA_GROUPS, nb, CMP_STRIDE * HEAD_DIM)
    pos = cmp_pos.astype(F32).reshape(2, 2, CMP_STRIDE * HEAD_DIM)
    return pl.pallas_call(
        functools.partial(_cmp_kernel, nb=nb),
        grid=(2 * NSA_GROUPS,),
        in_specs=[pl.BlockSpec((None, nb, CMP_STRIDE * HEAD_DIM), lambda j: (j, 0, 0)),
                  pl.BlockSpec((None, 2, CMP_STRIDE * HEAD_DIM), lambda j: (j // NSA_GROUPS, 0, 0)),
                  pl.BlockSpec((None, CMP_BLOCK * HEAD_DIM, HEAD_DIM), lambda j: (j // NSA_GROUPS, 0, 0)),
                  pl.BlockSpec((None, HEAD_DIM, HEAD_DIM), lambda j: (j // NSA_GROUPS, 0, 0))],
        out_specs=pl.BlockSpec((None, CMP_PAD + nb, HEAD_DIM), lambda j: (j, 0, 0)),
        out_shape=jax.ShapeDtypeStruct((2 * NSA_GROUPS, CMP_PAD + nb, HEAD_DIM), F32),
        compiler_params=_cparams(("parallel",)),
        name="compress",
    )(kb, pos, cmp_w1.astype(BF16), cmp_w2.astype(BF16))


def _dot_split_rhs(w, x):
    hi, mid, lo = _split3(x)
    return _dot(w, hi) + _dot(w, mid) + _dot(w, lo)


def _nsa_kernel(q_ref, g_ref, kc_ref, vc_ref, vct_ref, ksa_ref, vst_ref, kw_ref, vwt_ref,
                bd_ref, bc_ref, ovl_ref, o_ref,
                qa_ref, s_ref, m_ref, l_ref, acc_ref, out_ref, *, ncp):
    i = pl.program_id(1)
    rows = NSA_ROWS
    q = q_ref[...].reshape(rows, HEAD_DIM)
    tl_lane = lax.broadcasted_iota(jnp.int32, (Q_BLOCK, rows), 1) % Q_BLOCK
    key_row = lax.broadcasted_iota(jnp.int32, (Q_BLOCK, rows), 0)
    sig_t = jax.nn.sigmoid(g_ref[...]).T

    def gate(branch):
        return jnp.concatenate([sig_t[3 * hg + branch:3 * hg + branch + 1, :] for hg in range(NSA_HG)], axis=1)

    def head_sum(p):
        acc = p[:, 0:Q_BLOCK]
        for hg in range(1, NSA_HG):
            acc = acc + p[:, hg * Q_BLOCK:(hg + 1) * Q_BLOCK]
        return acc

    n_first_near = 8 * i - (CMP_NEAR - 8)
    kcf = kc_ref[CMP_PAD:CMP_PAD + ncp, :].astype(BF16)
    far_ok = lax.broadcasted_iota(jnp.int32, (ncp, rows), 0) < n_first_near
    s_far = jnp.where(far_ok, _dot_nt(kcf, q), NEG_INF)
    start = pl.multiple_of(8 * i + 8 + CMP_PAD - CMP_NEAR, 8)
    kcn = kc_ref[pl.ds(start, CMP_NEAR), :].astype(BF16)
    vcn_t = vc_ref[pl.ds(start, CMP_NEAR), :].T.astype(BF16)
    d_near = tl_lane - CMP_STRIDE * (key_row - (CMP_NEAR - 8)) - (CMP_BLOCK - 1)
    near_ok = (d_near >= 0) & (n_first_near + key_row >= 0)
    s_near = jnp.where(near_ok, _dot_nt(kcn, q) + bc_ref[...], NEG_INF)
    mx = jnp.maximum(jnp.max(s_far, axis=0, keepdims=True), jnp.max(s_near, axis=0, keepdims=True))
    e_far = jnp.exp2(s_far - mx)
    e_near = jnp.exp2(s_near - mx)
    den = jnp.sum(e_far, axis=0, keepdims=True) + jnp.sum(e_near, axis=0, keepdims=True)
    inv = jnp.where(mx > 0.5 * NEG_INF, 1.0 / den, 0.0)
    p_far = e_far * inv
    p_near = e_near * inv
    o_c = (_dot(vct_ref[:, CMP_PAD:CMP_PAD + ncp].astype(BF16), p_far.astype(BF16))
           + _dot(vcn_t, p_near.astype(BF16)))
    out_ref[...] = gate(0) * o_c

    sb = lax.broadcasted_iota(jnp.int32, (SEL_COLS, CMP_NEAR), 0)
    nn = n_first_near + lax.broadcasted_iota(jnp.int32, (SEL_COLS, CMP_NEAR), 1)
    ovl_near = jnp.where((nn >= 4 * sb - 1) & (nn <= 4 * sb + 3) & (nn >= 0), 1.0, 0.0).astype(BF16)
    imp = _dot_split_rhs(ovl_ref[...], head_sum(p_far)) + _dot_split_rhs(ovl_near, head_sum(p_near))
    blk = lax.broadcasted_iota(jnp.int32, (SEL_COLS, Q_BLOCK), 0)
    tl = lax.broadcasted_iota(jnp.int32, (SEL_COLS, Q_BLOCK), 1)
    cur = 2 * i + (tl >= SEL_BLOCK).astype(jnp.int32)
    forced = (blk == 0) | (blk == cur) | (blk == cur - 1)
    causal_blk = blk * SEL_BLOCK <= Q_BLOCK * i + tl
    work = jnp.where(forced, FORCED_SCORE, jnp.where(causal_blk, imp, -1.0))
    blk_f = blk.astype(F32)
    sel = jnp.zeros((SEL_COLS, Q_BLOCK), F32)
    for _ in range(SEL_TOPK):
        best = jnp.max(work, axis=0, keepdims=True)
        first = jnp.min(jnp.where(work == best, blk_f, float(SEL_COLS)), axis=0, keepdims=True)
        pick = blk_f == first
        sel = jnp.where(pick, 1.0, sel)
        work = jnp.where(pick, -2.0, work)
    amask = jnp.where(sel > 0.0, 0.0, NEG_INF)
    for half in range(SEL_COLS // LANE):
        a_t = amask[half * LANE:(half + 1) * LANE, :].T.astype(BF16)
        qa_ref[half, :, 0:HEAD_DIM] = q
        qa_ref[half, :, HEAD_DIM:2 * HEAD_DIM] = jnp.concatenate([a_t] * NSA_HG, axis=0)

    def sel_chunk(tile, n_keys):
        off = pl.multiple_of(tile * Q_BLOCK, Q_BLOCK)
        qa = qa_ref[tile // (LANE // 2)]
        return _dot_nt(ksa_ref[pl.ds(off, n_keys), :], qa), vst_ref[:, pl.ds(off, n_keys)]

    _flash_init(m_ref, l_ref, acc_ref)
    s_t, v_t = sel_chunk(i, Q_BLOCK)
    _flash_update(jnp.where(key_row <= tl_lane, s_t + bd_ref[Q_BLOCK:, :], NEG_INF), v_t, m_ref, l_ref, acc_ref)
    s_t, v_t = sel_chunk(jnp.maximum(i - 1, 0), Q_BLOCK)
    _flash_update(jnp.where(i >= 1, s_t + bd_ref[:Q_BLOCK, :], NEG_INF), v_t, m_ref, l_ref, acc_ref)
    n_far = jnp.maximum(i - 1, 0)

    n_pairs = n_far // 2

    def far_scores(c):
        tile = 2 * jnp.minimum(c, jnp.maximum(n_pairs - 1, 0))
        off = pl.multiple_of(tile * Q_BLOCK, 2 * Q_BLOCK)
        return _dot_nt(ksa_ref[pl.ds(off, 2 * Q_BLOCK), :], qa_ref[tile // (LANE // 2)])

    def far_consume(c, s_t):
        off = pl.multiple_of(c * 2 * Q_BLOCK, 2 * Q_BLOCK)
        _flash_update(s_t, vst_ref[:, pl.ds(off, 2 * Q_BLOCK)], m_ref, l_ref, acc_ref)

    s_ref[0] = far_scores(0)
    _pipelined_pairs(n_pairs // 2, far_scores, far_consume, s_ref)

    @pl.when(n_pairs % 2 == 1)
    def _():
        far_consume(n_pairs - 1, s_ref[0])

    @pl.when(n_far % 2 == 1)
    def _():
        s_t, v_t = sel_chunk(n_far - 1, Q_BLOCK)
        _flash_update(s_t, v_t, m_ref, l_ref, acc_ref)

    out_ref[...] += gate(1) * (acc_ref[...] / l_ref[...])

    def win_chunk(tile):
        off = pl.multiple_of(jnp.maximum(tile, 0) * Q_BLOCK, Q_BLOCK)
        return _dot_nt(kw_ref[pl.ds(off, Q_BLOCK), :], q), vwt_ref[:, pl.ds(off, Q_BLOCK)]

    _flash_init(m_ref, l_ref, acc_ref)
    s_t, v_t = win_chunk(i)
    _flash_update(jnp.where(key_row <= tl_lane, s_t + bd_ref[Q_BLOCK:, :], NEG_INF), v_t, m_ref, l_ref, acc_ref)
    s_t, v_t = win_chunk(i - 1)
    _flash_update(jnp.where(i >= 1, s_t + bd_ref[:Q_BLOCK, :], NEG_INF), v_t, m_ref, l_ref, acc_ref)
    for back in (2, 3):
        s_t, v_t = win_chunk(i - back)
        _flash_update(jnp.where(i >= back, s_t, NEG_INF), v_t, m_ref, l_ref, acc_ref)
    s_t, v_t = win_chunk(i - 4)
    _flash_update(jnp.where((key_row > tl_lane) & (i >= 4), s_t, NEG_INF), v_t, m_ref, l_ref, acc_ref)
    o = out_ref[...] + gate(2) * (acc_ref[...] / l_ref[...])
    for hg in range(NSA_HG):
        o_ref[:, hg * HEAD_DIM:(hg + 1) * HEAD_DIM] = o[:, hg * Q_BLOCK:(hg + 1) * Q_BLOCK].T.astype(o_ref.dtype)


def _nsa_attention(qkvh, gates, kcv, kcv_t, ksa, vst, vwt, bd, bc, ovl):
    t = qkvh.shape[1]
    ncp = t // CMP_STRIDE
    kv0 = N_HEADS
    resident = functools.partial(pl.BlockSpec, pipeline_mode=pl.Buffered(1))
    return pl.pallas_call(
        functools.partial(_nsa_kernel, ncp=ncp),
        grid=(NSA_GROUPS, t // Q_BLOCK),
        in_specs=[pl.BlockSpec((NSA_HG, Q_BLOCK, HEAD_DIM), lambda g, i: (g, i, 0)),
                  pl.BlockSpec((None, Q_BLOCK, LANE), lambda g, i: (g, i, 0)),
                  resident((None, CMP_PAD + ncp, HEAD_DIM), lambda g, i: (g, 0, 0)),
                  resident((None, CMP_PAD + ncp, HEAD_DIM), lambda g, i: (NSA_GROUPS + g, 0, 0)),
                  resident((None, HEAD_DIM, CMP_PAD + ncp), lambda g, i: (NSA_GROUPS + g, 0, 0)),
                  resident((None, t, 2 * HEAD_DIM), lambda g, i: (g, 0, 0)),
                  resident((None, HEAD_DIM, t), lambda g, i: (g, 0, 0)),
                  resident((None, t, HEAD_DIM), lambda g, i: (kv0 + 4 * NSA_GROUPS + g, 0, 0)),
                  resident((None, HEAD_DIM, t), lambda g, i: (g, 0, 0)),
                  resident((2 * Q_BLOCK, NSA_ROWS), lambda g, i: (0, g)),
                  resident((CMP_NEAR, NSA_ROWS), lambda g, i: (0, g)),
                  resident((SEL_COLS, ncp), lambda g, i: (0, 0))],
        out_specs=pl.BlockSpec((Q_BLOCK, NSA_HG * HEAD_DIM), lambda g, i: (i, g)),
        out_shape=jax.ShapeDtypeStruct((t, D_MODEL), BF16),
        scratch_shapes=[pltpu.VMEM((SEL_COLS // LANE, NSA_ROWS, 2 * HEAD_DIM), BF16),
                        pltpu.VMEM((2, 2 * Q_BLOCK, NSA_ROWS), F32),
                        pltpu.VMEM((1, NSA_ROWS), F32), pltpu.VMEM((1, NSA_ROWS), F32),
                        pltpu.VMEM((HEAD_DIM, NSA_ROWS), F32), pltpu.VMEM((HEAD_DIM, NSA_ROWS), F32)],
        compiler_params=_cparams(("parallel", "arbitrary")),
        name="nsa_attention",
    )(qkvh, gates, kcv, kcv, kcv_t, ksa, vst, qkvh, vwt, bd, bc, ovl)


def _nsa_layer(h, g_norm, w_in, w_o, cmp_pos, cmp_w1, cmp_w2, bd, bc):
    t = h.shape[0]
    assert t // SEL_BLOCK <= SEL_COLS
    a = _rmsnorm(h, g_norm, BF16)
    n_qkv = D_MODEL + 6 * NSA_KV_DIM
    qkvh = _proj(a, w_in[:, :n_qkv].astype(BF16), out_dtype=BF16, head_major=True, tn=512,
                 n_scaled=D_MODEL)
    w_g = w_in[:, n_qkv:].reshape(D_MODEL, NSA_GROUPS, 3 * NSA_HG)
    w_g = jnp.pad(w_g, ((0, 0), (0, 0), (0, LANE - 3 * NSA_HG))).reshape(D_MODEL, NSA_GROUPS * LANE)
    gates = _proj(a, w_g.astype(BF16), out_dtype=F32, head_major=True, tn=NSA_GROUPS * LANE)
    kvh = qkvh[N_HEADS:]
    kcv = _compress(kvh, cmp_pos, cmp_w1, cmp_w2)
    kcv_t = jnp.swapaxes(kcv, 1, 2)
    key_blk = (jnp.arange(t) // SEL_BLOCK) % LANE
    onehot = (key_blk[:, None] == jnp.arange(LANE)[None, :]).astype(BF16)
    ksa = jnp.concatenate([kvh[2 * NSA_GROUPS:3 * NSA_GROUPS],
                           jnp.broadcast_to(onehot, (NSA_GROUPS, t, LANE))], axis=-1)
    vst = jnp.swapaxes(kvh[3 * NSA_GROUPS:4 * NSA_GROUPS], 1, 2)
    vwt = jnp.swapaxes(kvh[5 * NSA_GROUPS:6 * NSA_GROUPS], 1, 2)
    n = jnp.arange(t // CMP_STRIDE)[None, :]
    sblk = jnp.arange(SEL_COLS)[:, None]
    ovl = ((n >= 4 * sblk - 1) & (n <= 4 * sblk + 3)).astype(BF16)
    o = _nsa_attention(qkvh, gates, kcv, kcv_t, ksa, vst, vwt, bd, bc, ovl)
    return _mm_res(o, w_o.astype(BF16), h, tm=1024, tn=512)


def kernel(x, norm_mix, norm_ffn, norm_final, rel_table, nsa_w_in, nsa_w_o, nsa_cmp_pos, nsa_cmp_w1,
           nsa_cmp_w2, fox_w_in, fox_b_f, fox_w_o, ffn_w_gate, ffn_w_up, ffn_w_down):
    b, t, d = x.shape
    depth = norm_mix.shape[0]
    bd, bc = _nsa_bias_templates(rel_table.astype(F32))
    outs = []
    for bi in range(b):
        h = x[bi]
        for i in range(depth):
            j = i // 2
            if i % 2 == 0:
                h = _nsa_layer(h, norm_mix[i], nsa_w_in[j], nsa_w_o[j], nsa_cmp_pos[j], nsa_cmp_w1[j],
                               nsa_cmp_w2[j], bd, bc)
            else:
                h = _fox_layer(h, norm_mix[i], fox_w_in[j], fox_b_f[j], fox_w_o[j])
            h = _ffn(h, norm_ffn[i], ffn_w_gate[i], ffn_w_up[i], ffn_w_down[i])
        outs.append(_rmsnorm(h, norm_final, F32))
    return jnp.stack(outs, axis=0)
```

```python
import functools
import math

import jax
import jax.numpy as jnp
from jax import lax
from jax.experimental import pallas as pl
from jax.experimental.pallas import tpu as pltpu

F32 = jnp.float32
BF16 = jnp.bfloat16

D_MODEL = 4096
HEAD_DIM = 128
N_HEADS = D_MODEL // HEAD_DIM
NSA_GROUPS = 4
NSA_HG = N_HEADS // NSA_GROUPS
NSA_KV_DIM = NSA_GROUPS * HEAD_DIM
CMP_BLOCK = 32
CMP_STRIDE = 16
SEL_BLOCK = 64
SEL_TOPK = 16
WINDOW = 512
REL_BUCKETS = 32
REL_MAX_DIST = 128
Q_BLOCK = 128
RMS_EPS = 1e-6
NEG_INF = -1e30
FORCED_SCORE = 1e6
LOG2E = 1.4426950408889634
Q_SCALE = HEAD_DIM ** -0.5 * LOG2E

LANE = 128
SEL_COLS = 256
CMP_PAD = 128
CMP_NEAR = 128
CMP_ROWS = 256
NSA_ROWS = NSA_HG * Q_BLOCK
VMEM_LIMIT = 56 * 1024 * 1024


def _cparams(sem):
    return pltpu.CompilerParams(dimension_semantics=sem, vmem_limit_bytes=VMEM_LIMIT)


def _dot(a, b):
    return jnp.dot(a, b, preferred_element_type=F32)


def _dot_nt(a, b):
    return lax.dot_general(a, b, (((1,), (1,)), ((), ())), preferred_element_type=F32)


def _rmsnorm_kernel(x_ref, g_ref, o_ref):
    x = x_ref[...]
    ms = jnp.mean(x * x, axis=-1, keepdims=True)
    o_ref[...] = (x * lax.rsqrt(ms + RMS_EPS) * g_ref[...]).astype(o_ref.dtype)


def _rmsnorm(x, g, out_dtype):
    t, d = x.shape
    tm = min(256, t)
    return pl.pallas_call(
        _rmsnorm_kernel,
        grid=(t // tm,),
        in_specs=[pl.BlockSpec((tm, d), lambda i: (i, 0)),
                  pl.BlockSpec((1, d), lambda i: (0, 0))],
        out_specs=pl.BlockSpec((tm, d), lambda i: (i, 0)),
        out_shape=jax.ShapeDtypeStruct((t, d), out_dtype),
        compiler_params=_cparams(("parallel",)),
        name="rmsnorm",
    )(x, g.reshape(1, d))


def _proj_kernel(x_ref, w_ref, o_ref, *, head_major, n_scaled_blocks):
    r = _dot(x_ref[...], w_ref[...])
    if n_scaled_blocks:
        r = r * jnp.where(pl.program_id(1) < n_scaled_blocks, Q_SCALE, 1.0)
    if head_major:
        for s in range(o_ref.shape[0]):
            o_ref[s] = r[:, s * LANE:(s + 1) * LANE].astype(o_ref.dtype)
    else:
        o_ref[...] = r.astype(o_ref.dtype)


def _proj(x, w, *, out_dtype, head_major, tn, n_scaled=0):
    t, k = x.shape
    n = w.shape[1]
    tm = min(1024, t)
    tn = min(tn, n)
    assert n_scaled % tn == 0
    if head_major:
        out_shape = jax.ShapeDtypeStruct((n // LANE, t, LANE), out_dtype)
        out_spec = pl.BlockSpec((tn // LANE, tm, LANE), lambda i, j: (j, i, 0))
    else:
        out_shape = jax.ShapeDtypeStruct((t, n), out_dtype)
        out_spec = pl.BlockSpec((tm, tn), lambda i, j: (i, j))
    return pl.pallas_call(
        functools.partial(_proj_kernel, head_major=head_major, n_scaled_blocks=n_scaled // tn),
        grid=(t // tm, n // tn),
        in_specs=[pl.BlockSpec((tm, k), lambda i, j: (i, 0)),
                  pl.BlockSpec((k, tn), lambda i, j: (0, j))],
        out_specs=out_spec,
        out_shape=out_shape,
        compiler_params=_cparams(("parallel", "parallel")),
        name="proj",
    )(x, w)


def _proj_t_kernel(x_ref, w_ref, o_ref):
    r = _dot(x_ref[...], w_ref[...])
    for s in range(o_ref.shape[0]):
        o_ref[s] = r[:, s * LANE:(s + 1) * LANE].T.astype(o_ref.dtype)


def _proj_t(x, w, *, tn):
    t, k = x.shape
    n = w.shape[1]
    tm = min(1024, t)
    tn = min(tn, n)
    return pl.pallas_call(
        _proj_t_kernel,
        grid=(t // tm, n // tn),
        in_specs=[pl.BlockSpec((tm, k), lambda i, j: (i, 0)),
                  pl.BlockSpec((k, tn), lambda i, j: (0, j))],
        out_specs=pl.BlockSpec((tn // LANE, LANE, tm), lambda i, j: (j, 0, i)),
        out_shape=jax.ShapeDtypeStruct((n // LANE, LANE, t), BF16),
        compiler_params=_cparams(("parallel", "parallel")),
        name="proj_t",
    )(x, w)


def _mm_res_kernel(x_ref, w_ref, r_ref, o_ref):
    o_ref[...] = r_ref[...] + _dot(x_ref[...], w_ref[...])


def _mm_res(x, w, res, *, tm, tn):
    t, k = x.shape
    n = w.shape[1]
    tm = min(tm, t)
    return pl.pallas_call(
        _mm_res_kernel,
        grid=(t // tm, n // tn),
        in_specs=[pl.BlockSpec((tm, k), lambda i, j: (i, 0)),
                  pl.BlockSpec((k, tn), lambda i, j: (0, j)),
                  pl.BlockSpec((tm, tn), lambda i, j: (i, j))],
        out_specs=pl.BlockSpec((tm, tn), lambda i, j: (i, j)),
        out_shape=jax.ShapeDtypeStruct((t, n), F32),
        compiler_params=_cparams(("parallel", "parallel")),
        name="mm_res",
    )(x, w, res)


def _gateup_kernel(x_ref, wg_ref, wu_ref, o_ref):
    x = x_ref[...]
    g = _dot(x, wg_ref[...])
    u = _dot(x, wu_ref[...])
    o_ref[...] = (g * jax.nn.sigmoid(g) * u).astype(o_ref.dtype)


def _gateup(x, wg, wu, *, tn):
    t, k = x.shape
    n = wg.shape[1]
    tm = min(1024, t)
    return pl.pallas_call(
        _gateup_kernel,
        grid=(t // tm, n // tn),
        in_specs=[pl.BlockSpec((tm, k), lambda i, j: (i, 0)),
                  pl.BlockSpec((k, tn), lambda i, j: (0, j)),
                  pl.BlockSpec((k, tn), lambda i, j: (0, j))],
        out_specs=pl.BlockSpec((tm, tn), lambda i, j: (i, j)),
        out_shape=jax.ShapeDtypeStruct((t, n), BF16),
        compiler_params=_cparams(("parallel", "parallel")),
        name="gateup",
    )(x, wg, wu)


def _ffn(h, g_norm, w_gate, w_up, w_down):
    a = _rmsnorm(h, g_norm, BF16)
    h1 = _gateup(a, w_gate.astype(BF16), w_up.astype(BF16), tn=256)
    return _mm_res(h1, w_down.astype(BF16), h, tm=512, tn=256)


def _flash_init(m_ref, l_ref, acc_ref):
    m_ref[...] = jnp.full(m_ref.shape, NEG_INF, F32)
    l_ref[...] = jnp.zeros(l_ref.shape, F32)
    acc_ref[...] = jnp.zeros(acc_ref.shape, F32)


def _flash_update(s_t, v_t, m_ref, l_ref, acc_ref):
    m_prev = m_ref[...]
    m_new = jnp.maximum(m_prev, jnp.max(s_t, axis=0, keepdims=True))
    alpha = jnp.exp2(m_prev - m_new)
    p = jnp.exp2(s_t - m_new)
    l_ref[...] = alpha * l_ref[...] + jnp.sum(p, axis=0, keepdims=True)
    acc_ref[...] = alpha * acc_ref[...] + _dot(v_t, p.astype(BF16))
    m_ref[...] = m_new


def _softmax_tiles(tiles):
    mx = functools.reduce(jnp.maximum, [jnp.max(s_t, axis=0, keepdims=True) for s_t, _ in tiles])
    den = None
    acc = None
    for s_t, v_t in tiles:
        p = jnp.exp2(s_t - mx)
        d = jnp.sum(p, axis=0, keepdims=True)
        a = _dot(v_t, p.astype(BF16))
        den = d if den is None else den + d
        acc = a if acc is None else acc + a
    return mx, den, acc


def _split3(x):
    hi = x.astype(BF16)
    r = x - hi.astype(F32)
    mid = r.astype(BF16)
    lo = (r - mid.astype(F32)).astype(BF16)
    return hi, mid, lo


def _split3_trunc(x):
    def top(v):
        bits = lax.bitcast_convert_type(v, jnp.uint32) & jnp.uint32(0xFFFF0000)
        return lax.bitcast_convert_type(bits, F32)
    hi = top(x)
    mid = top(x - hi)
    lo = x - hi - mid
    return hi.astype(BF16), mid.astype(BF16), lo.astype(BF16)


def _cum_kernel(f_ref, b_ref, tri_ref, o_ref, carry_ref):
    @pl.when(pl.program_id(0) == 0)
    def _():
        carry_ref[...] = jnp.zeros_like(carry_ref)

    x = f_ref[...] + b_ref[...]
    log_f = jnp.minimum(x, 0.0) - jnp.log1p(jnp.exp(-jnp.abs(x)))
    c = jnp.dot(tri_ref[...], log_f, precision=lax.Precision.HIGHEST,
                preferred_element_type=F32) + carry_ref[...]
    o_ref[...] = c
    carry_ref[...] = c[c.shape[0] - 1:, :]


def _cum_log_forget(f, b):
    t, n = f.shape
    tb = min(512, t)
    tri = jnp.tril(jnp.ones((tb, tb), F32))
    return pl.pallas_call(
        _cum_kernel,
        grid=(t // tb,),
        in_specs=[pl.BlockSpec((tb, n), lambda i: (i, 0)),
                  pl.BlockSpec((1, n), lambda i: (0, 0)),
                  pl.BlockSpec((tb, tb), lambda i: (0, 0))],
        out_specs=pl.BlockSpec((tb, n), lambda i: (i, 0)),
        out_shape=jax.ShapeDtypeStruct((t, n), F32),
        scratch_shapes=[pltpu.VMEM((1, n), F32)],
        compiler_params=_cparams(("arbitrary",)),
        name="cum_log_forget",
    )(f, b, tri)


def _pipelined_loop(first, trips, per_trip, scores, consume, s_ref):
    def body(t, carry):
        c = first + t * per_trip
        for u in range(per_trip):
            s_next = scores(c + u + 1)
            consume(c + u, s_ref[u % 2])
            s_ref[(u + 1) % 2] = s_next
        return carry

    lax.fori_loop(0, trips, body, 0)


def _pipelined_chunks(n_chunks, scores, consume, s_ref):
    quads = n_chunks // 4
    _pipelined_loop(0, quads, 4, scores, consume, s_ref)
    _pipelined_loop(4 * quads, (n_chunks % 4) // 2, 2, scores, consume, s_ref)


def _fox_kernel(q_ref, qx_ref, k_ref, kx_ref, vt_ref, o_ref, ka_ref, qa_ref, s_ref, m_ref, l_ref, acc_ref,
                *, tq, tk):
    i = pl.program_id(1)
    assert tq == 2 * tk

    @pl.when(i == 0)
    def _():
        ka_ref[:, 0:HEAD_DIM] = k_ref[...]
        ka_ref[:, HEAD_DIM:] = kx_ref[...]

    qa_ref[:, 0:HEAD_DIM] = q_ref[...]
    qa_ref[:, HEAD_DIM:] = qx_ref[...]
    _flash_init(m_ref, l_ref, acc_ref)

    def scores(c):
        off = pl.multiple_of(c * tk, tk)
        return _dot_nt(ka_ref[pl.ds(off, tk), :], qa_ref[...])

    def consume(c, s_t):
        off = pl.multiple_of(c * tk, tk)
        _flash_update(s_t, vt_ref[:, pl.ds(off, tk)], m_ref, l_ref, acc_ref)

    s_ref[0] = scores(0)
    _pipelined_chunks(2 * i, scores, consume, s_ref)
    key = lax.broadcasted_iota(jnp.int32, (tk, tq), 0)
    qry = lax.broadcasted_iota(jnp.int32, (tk, tq), 1)
    s_last = scores(2 * i + 1)
    consume(2 * i, jnp.where(key <= qry, s_ref[0], NEG_INF))
    consume(2 * i + 1, jnp.where(key + tk <= qry, s_last, NEG_INF))
    o_ref[...] = (acc_ref[...] / l_ref[...]).T.astype(o_ref.dtype)


def _fox_attention(qkh, qx, kx, vt):
    t = qkh.shape[1]
    tq = min(1024, t)
    tk = tq // 2
    return pl.pallas_call(
        functools.partial(_fox_kernel, tq=tq, tk=tk),
        grid=(N_HEADS, t // tq),
        in_specs=[pl.BlockSpec((None, tq, HEAD_DIM), lambda h, i: (h, i, 0)),
                  pl.BlockSpec((None, tq, LANE), lambda h, i: (h, i, 0)),
                  pl.BlockSpec((None, t, HEAD_DIM), lambda h, i: (N_HEADS + h, 0, 0)),
                  pl.BlockSpec((None, t, LANE), lambda h, i: (h, 0, 0)),
                  pl.BlockSpec((None, HEAD_DIM, t), lambda h, i: (h, 0, 0))],
        out_specs=pl.BlockSpec((tq, HEAD_DIM), lambda h, i: (i, h)),
        out_shape=jax.ShapeDtypeStruct((t, D_MODEL), BF16),
        scratch_shapes=[pltpu.VMEM((t, 2 * HEAD_DIM), BF16),
                        pltpu.VMEM((tq, 2 * HEAD_DIM), BF16), pltpu.VMEM((2, tk, tq), F32),
                        pltpu.VMEM((1, tq), F32), pltpu.VMEM((1, tq), F32),
                        pltpu.VMEM((HEAD_DIM, tq), F32)],
        compiler_params=_cparams(("parallel", "arbitrary")),
        name="fox_attention",
    )(qkh, qx, qkh, kx, vt)


def _fox_layer(h, g_norm, w_in, b_f, w_o):
    t = h.shape[0]
    a = _rmsnorm(h, g_norm, BF16)
    n_qk = 2 * D_MODEL
    w = w_in.astype(BF16)
    qkh = _proj(a, w[:, :n_qk], out_dtype=BF16, head_major=True, tn=512, n_scaled=D_MODEL)
    vt = _proj_t(a, w[:, n_qk:n_qk + D_MODEL], tn=512)
    w_f = jnp.pad(w[:, n_qk + D_MODEL:], ((0, 0), (0, LANE - N_HEADS)))
    f = _proj(a, w_f, out_dtype=F32, head_major=False, tn=LANE)
    b = jnp.pad(b_f.astype(F32), (0, LANE - N_HEADS)).reshape(1, LANE)
    cum = _cum_log_forget(f, b)[:, :N_HEADS].T
    pieces = jnp.stack(_split3_trunc(cum * LOG2E), axis=-1)
    ones = jnp.ones_like(pieces)
    pad = jnp.zeros((N_HEADS, t, LANE - 6), BF16)
    qx = jnp.concatenate([pieces, -ones, pad], axis=-1)
    kx = jnp.concatenate([ones, pieces, pad], axis=-1)
    o = _fox_attention(qkh, qx, kx, vt)
    return _mm_res(o, w_o.astype(BF16), h, tm=1024, tn=512)


def _rel_bucket_const(dist):
    n = jnp.maximum(dist, 0)
    max_exact = REL_BUCKETS // 2
    nf = jnp.maximum(n, max_exact).astype(F32)
    large = max_exact + (jnp.log(nf / max_exact) / math.log(REL_MAX_DIST / max_exact)
                         * (REL_BUCKETS - max_exact)).astype(jnp.int32)
    large = jnp.minimum(large, REL_BUCKETS - 1)
    return jnp.where(n < max_exact, n, large)


def _bias_kernel(tab_ref, bkt_ref, o_ref):
    h = pl.program_id(0)
    bkt = bkt_ref[...]
    acc = jnp.zeros(bkt.shape, F32)
    for b in range(REL_BUCKETS):
        acc = jnp.where(bkt == b, tab_ref[b, h], acc)
    o_ref[...] = (acc - tab_ref[REL_BUCKETS - 1, h]) * LOG2E


def _bias_template(rel_table, bkt):
    r, c = bkt.shape
    return pl.pallas_call(
        _bias_kernel,
        grid=(N_HEADS,),
        in_specs=[pl.BlockSpec(memory_space=pltpu.SMEM),
                  pl.BlockSpec((r, c), lambda h: (0, 0))],
        out_specs=pl.BlockSpec((r, c), lambda h: (0, h)),
        out_shape=jax.ShapeDtypeStruct((r, N_HEADS * c), F32),
        compiler_params=_cparams(("arbitrary",)),
        name="bias_template",
    )(rel_table, bkt)


def _nsa_bias_templates(rel_table):
    tl = jnp.arange(Q_BLOCK)[None, :]
    bd = _bias_template(rel_table, _rel_bucket_const(tl + Q_BLOCK - jnp.arange(2 * Q_BLOCK)[:, None]))
    m = jnp.arange(CMP_NEAR)[:, None]
    bc = _bias_template(rel_table, _rel_bucket_const(
        tl - CMP_STRIDE * (m - (CMP_NEAR - 8)) - (CMP_BLOCK - 1)))
    return bd, bc


def _gelu_tanh(x):
    return 0.5 * x * (1.0 + jnp.tanh(math.sqrt(2.0 / math.pi) * (x + 0.044715 * (x * x * x))))


def _cmp_kernel(kb_ref, pos_ref, w1_ref, w2_ref, o_ref, *, nb):
    half = CMP_STRIDE * HEAD_DIM
    kb = kb_ref[...].astype(F32)
    xa = (kb + pos_ref[0:1, :]).astype(BF16)
    xb = (kb + pos_ref[1:2, :]).astype(BF16)
    a = _dot(xa, w1_ref[0:half, :])
    b = _dot(xb, w1_ref[half:2 * half, :])
    pre = a + pltpu.roll(b, nb - 1, axis=0)
    out = _dot(_gelu_tanh(pre).astype(BF16), w2_ref[...])
    row = lax.broadcasted_iota(jnp.int32, out.shape, 0)
    out = jnp.where(row < nb - 1, out, 0.0)
    o_ref[0:CMP_PAD, :] = jnp.zeros((CMP_PAD, HEAD_DIM), F32)
    o_ref[CMP_PAD:CMP_PAD + nb, :] = out


def _compress(kvh, cmp_pos, cmp_w1, cmp_w2):
    t = kvh.shape[1]
    nb = t // CMP_STRIDE
    kb = kvh[:2 * NSA_GROUPS].reshape(2 * NSA_GROUPS, nb, CMP_STRIDE * HEAD_DIM)
    pos = cmp_pos.astype(F32).reshape(2, 2, CMP_STRIDE * HEAD_DIM)
    return pl.pallas_call(
        functools.partial(_cmp_kernel, nb=nb),
        grid=(2 * NSA_GROUPS,),
        in_specs=[pl.BlockSpec((None, nb, CMP_STRIDE * HEAD_DIM), lambda j: (j, 0, 0)),
                  pl.BlockSpec((None, 2, CMP_STRIDE * HEAD_DIM), lambda j: (j // NSA_GROUPS, 0, 0)),
                  pl.BlockSpec((None, CMP_BLOCK * HEAD_DIM, HEAD_DIM), lambda j: (j // NSA_GROUPS, 0, 0)),
                  pl.BlockSpec((None, HEAD_DIM, HEAD_DIM), lambda j: (j // NSA_GROUPS, 0, 0))],
        out_specs=pl.BlockSpec((None, CMP_PAD + nb, HEAD_DIM), lambda j: (j, 0, 0)),
        out_shape=jax.ShapeDtypeStruct((2 * NSA_GROUPS, CMP_PAD + nb, HEAD_DIM), F32),
        compiler_params=_cparams(("parallel",)),
        name="compress",
    )(kb, pos, cmp_w1.astype(BF16), cmp_w2.astype(BF16))


def _dot_split_rhs(w, x):
    hi, mid, lo = _split3(x)
    return _dot(w, hi) + _dot(w, mid) + _dot(w, lo)


def _nsa_kernel(q_ref, g_ref, kc_ref, vc_ref, vct_ref, ksa_ref, vst_ref, kw_ref, vwt_ref,
                bd_ref, bc_ref, ovl_ref, o_ref,
                qa_ref, s_ref, sc_ref, imp_ref, m_ref, l_ref, acc_ref, out_ref):
    i = pl.program_id(1)
    rows = NSA_ROWS
    q = q_ref[...].reshape(rows, HEAD_DIM)
    tl_lane = lax.broadcasted_iota(jnp.int32, (Q_BLOCK, rows), 1) % Q_BLOCK
    key_row = lax.broadcasted_iota(jnp.int32, (Q_BLOCK, rows), 0)
    sig_t = jax.nn.sigmoid(g_ref[...]).T

    def gate(branch):
        return jnp.concatenate([sig_t[3 * hg + branch:3 * hg + branch + 1, :] for hg in range(NSA_HG)], axis=1)

    def head_sum(p):
        acc = p[:, 0:Q_BLOCK]
        for hg in range(1, NSA_HG):
            acc = acc + p[:, hg * Q_BLOCK:(hg + 1) * Q_BLOCK]
        return acc

    n_first_near = 8 * i - (CMP_NEAR - 8)
    n_blocks = (jnp.maximum(n_first_near, 0) + CMP_ROWS - 1) // CMP_ROWS

    def far_rows(b):
        return pl.multiple_of(b * CMP_ROWS, CMP_ROWS)

    def scores_pass(b, mx):
        r0 = far_rows(b)
        kb = kc_ref[pl.ds(CMP_PAD + r0, CMP_ROWS), :].astype(BF16)
        row = r0 + lax.broadcasted_iota(jnp.int32, (CMP_ROWS, rows), 0)
        s = jnp.where(row < n_first_near, _dot_nt(kb, q), NEG_INF)
        sc_ref[pl.ds(r0, CMP_ROWS), :] = s
        return jnp.maximum(mx, jnp.max(s, axis=0, keepdims=True))

    mx_far = lax.fori_loop(0, n_blocks, scores_pass, jnp.full((1, rows), NEG_INF, F32))
    start = pl.multiple_of(8 * i + 8 + CMP_PAD - CMP_NEAR, 8)
    kcn = kc_ref[pl.ds(start, CMP_NEAR), :].astype(BF16)
    vcn_t = vc_ref[pl.ds(start, CMP_NEAR), :].T.astype(BF16)
    d_near = tl_lane - CMP_STRIDE * (key_row - (CMP_NEAR - 8)) - (CMP_BLOCK - 1)
    near_ok = (d_near >= 0) & (n_first_near + key_row >= 0)
    s_near = jnp.where(near_ok, _dot_nt(kcn, q) + bc_ref[...], NEG_INF)
    mx = jnp.maximum(mx_far, jnp.max(s_near, axis=0, keepdims=True))
    e_near = jnp.exp2(s_near - mx)

    def exp_pass(b, den):
        r0 = far_rows(b)
        e = jnp.exp2(sc_ref[pl.ds(r0, CMP_ROWS), :] - mx)
        sc_ref[pl.ds(r0, CMP_ROWS), :] = e
        return den + jnp.sum(e, axis=0, keepdims=True)

    den = lax.fori_loop(0, n_blocks, exp_pass, jnp.sum(e_near, axis=0, keepdims=True))
    inv = jnp.where(mx > 0.5 * NEG_INF, 1.0 / den, 0.0)
    p_near = e_near * inv
    sb = lax.broadcasted_iota(jnp.int32, (SEL_COLS, CMP_NEAR), 0)
    nn = n_first_near + lax.broadcasted_iota(jnp.int32, (SEL_COLS, CMP_NEAR), 1)
    ovl_near = jnp.where((nn >= 4 * sb - 1) & (nn <= 4 * sb + 3) & (nn >= 0), 1.0, 0.0).astype(BF16)
    acc_ref[...] = _dot(vcn_t, p_near.astype(BF16))
    imp_ref[...] = _dot_split_rhs(ovl_near, head_sum(p_near))

    def out_pass(b, carry):
        r0 = far_rows(b)
        p = sc_ref[pl.ds(r0, CMP_ROWS), :] * inv
        acc_ref[...] += _dot(vct_ref[:, pl.ds(CMP_PAD + r0, CMP_ROWS)].astype(BF16), p.astype(BF16))
        imp_ref[...] += _dot_split_rhs(ovl_ref[:, pl.ds(r0, CMP_ROWS)], head_sum(p))
        return carry

    lax.fori_loop(0, n_blocks, out_pass, 0)
    out_ref[...] = gate(0) * acc_ref[...]

    imp = imp_ref[...]
    blk = lax.broadcasted_iota(jnp.int32, (SEL_COLS, Q_BLOCK), 0)
    tl = lax.broadcasted_iota(jnp.int32, (SEL_COLS, Q_BLOCK), 1)
    cur = 2 * i + (tl >= SEL_BLOCK).astype(jnp.int32)
    forced = (blk == 0) | (blk == cur) | (blk == cur - 1)
    causal_blk = blk * SEL_BLOCK <= Q_BLOCK * i + tl
    work = jnp.where(forced, FORCED_SCORE, jnp.where(causal_blk, imp, -1.0))
    blk_f = blk.astype(F32)
    sel = jnp.zeros((SEL_COLS, Q_BLOCK), F32)
    for _ in range(SEL_TOPK):
        best = jnp.max(work, axis=0, keepdims=True)
        first = jnp.min(jnp.where(work == best, blk_f, float(SEL_COLS)), axis=0, keepdims=True)
        pick = blk_f == first
        sel = jnp.where(pick, 1.0, sel)
        work = jnp.where(pick, -2.0, work)
    amask = jnp.where(sel > 0.0, 0.0, NEG_INF)
    for half in range(SEL_COLS // LANE):
        a_t = amask[half * LANE:(half + 1) * LANE, :].T.astype(BF16)
        qa_ref[half, :, 0:HEAD_DIM] = q
        qa_ref[half, :, HEAD_DIM:2 * HEAD_DIM] = jnp.concatenate([a_t] * NSA_HG, axis=0)

    def sel_chunk(tile, n_keys):
        off = pl.multiple_of(tile * Q_BLOCK, Q_BLOCK)
        qa = qa_ref[tile // (LANE // 2)]
        return _dot_nt(ksa_ref[pl.ds(off, n_keys), :], qa), vst_ref[:, pl.ds(off, n_keys)]

    s_t, v_t = sel_chunk(i, Q_BLOCK)
    near = [(jnp.where(key_row <= tl_lane, s_t + bd_ref[Q_BLOCK:, :], NEG_INF), v_t)]
    s_t, v_t = sel_chunk(jnp.maximum(i - 1, 0), Q_BLOCK)
    near.append((jnp.where(i >= 1, s_t + bd_ref[:Q_BLOCK, :], NEG_INF), v_t))
    m_ref[...], l_ref[...], acc_ref[...] = _softmax_tiles(near)
    n_far = jnp.maximum(i - 1, 0)

    n_pairs = n_far // 2

    def far_scores(c):
        tile = 2 * jnp.minimum(c, jnp.maximum(n_pairs - 1, 0))
        off = pl.multiple_of(tile * Q_BLOCK, 2 * Q_BLOCK)
        return _dot_nt(ksa_ref[pl.ds(off, 2 * Q_BLOCK), :], qa_ref[tile // (LANE // 2)])

    def far_consume(c, s_t):
        off = pl.multiple_of(c * 2 * Q_BLOCK, 2 * Q_BLOCK)
        _flash_update(s_t, vst_ref[:, pl.ds(off, 2 * Q_BLOCK)], m_ref, l_ref, acc_ref)

    s_ref[0] = far_scores(0)
    _pipelined_chunks(n_pairs, far_scores, far_consume, s_ref)

    @pl.when(n_pairs % 2 == 1)
    def _():
        far_consume(n_pairs - 1, s_ref[0])

    @pl.when(n_far % 2 == 1)
    def _():
        s_t, v_t = sel_chunk(n_far - 1, Q_BLOCK)
        _flash_update(s_t, v_t, m_ref, l_ref, acc_ref)

    out_ref[...] += gate(1) * (acc_ref[...] / l_ref[...])

    def win_chunk(tile):
        off = pl.multiple_of(jnp.maximum(tile, 0) * Q_BLOCK, Q_BLOCK)
        return _dot_nt(kw_ref[pl.ds(off, Q_BLOCK), :], q), vwt_ref[:, pl.ds(off, Q_BLOCK)]

    s_t, v_t = win_chunk(i)
    win = [(jnp.where(key_row <= tl_lane, s_t + bd_ref[Q_BLOCK:, :], NEG_INF), v_t)]
    s_t, v_t = win_chunk(i - 1)
    win.append((jnp.where(i >= 1, s_t + bd_ref[:Q_BLOCK, :], NEG_INF), v_t))
    for back in (2, 3):
        s_t, v_t = win_chunk(i - back)
        win.append((jnp.where(i >= back, s_t, NEG_INF), v_t))
    s_t, v_t = win_chunk(i - 4)
    win.append((jnp.where((key_row > tl_lane) & (i >= 4), s_t, NEG_INF), v_t))
    _, l_w, acc_w = _softmax_tiles(win)
    o = out_ref[...] + gate(2) * (acc_w / l_w)
    for hg in range(NSA_HG):
        o_ref[:, hg * HEAD_DIM:(hg + 1) * HEAD_DIM] = o[:, hg * Q_BLOCK:(hg + 1) * Q_BLOCK].T.astype(o_ref.dtype)


def _nsa_attention(qkh, gates, kcv, kcv_t, ksa, vt, bd, bc, ovl):
    t = qkh.shape[1]
    ncp = t // CMP_STRIDE
    assert ncp % CMP_ROWS == 0
    resident = functools.partial(pl.BlockSpec, pipeline_mode=pl.Buffered(1))
    return pl.pallas_call(
        _nsa_kernel,
        grid=(NSA_GROUPS, t // Q_BLOCK),
        in_specs=[pl.BlockSpec((NSA_HG, Q_BLOCK, HEAD_DIM), lambda g, i: (g, i, 0)),
                  pl.BlockSpec((None, Q_BLOCK, LANE), lambda g, i: (g, i, 0)),
                  resident((None, CMP_PAD + ncp, HEAD_DIM), lambda g, i: (g, 0, 0)),
                  resident((None, CMP_PAD + ncp, HEAD_DIM), lambda g, i: (NSA_GROUPS + g, 0, 0)),
                  resident((None, HEAD_DIM, CMP_PAD + ncp), lambda g, i: (NSA_GROUPS + g, 0, 0)),
                  resident((None, t, 2 * HEAD_DIM), lambda g, i: (g, 0, 0)),
                  resident((None, HEAD_DIM, t), lambda g, i: (g, 0, 0)),
                  resident((None, t, HEAD_DIM), lambda g, i: (N_HEADS + 3 * NSA_GROUPS + g, 0, 0)),
                  resident((None, HEAD_DIM, t), lambda g, i: (NSA_GROUPS + g, 0, 0)),
                  resident((2 * Q_BLOCK, NSA_ROWS), lambda g, i: (0, g)),
                  resident((CMP_NEAR, NSA_ROWS), lambda g, i: (0, g)),
                  resident((SEL_COLS, ncp), lambda g, i: (0, 0))],
        out_specs=pl.BlockSpec((Q_BLOCK, NSA_HG * HEAD_DIM), lambda g, i: (i, g)),
        out_shape=jax.ShapeDtypeStruct((t, D_MODEL), BF16),
        scratch_shapes=[pltpu.VMEM((SEL_COLS // LANE, NSA_ROWS, 2 * HEAD_DIM), BF16),
                        pltpu.VMEM((2, 2 * Q_BLOCK, NSA_ROWS), F32),
                        pltpu.VMEM((ncp, NSA_ROWS), F32), pltpu.VMEM((SEL_COLS, Q_BLOCK), F32),
                        pltpu.VMEM((1, NSA_ROWS), F32), pltpu.VMEM((1, NSA_ROWS), F32),
                        pltpu.VMEM((HEAD_DIM, NSA_ROWS), F32), pltpu.VMEM((HEAD_DIM, NSA_ROWS), F32)],
        compiler_params=_cparams(("parallel", "arbitrary")),
        name="nsa_attention",
    )(qkh, gates, kcv, kcv, kcv_t, ksa, vt, qkh, vt, bd, bc, ovl)


def _nsa_layer(h, g_norm, w_in, w_o, cmp_pos, cmp_w1, cmp_w2, bd, bc):
    t = h.shape[0]
    assert t // SEL_BLOCK <= SEL_COLS
    a = _rmsnorm(h, g_norm, BF16)
    n_qkv = D_MODEL + 6 * NSA_KV_DIM
    w = w_in[:, :n_qkv].astype(BF16)

    def kv_cols(j):
        return w[:, D_MODEL + j * NSA_KV_DIM:D_MODEL + (j + 1) * NSA_KV_DIM]

    w_rows = jnp.concatenate([w[:, :D_MODEL], kv_cols(0), kv_cols(1), kv_cols(2), kv_cols(4)], axis=1)
    qkh = _proj(a, w_rows, out_dtype=BF16, head_major=True, tn=512, n_scaled=D_MODEL)
    vt = _proj_t(a, jnp.concatenate([kv_cols(3), kv_cols(5)], axis=1), tn=512)
    w_g = w_in[:, n_qkv:].reshape(D_MODEL, NSA_GROUPS, 3 * NSA_HG)
    w_g = jnp.pad(w_g, ((0, 0), (0, 0), (0, LANE - 3 * NSA_HG))).reshape(D_MODEL, NSA_GROUPS * LANE)
    gates = _proj(a, w_g.astype(BF16), out_dtype=F32, head_major=True, tn=NSA_GROUPS * LANE)
    kcv = _compress(qkh[N_HEADS:N_HEADS + 2 * NSA_GROUPS], cmp_pos, cmp_w1, cmp_w2)
    kcv_t = jnp.swapaxes(kcv, 1, 2)
    key_blk = (jnp.arange(t) // SEL_BLOCK) % LANE
    onehot = (key_blk[:, None] == jnp.arange(LANE)[None, :]).astype(BF16)
    ksa = jnp.concatenate([qkh[N_HEADS + 2 * NSA_GROUPS:N_HEADS + 3 * NSA_GROUPS],
                           jnp.broadcast_to(onehot, (NSA_GROUPS, t, LANE))], axis=-1)
    n = jnp.arange(t // CMP_STRIDE)[None, :]
    sblk = jnp.arange(SEL_COLS)[:, None]
    ovl = ((n >= 4 * sblk - 1) & (n <= 4 * sblk + 3)).astype(BF16)
    o = _nsa_attention(qkh, gates, kcv, kcv_t, ksa, vt, bd, bc, ovl)
    return _mm_res(o, w_o.astype(BF16), h, tm=1024, tn=512)


def kernel(x, norm_mix, norm_ffn, norm_final, rel_table, nsa_w_in, nsa_w_o, nsa_cmp_pos, nsa_cmp_w1,
           nsa_cmp_w2, fox_w_in, fox_b_f, fox_w_o, ffn_w_gate, ffn_w_up, ffn_w_down):
    b, t, d = x.shape
    depth = norm_mix.shape[0]
    bd, bc = _nsa_bias_templates(rel_table.astype(F32))
    outs = []
    for bi in range(b):
        h = x[bi]
        for i in range(depth):
            j = i // 2
            if i % 2 == 0:
                h = _nsa_layer(h, norm_mix[i], nsa_w_in[j], nsa_w_o[j], nsa_cmp_pos[j], nsa_cmp_w1[j],
                               nsa_cmp_w2[j], bd, bc)
            else:
                h = _fox_layer(h, norm_mix[i], fox_w_in[j], fox_b_f[j], fox_w_o[j])
            h = _ffn(h, norm_ffn[i], ffn_w_gate[i], ffn_w_up[i], ffn_w_down[i])
        outs.append(_rmsnorm(h, norm_final, F32))
    return jnp.stack(outs, axis=0)
```

```python
import functools
import math

import jax
import jax.numpy as jnp
from jax import lax
from jax.experimental import pallas as pl
from jax.experimental.pallas import tpu as pltpu

F32 = jnp.float32
BF16 = jnp.bfloat16

D_MODEL = 4096
HEAD_DIM = 128
N_HEADS = D_MODEL // HEAD_DIM
NSA_GROUPS = 4
NSA_HG = N_HEADS // NSA_GROUPS
NSA_KV_DIM = NSA_GROUPS * HEAD_DIM
CMP_BLOCK = 32
CMP_STRIDE = 16
SEL_BLOCK = 64
SEL_TOPK = 16
WINDOW = 512
REL_BUCKETS = 32
REL_MAX_DIST = 128
Q_BLOCK = 128
RMS_EPS = 1e-6
NEG_INF = -1e30
FORCED_SCORE = 1e6
LOG2E = 1.4426950408889634
Q_SCALE = HEAD_DIM ** -0.5 * LOG2E

LANE = 128
SEL_COLS = 256
CMP_PAD = 128
CMP_NEAR = 128
CMP_ROWS = 256
FAR_TILES = 4
NSA_ROWS = NSA_HG * Q_BLOCK
VMEM_LIMIT = 56 * 1024 * 1024


def _cparams(sem):
    return pltpu.CompilerParams(dimension_semantics=sem, vmem_limit_bytes=VMEM_LIMIT)


def _dot(a, b):
    return jnp.dot(a, b, preferred_element_type=F32)


def _dot_nt(a, b):
    return lax.dot_general(a, b, (((1,), (1,)), ((), ())), preferred_element_type=F32)


def _rmsnorm_kernel(x_ref, g_ref, o_ref):
    x = x_ref[...]
    ms = jnp.mean(x * x, axis=-1, keepdims=True)
    o_ref[...] = (x * lax.rsqrt(ms + RMS_EPS) * g_ref[...]).astype(o_ref.dtype)


def _rmsnorm(x, g, out_dtype):
    t, d = x.shape
    tm = min(256, t)
    return pl.pallas_call(
        _rmsnorm_kernel,
        grid=(t // tm,),
        in_specs=[pl.BlockSpec((tm, d), lambda i: (i, 0)),
                  pl.BlockSpec((1, d), lambda i: (0, 0))],
        out_specs=pl.BlockSpec((tm, d), lambda i: (i, 0)),
        out_shape=jax.ShapeDtypeStruct((t, d), out_dtype),
        compiler_params=_cparams(("parallel",)),
        name="rmsnorm",
    )(x, g.reshape(1, d))


def _proj_kernel(x_ref, w_ref, o_ref, *, head_major, n_scaled_blocks):
    r = _dot(x_ref[...], w_ref[...])
    if n_scaled_blocks:
        r = r * jnp.where(pl.program_id(1) < n_scaled_blocks, Q_SCALE, 1.0)
    if head_major:
        for s in range(o_ref.shape[0]):
            o_ref[s] = r[:, s * LANE:(s + 1) * LANE].astype(o_ref.dtype)
    else:
        o_ref[...] = r.astype(o_ref.dtype)


def _proj(x, w, *, out_dtype, head_major, tn, n_scaled=0):
    t, k = x.shape
    n = w.shape[1]
    tm = min(1024, t)
    tn = min(tn, n)
    assert n_scaled % tn == 0
    if head_major:
        out_shape = jax.ShapeDtypeStruct((n // LANE, t, LANE), out_dtype)
        out_spec = pl.BlockSpec((tn // LANE, tm, LANE), lambda i, j: (j, i, 0))
    else:
        out_shape = jax.ShapeDtypeStruct((t, n), out_dtype)
        out_spec = pl.BlockSpec((tm, tn), lambda i, j: (i, j))
    return pl.pallas_call(
        functools.partial(_proj_kernel, head_major=head_major, n_scaled_blocks=n_scaled // tn),
        grid=(t // tm, n // tn),
        in_specs=[pl.BlockSpec((tm, k), lambda i, j: (i, 0)),
                  pl.BlockSpec((k, tn), lambda i, j: (0, j))],
        out_specs=out_spec,
        out_shape=out_shape,
        compiler_params=_cparams(("parallel", "parallel")),
        name="proj",
    )(x, w)


def _proj_t_kernel(x_ref, w_ref, o_ref):
    r = _dot(x_ref[...], w_ref[...])
    ones = _ones_rows(r.shape[0], o_ref.dtype)
    for s in range(o_ref.shape[0]):
        o_ref[s, 0:HEAD_DIM, :] = r[:, s * LANE:(s + 1) * LANE].T.astype(o_ref.dtype)
        o_ref[s, HEAD_DIM:, :] = ones


def _proj_t(x, w, *, tn):
    t, k = x.shape
    n = w.shape[1]
    tm = min(1024, t)
    tn = min(tn, n)
    return pl.pallas_call(
        _proj_t_kernel,
        grid=(t // tm, n // tn),
        in_specs=[pl.BlockSpec((tm, k), lambda i, j: (i, 0)),
                  pl.BlockSpec((k, tn), lambda i, j: (0, j))],
        out_specs=pl.BlockSpec((tn // LANE, V_ROWS, tm), lambda i, j: (j, 0, i)),
        out_shape=jax.ShapeDtypeStruct((n // LANE, V_ROWS, t), BF16),
        compiler_params=_cparams(("parallel", "parallel")),
        name="proj_t",
    )(x, w)


def _mm_res_kernel(x_ref, w_ref, r_ref, o_ref):
    o_ref[...] = r_ref[...] + _dot(x_ref[...], w_ref[...])


def _mm_res(x, w, res, *, tm, tn):
    t, k = x.shape
    n = w.shape[1]
    tm = min(tm, t)
    return pl.pallas_call(
        _mm_res_kernel,
        grid=(t // tm, n // tn),
        in_specs=[pl.BlockSpec((tm, k), lambda i, j: (i, 0)),
                  pl.BlockSpec((k, tn), lambda i, j: (0, j)),
                  pl.BlockSpec((tm, tn), lambda i, j: (i, j))],
        out_specs=pl.BlockSpec((tm, tn), lambda i, j: (i, j)),
        out_shape=jax.ShapeDtypeStruct((t, n), F32),
        compiler_params=_cparams(("parallel", "parallel")),
        name="mm_res",
    )(x, w, res)


def _gateup_kernel(x_ref, wg_ref, wu_ref, o_ref):
    x = x_ref[...]
    g = _dot(x, wg_ref[...])
    u = _dot(x, wu_ref[...])
    o_ref[...] = (g * jax.nn.sigmoid(g) * u).astype(o_ref.dtype)


def _gateup(x, wg, wu, *, tn):
    t, k = x.shape
    n = wg.shape[1]
    tm = min(1024, t)
    return pl.pallas_call(
        _gateup_kernel,
        grid=(t // tm, n // tn),
        in_specs=[pl.BlockSpec((tm, k), lambda i, j: (i, 0)),
                  pl.BlockSpec((k, tn), lambda i, j: (0, j)),
                  pl.BlockSpec((k, tn), lambda i, j: (0, j))],
        out_specs=pl.BlockSpec((tm, tn), lambda i, j: (i, j)),
        out_shape=jax.ShapeDtypeStruct((t, n), BF16),
        compiler_params=_cparams(("parallel", "parallel")),
        name="gateup",
    )(x, wg, wu)


def _ffn(h, g_norm, w_gate, w_up, w_down):
    a = _rmsnorm(h, g_norm, BF16)
    h1 = _gateup(a, w_gate.astype(BF16), w_up.astype(BF16), tn=256)
    return _mm_res(h1, w_down.astype(BF16), h, tm=512, tn=256)


V_ROWS = HEAD_DIM + 16


def _ones_rows(n_cols, dtype):
    row = lax.broadcasted_iota(jnp.int32, (V_ROWS - HEAD_DIM, n_cols), 0)
    return jnp.where(row == 0, 1.0, 0.0).astype(dtype)


def _flash_init(m_ref, acc_ref):
    m_ref[...] = jnp.full(m_ref.shape, NEG_INF, F32)
    acc_ref[...] = jnp.zeros(acc_ref.shape, F32)


def _flash_update(s_t, v_t, m_ref, acc_ref):
    m_prev = m_ref[...]
    m_new = jnp.maximum(m_prev, jnp.max(s_t, axis=0, keepdims=True))
    alpha = jnp.exp2(m_prev - m_new)
    p = jnp.exp2(s_t - m_new)
    acc_ref[...] = alpha * acc_ref[...] + _dot(v_t, p.astype(BF16))
    m_ref[...] = m_new


def _softmax_tiles(tiles):
    mx = functools.reduce(jnp.maximum, [jnp.max(s_t, axis=0, keepdims=True) for s_t, _ in tiles])
    acc = None
    for s_t, v_t in tiles:
        a = _dot(v_t, jnp.exp2(s_t - mx).astype(BF16))
        acc = a if acc is None else acc + a
    return mx, acc


def _normalized(acc):
    return acc[0:HEAD_DIM] / acc[HEAD_DIM:HEAD_DIM + 1]


def _split3(x):
    hi = x.astype(BF16)
    r = x - hi.astype(F32)
    mid = r.astype(BF16)
    lo = (r - mid.astype(F32)).astype(BF16)
    return hi, mid, lo


N_PIECES = 3


def _cum_kernel(f_ref, b_ref, tri_ref, place_ref, qx_ref, kx_ref, carry_ref):
    @pl.when(pl.program_id(0) == 0)
    def _():
        carry_ref[...] = jnp.zeros_like(carry_ref)

    x = f_ref[...] + b_ref[...]
    log_f = jnp.minimum(x, 0.0) - jnp.log1p(jnp.exp(-jnp.abs(x)))
    c = jnp.dot(tri_ref[...], log_f, precision=lax.Precision.HIGHEST,
                preferred_element_type=F32) + carry_ref[...]
    carry_ref[...] = c[c.shape[0] - 1:, :]
    pieces = jnp.concatenate(_split3(c * LOG2E), axis=1)
    lane = lax.broadcasted_iota(jnp.int32, (c.shape[0], LANE), 1)
    q_const = jnp.where((lane >= N_PIECES) & (lane < 2 * N_PIECES), -1.0, 0.0)
    k_const = jnp.where(lane < N_PIECES, 1.0, 0.0)
    for h in range(N_HEADS):
        placed = _dot(pieces, place_ref[h])
        qx_ref[h] = (placed[:, :LANE] + q_const).astype(BF16)
        kx_ref[h] = (placed[:, LANE:] + k_const).astype(BF16)


def _forget_extras(f, b):
    t, n = f.shape
    tb = min(512, t)
    tri = jnp.tril(jnp.ones((tb, tb), F32))
    h = jnp.arange(N_HEADS)[:, None, None]
    row = jnp.arange(N_PIECES * LANE)[None, :, None]
    col = jnp.arange(2 * LANE)[None, None, :]
    piece, head = row // LANE, row % LANE
    place = ((head == h) & ((col == piece) | (col == LANE + N_PIECES + piece))).astype(BF16)
    out = jax.ShapeDtypeStruct((N_HEADS, t, LANE), BF16)
    return pl.pallas_call(
        _cum_kernel,
        grid=(t // tb,),
        in_specs=[pl.BlockSpec((tb, n), lambda i: (i, 0)),
                  pl.BlockSpec((1, n), lambda i: (0, 0)),
                  pl.BlockSpec((tb, tb), lambda i: (0, 0)),
                  pl.BlockSpec((N_HEADS, N_PIECES * LANE, 2 * LANE), lambda i: (0, 0, 0))],
        out_specs=[pl.BlockSpec((N_HEADS, tb, LANE), lambda i: (0, i, 0)),
                   pl.BlockSpec((N_HEADS, tb, LANE), lambda i: (0, i, 0))],
        out_shape=[out, out],
        scratch_shapes=[pltpu.VMEM((1, n), F32)],
        compiler_params=_cparams(("arbitrary",)),
        name="forget_extras",
    )(f, b, tri, place)


def _pipelined_loop(first, trips, per_trip, scores, consume, s_ref):
    def body(t, carry):
        c = first + t * per_trip
        for u in range(per_trip):
            s_next = scores(c + u + 1)
            consume(c + u, s_ref[u % 2])
            s_ref[(u + 1) % 2] = s_next
        return carry

    lax.fori_loop(0, trips, body, 0)


def _pipelined_chunks(n_chunks, scores, consume, s_ref):
    quads = n_chunks // 4
    _pipelined_loop(0, quads, 4, scores, consume, s_ref)
    _pipelined_loop(4 * quads, (n_chunks % 4) // 2, 2, scores, consume, s_ref)


def _fox_kernel(q_ref, qx_ref, k_ref, kx_ref, vt_ref, o_ref, ka_ref, qa_ref, s_ref, m_ref, acc_ref, *, tq, tk):
    i = pl.program_id(1)
    assert tq == 2 * tk

    @pl.when(i == 0)
    def _():
        ka_ref[:, 0:HEAD_DIM] = k_ref[...]
        ka_ref[:, HEAD_DIM:] = kx_ref[...]

    qa_ref[:, 0:HEAD_DIM] = q_ref[...]
    qa_ref[:, HEAD_DIM:] = qx_ref[...]
    _flash_init(m_ref, acc_ref)

    def scores(c):
        off = pl.multiple_of(c * tk, tk)
        return _dot_nt(ka_ref[pl.ds(off, tk), :], qa_ref[...])

    def consume(c, s_t):
        off = pl.multiple_of(c * tk, tk)
        _flash_update(s_t, vt_ref[:, pl.ds(off, tk)], m_ref, acc_ref)

    s_ref[0] = scores(0)
    _pipelined_chunks(2 * i, scores, consume, s_ref)
    key = lax.broadcasted_iota(jnp.int32, (tk, tq), 0)
    qry = lax.broadcasted_iota(jnp.int32, (tk, tq), 1)
    s_last = scores(2 * i + 1)
    consume(2 * i, jnp.where(key <= qry, s_ref[0], NEG_INF))
    consume(2 * i + 1, jnp.where(key + tk <= qry, s_last, NEG_INF))
    o_ref[...] = _normalized(acc_ref[...]).T.astype(o_ref.dtype)


def _fox_attention(qkh, qx, kx, vt):
    t = qkh.shape[1]
    tq = min(1024, t)
    tk = tq // 2
    return pl.pallas_call(
        functools.partial(_fox_kernel, tq=tq, tk=tk),
        grid=(N_HEADS, t // tq),
        in_specs=[pl.BlockSpec((None, tq, HEAD_DIM), lambda h, i: (h, i, 0)),
                  pl.BlockSpec((None, tq, LANE), lambda h, i: (h, i, 0)),
                  pl.BlockSpec((None, t, HEAD_DIM), lambda h, i: (N_HEADS + h, 0, 0)),
                  pl.BlockSpec((None, t, LANE), lambda h, i: (h, 0, 0)),
                  pl.BlockSpec((None, V_ROWS, t), lambda h, i: (h, 0, 0))],
        out_specs=pl.BlockSpec((tq, HEAD_DIM), lambda h, i: (i, h)),
        out_shape=jax.ShapeDtypeStruct((t, D_MODEL), BF16),
        scratch_shapes=[pltpu.VMEM((t, 2 * HEAD_DIM), BF16),
                        pltpu.VMEM((tq, 2 * HEAD_DIM), BF16), pltpu.VMEM((2, tk, tq), F32),
                        pltpu.VMEM((1, tq), F32), pltpu.VMEM((V_ROWS, tq), F32)],
        compiler_params=_cparams(("parallel", "arbitrary")),
        name="fox_attention",
    )(qkh, qx, qkh, kx, vt)


def _fox_layer(h, g_norm, w_in, b_f, w_o):
    t = h.shape[0]
    a = _rmsnorm(h, g_norm, BF16)
    n_qk = 2 * D_MODEL
    w = w_in.astype(BF16)
    qkh = _proj(a, w[:, :n_qk], out_dtype=BF16, head_major=True, tn=512, n_scaled=D_MODEL)
    vt = _proj_t(a, w[:, n_qk:n_qk + D_MODEL], tn=512)
    w_f = jnp.pad(w[:, n_qk + D_MODEL:], ((0, 0), (0, LANE - N_HEADS)))
    f = _proj(a, w_f, out_dtype=F32, head_major=False, tn=LANE)
    b = jnp.pad(b_f.astype(F32), (0, LANE - N_HEADS)).reshape(1, LANE)
    qx, kx = _forget_extras(f, b)
    o = _fox_attention(qkh, qx, kx, vt)
    return _mm_res(o, w_o.astype(BF16), h, tm=1024, tn=512)


def _rel_bucket_const(dist):
    n = jnp.maximum(dist, 0)
    max_exact = REL_BUCKETS // 2
    nf = jnp.maximum(n, max_exact).astype(F32)
    large = max_exact + (jnp.log(nf / max_exact) / math.log(REL_MAX_DIST / max_exact)
                         * (REL_BUCKETS - max_exact)).astype(jnp.int32)
    large = jnp.minimum(large, REL_BUCKETS - 1)
    return jnp.where(n < max_exact, n, large)


def _bias_kernel(tab_ref, bkt_ref, o_ref):
    h = pl.program_id(0)
    bkt = bkt_ref[...]
    acc = jnp.zeros(bkt.shape, F32)
    for b in range(REL_BUCKETS):
        acc = jnp.where(bkt == b, tab_ref[b, h], acc)
    o_ref[...] = (acc - tab_ref[REL_BUCKETS - 1, h]) * LOG2E


def _bias_template(rel_table, bkt):
    r, c = bkt.shape
    return pl.pallas_call(
        _bias_kernel,
        grid=(N_HEADS,),
        in_specs=[pl.BlockSpec(memory_space=pltpu.SMEM),
                  pl.BlockSpec((r, c), lambda h: (0, 0))],
        out_specs=pl.BlockSpec((r, c), lambda h: (0, h)),
        out_shape=jax.ShapeDtypeStruct((r, N_HEADS * c), F32),
        compiler_params=_cparams(("arbitrary",)),
        name="bias_template",
    )(rel_table, bkt)


def _nsa_bias_templates(rel_table):
    tl = jnp.arange(Q_BLOCK)[None, :]
    bd = _bias_template(rel_table, _rel_bucket_const(tl + Q_BLOCK - jnp.arange(2 * Q_BLOCK)[:, None]))
    m = jnp.arange(CMP_NEAR)[:, None]
    bc = _bias_template(rel_table, _rel_bucket_const(
        tl - CMP_STRIDE * (m - (CMP_NEAR - 8)) - (CMP_BLOCK - 1)))
    return bd, bc


def _gelu_tanh(x):
    return 0.5 * x * (1.0 + jnp.tanh(math.sqrt(2.0 / math.pi) * (x + 0.044715 * (x * x * x))))


def _cmp_kernel(kb_ref, pos_ref, w1_ref, w2_ref, o_ref, *, nb):
    half = CMP_STRIDE * HEAD_DIM
    kb = kb_ref[...].astype(F32)
    xa = (kb + pos_ref[0:1, :]).astype(BF16)
    xb = (kb + pos_ref[1:2, :]).astype(BF16)
    a = _dot(xa, w1_ref[0:half, :])
    b = _dot(xb, w1_ref[half:2 * half, :])
    pre = a + pltpu.roll(b, nb - 1, axis=0)
    out = _dot(_gelu_tanh(pre).astype(BF16), w2_ref[...])
    row = lax.broadcasted_iota(jnp.int32, out.shape, 0)
    out = jnp.where(row < nb - 1, out, 0.0)
    o_ref[0:CMP_PAD, :] = jnp.zeros((CMP_PAD, HEAD_DIM), F32)
    o_ref[CMP_PAD:CMP_PAD + nb, :] = out


def _compress(kvh, cmp_pos, cmp_w1, cmp_w2):
    t = kvh.shape[1]
    nb = t // CMP_STRIDE
    kb = kvh[:2 * NSA_GROUPS].reshape(2 * NSA_GROUPS, nb, CMP_STRIDE * HEAD_DIM)
    pos = cmp_pos.astype(F32).reshape(2, 2, CMP_STRIDE * HEAD_DIM)
    return pl.pallas_call(
        functools.partial(_cmp_kernel, nb=nb),
        grid=(2 * NSA_GROUPS,),
        in_specs=[pl.BlockSpec((None, nb, CMP_STRIDE * HEAD_DIM), lambda j: (j, 0, 0)),
                  pl.BlockSpec((None, 2, CMP_STRIDE * HEAD_DIM), lambda j: (j // NSA_GROUPS, 0, 0)),
                  pl.BlockSpec((None, CMP_BLOCK * HEAD_DIM, HEAD_DIM), lambda j: (j // NSA_GROUPS, 0, 0)),
                  pl.BlockSpec((None, HEAD_DIM, HEAD_DIM), lambda j: (j // NSA_GROUPS, 0, 0))],
        out_specs=pl.BlockSpec((None, CMP_PAD + nb, HEAD_DIM), lambda j: (j, 0, 0)),
        out_shape=jax.ShapeDtypeStruct((2 * NSA_GROUPS, CMP_PAD + nb, HEAD_DIM), F32),
        compiler_params=_cparams(("parallel",)),
        name="compress",
    )(kb, pos, cmp_w1.astype(BF16), cmp_w2.astype(BF16))


def _dot_split_rhs(w, x):
    hi, mid, lo = _split3(x)
    return _dot(w, hi) + _dot(w, mid) + _dot(w, lo)


def _nsa_kernel(q_ref, g_ref, kc_ref, vc_ref, vct_ref, ksa_ref, vst_ref, kw_ref, vwt_ref,
                bd_ref, bc_ref, ovl_ref, o_ref,
                qa_ref, s_ref, sc_ref, imp_ref, m_ref, acc_ref, out_ref):
    i = pl.program_id(1)
    rows = NSA_ROWS
    q = q_ref[...].reshape(rows, HEAD_DIM)
    tl_lane = lax.broadcasted_iota(jnp.int32, (Q_BLOCK, rows), 1) % Q_BLOCK
    key_row = lax.broadcasted_iota(jnp.int32, (Q_BLOCK, rows), 0)
    sig_t = jax.nn.sigmoid(g_ref[...]).T

    def gate(branch):
        return jnp.concatenate([sig_t[3 * hg + branch:3 * hg + branch + 1, :] for hg in range(NSA_HG)], axis=1)

    def head_sum(p):
        acc = p[:, 0:Q_BLOCK]
        for hg in range(1, NSA_HG):
            acc = acc + p[:, hg * Q_BLOCK:(hg + 1) * Q_BLOCK]
        return acc

    n_first_near = 8 * i - (CMP_NEAR - 8)
    n_blocks = (jnp.maximum(n_first_near, 0) + CMP_ROWS - 1) // CMP_ROWS

    def far_rows(b):
        return pl.multiple_of(b * CMP_ROWS, CMP_ROWS)

    def scores_pass(b, mx):
        r0 = far_rows(b)
        kb = kc_ref[pl.ds(CMP_PAD + r0, CMP_ROWS), :].astype(BF16)
        row = r0 + lax.broadcasted_iota(jnp.int32, (CMP_ROWS, rows), 0)
        s = jnp.where(row < n_first_near, _dot_nt(kb, q), NEG_INF)
        sc_ref[pl.ds(r0, CMP_ROWS), :] = s
        return jnp.maximum(mx, jnp.max(s, axis=0, keepdims=True))

    mx_far = lax.fori_loop(0, n_blocks, scores_pass, jnp.full((1, rows), NEG_INF, F32))
    start = pl.multiple_of(8 * i + 8 + CMP_PAD - CMP_NEAR, 8)
    kcn = kc_ref[pl.ds(start, CMP_NEAR), :].astype(BF16)
    vcn_t = vc_ref[pl.ds(start, CMP_NEAR), :].T.astype(BF16)
    d_near = tl_lane - CMP_STRIDE * (key_row - (CMP_NEAR - 8)) - (CMP_BLOCK - 1)
    near_ok = (d_near >= 0) & (n_first_near + key_row >= 0)
    s_near = jnp.where(near_ok, _dot_nt(kcn, q) + bc_ref[...], NEG_INF)
    mx = jnp.maximum(mx_far, jnp.max(s_near, axis=0, keepdims=True))
    e_near = jnp.exp2(s_near - mx)

    def exp_pass(b, den):
        r0 = far_rows(b)
        e = jnp.exp2(sc_ref[pl.ds(r0, CMP_ROWS), :] - mx)
        sc_ref[pl.ds(r0, CMP_ROWS), :] = e
        return den + jnp.sum(e, axis=0, keepdims=True)

    den = lax.fori_loop(0, n_blocks, exp_pass, jnp.sum(e_near, axis=0, keepdims=True))
    inv = jnp.where(mx > 0.5 * NEG_INF, 1.0 / den, 0.0)
    p_near = e_near * inv
    sb = lax.broadcasted_iota(jnp.int32, (SEL_COLS, CMP_NEAR), 0)
    nn = n_first_near + lax.broadcasted_iota(jnp.int32, (SEL_COLS, CMP_NEAR), 1)
    ovl_near = jnp.where((nn >= 4 * sb - 1) & (nn <= 4 * sb + 3) & (nn >= 0), 1.0, 0.0).astype(BF16)
    out_ref[...] = _dot(vcn_t, p_near.astype(BF16))
    imp_ref[...] = _dot_split_rhs(ovl_near, head_sum(p_near))

    def out_pass(b, carry):
        r0 = far_rows(b)
        p = sc_ref[pl.ds(r0, CMP_ROWS), :] * inv
        out_ref[...] += _dot(vct_ref[:, pl.ds(CMP_PAD + r0, CMP_ROWS)].astype(BF16), p.astype(BF16))
        imp_ref[...] += _dot_split_rhs(ovl_ref[:, pl.ds(r0, CMP_ROWS)], head_sum(p))
        return carry

    lax.fori_loop(0, n_blocks, out_pass, 0)
    out_ref[...] = gate(0) * out_ref[...]

    imp = imp_ref[...]
    blk = lax.broadcasted_iota(jnp.int32, (SEL_COLS, Q_BLOCK), 0)
    tl = lax.broadcasted_iota(jnp.int32, (SEL_COLS, Q_BLOCK), 1)
    cur = 2 * i + (tl >= SEL_BLOCK).astype(jnp.int32)
    forced = (blk == 0) | (blk == cur) | (blk == cur - 1)
    causal_blk = blk * SEL_BLOCK <= Q_BLOCK * i + tl
    work = jnp.where(forced, FORCED_SCORE, jnp.where(causal_blk, imp, -1.0))
    blk_f = blk.astype(F32)
    sel = jnp.zeros((SEL_COLS, Q_BLOCK), F32)
    for _ in range(SEL_TOPK):
        best = jnp.max(work, axis=0, keepdims=True)
        first = jnp.min(jnp.where(work == best, blk_f, float(SEL_COLS)), axis=0, keepdims=True)
        pick = blk_f == first
        sel = jnp.where(pick, 1.0, sel)
        work = jnp.where(pick, -2.0, work)
    amask = jnp.where(sel > 0.0, 0.0, NEG_INF)
    for half in range(SEL_COLS // LANE):
        a_t = amask[half * LANE:(half + 1) * LANE, :].T.astype(BF16)
        qa_ref[half, :, 0:HEAD_DIM] = q
        qa_ref[half, :, HEAD_DIM:2 * HEAD_DIM] = jnp.concatenate([a_t] * NSA_HG, axis=0)

    def sel_chunk(tile, n_keys):
        off = pl.multiple_of(tile * Q_BLOCK, Q_BLOCK)
        qa = qa_ref[tile // (LANE // 2)]
        return _dot_nt(ksa_ref[pl.ds(off, n_keys), :], qa), vst_ref[:, pl.ds(off, n_keys)]

    s_t, v_t = sel_chunk(i, Q_BLOCK)
    near = [(jnp.where(key_row <= tl_lane, s_t + bd_ref[Q_BLOCK:, :], NEG_INF), v_t)]
    s_t, v_t = sel_chunk(jnp.maximum(i - 1, 0), Q_BLOCK)
    near.append((jnp.where(i >= 1, s_t + bd_ref[:Q_BLOCK, :], NEG_INF), v_t))
    m_ref[...], acc_ref[...] = _softmax_tiles(near)
    n_far = jnp.maximum(i - 1, 0)
    n_chunks = n_far // FAR_TILES
    chunk_keys = FAR_TILES * Q_BLOCK

    def far_scores(c):
        tile = FAR_TILES * jnp.minimum(c, jnp.maximum(n_chunks - 1, 0))
        off = pl.multiple_of(tile * Q_BLOCK, chunk_keys)
        return _dot_nt(ksa_ref[pl.ds(off, chunk_keys), :], qa_ref[tile // (LANE // 2)])

    def far_consume(c, s_t):
        off = pl.multiple_of(c * chunk_keys, chunk_keys)
        _flash_update(s_t, vst_ref[:, pl.ds(off, chunk_keys)], m_ref, acc_ref)

    s_ref[0] = far_scores(0)
    _pipelined_chunks(n_chunks, far_scores, far_consume, s_ref)

    @pl.when(n_chunks % 2 == 1)
    def _():
        far_consume(n_chunks - 1, s_ref[0])

    left = n_far % FAR_TILES

    @pl.when(left >= 2)
    def _():
        s_t, v_t = sel_chunk(n_far - left, 2 * Q_BLOCK)
        _flash_update(s_t, v_t, m_ref, acc_ref)

    @pl.when(left % 2 == 1)
    def _():
        s_t, v_t = sel_chunk(n_far - 1, Q_BLOCK)
        _flash_update(s_t, v_t, m_ref, acc_ref)

    out_ref[...] += gate(1) * _normalized(acc_ref[...])

    def win_chunk(tile):
        off = pl.multiple_of(jnp.maximum(tile, 0) * Q_BLOCK, Q_BLOCK)
        return _dot_nt(kw_ref[pl.ds(off, Q_BLOCK), :], q), vwt_ref[:, pl.ds(off, Q_BLOCK)]

    s_t, v_t = win_chunk(i)
    win = [(jnp.where(key_row <= tl_lane, s_t + bd_ref[Q_BLOCK:, :], NEG_INF), v_t)]
    s_t, v_t = win_chunk(i - 1)
    win.append((jnp.where(i >= 1, s_t + bd_ref[:Q_BLOCK, :], NEG_INF), v_t))
    for back in (2, 3):
        s_t, v_t = win_chunk(i - back)
        win.append((jnp.where(i >= back, s_t, NEG_INF), v_t))
    s_t, v_t = win_chunk(i - 4)
    win.append((jnp.where((key_row > tl_lane) & (i >= 4), s_t, NEG_INF), v_t))
    _, acc_w = _softmax_tiles(win)
    o = out_ref[...] + gate(2) * _normalized(acc_w)
    for hg in range(NSA_HG):
        o_ref[:, hg * HEAD_DIM:(hg + 1) * HEAD_DIM] = o[:, hg * Q_BLOCK:(hg + 1) * Q_BLOCK].T.astype(o_ref.dtype)


def _nsa_attention(qkh, gates, kcv, kcv_t, ksa, vt, bd, bc, ovl):
    t = qkh.shape[1]
    ncp = t // CMP_STRIDE
    assert ncp % CMP_ROWS == 0
    resident = functools.partial(pl.BlockSpec, pipeline_mode=pl.Buffered(1))
    return pl.pallas_call(
        _nsa_kernel,
        grid=(NSA_GROUPS, t // Q_BLOCK),
        in_specs=[pl.BlockSpec((NSA_HG, Q_BLOCK, HEAD_DIM), lambda g, i: (g, i, 0)),
                  pl.BlockSpec((None, Q_BLOCK, LANE), lambda g, i: (g, i, 0)),
                  resident((None, CMP_PAD + ncp, HEAD_DIM), lambda g, i: (g, 0, 0)),
                  resident((None, CMP_PAD + ncp, HEAD_DIM), lambda g, i: (NSA_GROUPS + g, 0, 0)),
                  resident((None, HEAD_DIM, CMP_PAD + ncp), lambda g, i: (NSA_GROUPS + g, 0, 0)),
                  resident((None, t, 2 * HEAD_DIM), lambda g, i: (g, 0, 0)),
                  resident((None, V_ROWS, t), lambda g, i: (g, 0, 0)),
                  resident((None, t, HEAD_DIM), lambda g, i: (N_HEADS + 3 * NSA_GROUPS + g, 0, 0)),
                  resident((None, V_ROWS, t), lambda g, i: (NSA_GROUPS + g, 0, 0)),
                  resident((2 * Q_BLOCK, NSA_ROWS), lambda g, i: (0, g)),
                  resident((CMP_NEAR, NSA_ROWS), lambda g, i: (0, g)),
                  resident((SEL_COLS, ncp), lambda g, i: (0, 0))],
        out_specs=pl.BlockSpec((Q_BLOCK, NSA_HG * HEAD_DIM), lambda g, i: (i, g)),
        out_shape=jax.ShapeDtypeStruct((t, D_MODEL), BF16),
        scratch_shapes=[pltpu.VMEM((SEL_COLS // LANE, NSA_ROWS, 2 * HEAD_DIM), BF16),
                        pltpu.VMEM((2, FAR_TILES * Q_BLOCK, NSA_ROWS), F32),
                        pltpu.VMEM((ncp, NSA_ROWS), F32), pltpu.VMEM((SEL_COLS, Q_BLOCK), F32),
                        pltpu.VMEM((1, NSA_ROWS), F32),
                        pltpu.VMEM((V_ROWS, NSA_ROWS), F32), pltpu.VMEM((HEAD_DIM, NSA_ROWS), F32)],
        compiler_params=_cparams(("parallel", "arbitrary")),
        name="nsa_attention",
    )(qkh, gates, kcv, kcv, kcv_t, ksa, vt, qkh, vt, bd, bc, ovl)


def _nsa_layer(h, g_norm, w_in, w_o, cmp_pos, cmp_w1, cmp_w2, bd, bc):
    t = h.shape[0]
    assert t // SEL_BLOCK <= SEL_COLS
    a = _rmsnorm(h, g_norm, BF16)
    n_qkv = D_MODEL + 6 * NSA_KV_DIM
    w = w_in[:, :n_qkv].astype(BF16)

    def kv_cols(j):
        return w[:, D_MODEL + j * NSA_KV_DIM:D_MODEL + (j + 1) * NSA_KV_DIM]

    w_rows = jnp.concatenate([w[:, :D_MODEL], kv_cols(0), kv_cols(1), kv_cols(2), kv_cols(4)], axis=1)
    qkh = _proj(a, w_rows, out_dtype=BF16, head_major=True, tn=512, n_scaled=D_MODEL)
    vt = _proj_t(a, jnp.concatenate([kv_cols(3), kv_cols(5)], axis=1), tn=512)
    w_g = w_in[:, n_qkv:].reshape(D_MODEL, NSA_GROUPS, 3 * NSA_HG)
    w_g = jnp.pad(w_g, ((0, 0), (0, 0), (0, LANE - 3 * NSA_HG))).reshape(D_MODEL, NSA_GROUPS * LANE)
    gates = _proj(a, w_g.astype(BF16), out_dtype=F32, head_major=True, tn=NSA_GROUPS * LANE)
    kcv = _compress(qkh[N_HEADS:N_HEADS + 2 * NSA_GROUPS], cmp_pos, cmp_w1, cmp_w2)
    kcv_t = jnp.swapaxes(kcv, 1, 2)
    key_blk = (jnp.arange(t) // SEL_BLOCK) % LANE
    onehot = (key_blk[:, None] == jnp.arange(LANE)[None, :]).astype(BF16)
    ksa = jnp.concatenate([qkh[N_HEADS + 2 * NSA_GROUPS:N_HEADS + 3 * NSA_GROUPS],
                           jnp.broadcast_to(onehot, (NSA_GROUPS, t, LANE))], axis=-1)
    n = jnp.arange(t // CMP_STRIDE)[None, :]
    sblk = jnp.arange(SEL_COLS)[:, None]
    ovl = ((n >= 4 * sblk - 1) & (n <= 4 * sblk + 3)).astype(BF16)
    o = _nsa_attention(qkh, gates, kcv, kcv_t, ksa, vt, bd, bc, ovl)
    return _mm_res(o, w_o.astype(BF16), h, tm=1024, tn=512)


def kernel(x, norm_mix, norm_ffn, norm_final, rel_table, nsa_w_in, nsa_w_o, nsa_cmp_pos, nsa_cmp_w1,
           nsa_cmp_w2, fox_w_in, fox_b_f, fox_w_o, ffn_w_gate, ffn_w_up, ffn_w_down):
    b, t, d = x.shape
    depth = norm_mix.shape[0]
    bd, bc = _nsa_bias_templates(rel_table.astype(F32))
    outs = []
    for bi in range(b):
        h = x[bi]
        for i in range(depth):
            j = i // 2
            if i % 2 == 0:
                h = _nsa_layer(h, norm_mix[i], nsa_w_in[j], nsa_w_o[j], nsa_cmp_pos[j], nsa_cmp_w1[j],
                               nsa_cmp_w2[j], bd, bc)
            else:
                h = _fox_layer(h, norm_mix[i], fox_w_in[j], fox_b_f[j], fox_w_o[j])
            h = _ffn(h, norm_ffn[i], ffn_w_gate[i], ffn_w_up[i], ffn_w_down[i])
        outs.append(_rmsnorm(h, norm_final, F32))
    return jnp.stack(outs, axis=0)
```

```python
import functools
import math

import jax
import jax.numpy as jnp
from jax import lax
from jax.experimental import pallas as pl
from jax.experimental.pallas import tpu as pltpu

F32 = jnp.float32
BF16 = jnp.bfloat16

D_MODEL = 4096
HEAD_DIM = 128
N_HEADS = D_MODEL // HEAD_DIM
NSA_GROUPS = 4
NSA_HG = N_HEADS // NSA_GROUPS
NSA_KV_DIM = NSA_GROUPS * HEAD_DIM
CMP_BLOCK = 32
CMP_STRIDE = 16
SEL_BLOCK = 64
SEL_TOPK = 16
WINDOW = 512
REL_BUCKETS = 32
REL_MAX_DIST = 128
Q_BLOCK = 128
RMS_EPS = 1e-6
NEG_INF = -1e30
FORCED_SCORE = 1e6
LOG2E = 1.4426950408889634
Q_SCALE = HEAD_DIM ** -0.5 * LOG2E

LANE = 128
SEL_COLS = 256
CMP_PAD = 128
CMP_NEAR = 128
CMP_ROWS = 256
FAR_TILES = 4
NSA_ROWS = NSA_HG * Q_BLOCK
VMEM_LIMIT = 56 * 1024 * 1024


def _cparams(sem):
    return pltpu.CompilerParams(dimension_semantics=sem, vmem_limit_bytes=VMEM_LIMIT)


def _dot(a, b):
    return jnp.dot(a, b, preferred_element_type=F32)


def _dot_nt(a, b):
    return lax.dot_general(a, b, (((1,), (1,)), ((), ())), preferred_element_type=F32)


def _rmsnorm_kernel(x_ref, g_ref, o_ref):
    x = x_ref[...]
    ms = jnp.mean(x * x, axis=-1, keepdims=True)
    o_ref[...] = (x * lax.rsqrt(ms + RMS_EPS) * g_ref[...]).astype(o_ref.dtype)


def _rmsnorm(x, g, out_dtype):
    t, d = x.shape
    tm = min(256, t)
    return pl.pallas_call(
        _rmsnorm_kernel,
        grid=(t // tm,),
        in_specs=[pl.BlockSpec((tm, d), lambda i: (i, 0)),
                  pl.BlockSpec((1, d), lambda i: (0, 0))],
        out_specs=pl.BlockSpec((tm, d), lambda i: (i, 0)),
        out_shape=jax.ShapeDtypeStruct((t, d), out_dtype),
        compiler_params=_cparams(("parallel",)),
        name="rmsnorm",
    )(x, g.reshape(1, d))


def _proj_kernel(x_ref, w_ref, o_ref, *, head_major, n_scaled_blocks):
    r = _dot(x_ref[...], w_ref[...].astype(BF16))
    if n_scaled_blocks:
        r = r * jnp.where(pl.program_id(1) < n_scaled_blocks, Q_SCALE, 1.0)
    if head_major:
        for s in range(o_ref.shape[0]):
            o_ref[s] = r[:, s * LANE:(s + 1) * LANE].astype(o_ref.dtype)
    else:
        o_ref[...] = r.astype(o_ref.dtype)


def _identity(j):
    return j


def _proj(x, w, layer, *, n_out, out_dtype, head_major, tn, col_block=_identity, n_scaled=0):
    t, k = x.shape
    n = n_out
    tm = min(1024, t)
    tn = min(tn, n)
    assert n_scaled % tn == 0 and n % tn == 0
    if head_major:
        out_shape = jax.ShapeDtypeStruct((n // LANE, t, LANE), out_dtype)
        out_spec = pl.BlockSpec((tn // LANE, tm, LANE), lambda i, j: (j, i, 0))
    else:
        out_shape = jax.ShapeDtypeStruct((t, n), out_dtype)
        out_spec = pl.BlockSpec((tm, tn), lambda i, j: (i, j))
    return pl.pallas_call(
        functools.partial(_proj_kernel, head_major=head_major, n_scaled_blocks=n_scaled // tn),
        grid=(t // tm, n // tn),
        in_specs=[pl.BlockSpec((tm, k), lambda i, j: (i, 0)),
                  pl.BlockSpec((None, k, tn), lambda i, j: (layer, 0, col_block(j)))],
        out_specs=out_spec,
        out_shape=out_shape,
        compiler_params=_cparams(("parallel", "parallel")),
        name="proj",
    )(x, w)


def _proj_t_kernel(x_ref, w_ref, o_ref):
    r = _dot(x_ref[...], w_ref[...].astype(BF16))
    ones = _ones_rows(r.shape[0], o_ref.dtype)
    for s in range(o_ref.shape[0]):
        o_ref[s, 0:HEAD_DIM, :] = r[:, s * LANE:(s + 1) * LANE].T.astype(o_ref.dtype)
        o_ref[s, HEAD_DIM:, :] = ones


def _proj_t(x, w, layer, *, n_out, tn, col_block=_identity):
    t, k = x.shape
    n = n_out
    tm = min(1024, t)
    tn = min(tn, n)
    return pl.pallas_call(
        _proj_t_kernel,
        grid=(t // tm, n // tn),
        in_specs=[pl.BlockSpec((tm, k), lambda i, j: (i, 0)),
                  pl.BlockSpec((None, k, tn), lambda i, j: (layer, 0, col_block(j)))],
        out_specs=pl.BlockSpec((tn // LANE, V_ROWS, tm), lambda i, j: (j, 0, i)),
        out_shape=jax.ShapeDtypeStruct((n // LANE, V_ROWS, t), BF16),
        compiler_params=_cparams(("parallel", "parallel")),
        name="proj_t",
    )(x, w)


def _mm_res_kernel(x_ref, w_ref, r_ref, o_ref):
    o_ref[...] = r_ref[...] + _dot(x_ref[...], w_ref[...])


def _mm_res(x, w, layer, res, *, tm, tn):
    t, k = x.shape
    n = w.shape[2]
    tm = min(tm, t)
    return pl.pallas_call(
        _mm_res_kernel,
        grid=(t // tm, n // tn),
        in_specs=[pl.BlockSpec((tm, k), lambda i, j: (i, 0)),
                  pl.BlockSpec((None, k, tn), lambda i, j: (layer, 0, j)),
                  pl.BlockSpec((tm, tn), lambda i, j: (i, j))],
        out_specs=pl.BlockSpec((tm, tn), lambda i, j: (i, j)),
        out_shape=jax.ShapeDtypeStruct((t, n), F32),
        compiler_params=_cparams(("parallel", "parallel")),
        name="mm_res",
    )(x, w, res)


def _gateup_kernel(x_ref, wg_ref, wu_ref, o_ref):
    x = x_ref[...]
    g = _dot(x, wg_ref[...].astype(BF16))
    u = _dot(x, wu_ref[...].astype(BF16))
    o_ref[...] = (g * jax.nn.sigmoid(g) * u).astype(o_ref.dtype)


def _gateup(x, wg, wu, layer, *, tn):
    t, k = x.shape
    n = wg.shape[2]
    tm = min(1024, t)
    return pl.pallas_call(
        _gateup_kernel,
        grid=(t // tm, n // tn),
        in_specs=[pl.BlockSpec((tm, k), lambda i, j: (i, 0)),
                  pl.BlockSpec((None, k, tn), lambda i, j: (layer, 0, j)),
                  pl.BlockSpec((None, k, tn), lambda i, j: (layer, 0, j))],
        out_specs=pl.BlockSpec((tm, tn), lambda i, j: (i, j)),
        out_shape=jax.ShapeDtypeStruct((t, n), BF16),
        compiler_params=_cparams(("parallel", "parallel")),
        name="gateup",
    )(x, wg, wu)


def _ffn(h, g_norm, w_gate, w_up, w_down, layer):
    a = _rmsnorm(h, g_norm, BF16)
    h1 = _gateup(a, w_gate, w_up, layer, tn=256)
    return _mm_res(h1, w_down, layer, h, tm=512, tn=256)


V_ROWS = HEAD_DIM + 16


def _ones_rows(n_cols, dtype):
    row = lax.broadcasted_iota(jnp.int32, (V_ROWS - HEAD_DIM, n_cols), 0)
    return jnp.where(row == 0, 1.0, 0.0).astype(dtype)


def _flash_init(m_ref, acc_ref):
    m_ref[...] = jnp.full(m_ref.shape, NEG_INF, F32)
    acc_ref[...] = jnp.zeros(acc_ref.shape, F32)


def _flash_update(s_t, v_t, m_ref, acc_ref):
    m_prev = m_ref[...]
    m_new = jnp.maximum(m_prev, jnp.max(s_t, axis=0, keepdims=True))
    alpha = jnp.exp2(m_prev - m_new)
    p = jnp.exp2(s_t - m_new)
    acc_ref[...] = alpha * acc_ref[...] + _dot(v_t, p.astype(BF16))
    m_ref[...] = m_new


def _softmax_tiles(tiles):
    mx = functools.reduce(jnp.maximum, [jnp.max(s_t, axis=0, keepdims=True) for s_t, _ in tiles])
    acc = None
    for s_t, v_t in tiles:
        a = _dot(v_t, jnp.exp2(s_t - mx).astype(BF16))
        acc = a if acc is None else acc + a
    return mx, acc


def _normalized(acc):
    return acc[0:HEAD_DIM] / acc[HEAD_DIM:HEAD_DIM + 1]


def _split3(x):
    hi = x.astype(BF16)
    r = x - hi.astype(F32)
    mid = r.astype(BF16)
    lo = (r - mid.astype(F32)).astype(BF16)
    return hi, mid, lo


N_PIECES = 3


def _cum_kernel(f_ref, b_ref, tri_ref, place_ref, qx_ref, kx_ref, carry_ref):
    @pl.when(pl.program_id(0) == 0)
    def _():
        carry_ref[...] = jnp.zeros_like(carry_ref)

    x = f_ref[...] + b_ref[...]
    log_f = jnp.minimum(x, 0.0) - jnp.log1p(jnp.exp(-jnp.abs(x)))
    c = jnp.dot(tri_ref[...], log_f, precision=lax.Precision.HIGHEST,
                preferred_element_type=F32) + carry_ref[...]
    carry_ref[...] = c[c.shape[0] - 1:, :]
    pieces = jnp.concatenate(_split3(c * LOG2E), axis=1)
    lane = lax.broadcasted_iota(jnp.int32, (c.shape[0], LANE), 1)
    q_const = jnp.where((lane >= N_PIECES) & (lane < 2 * N_PIECES), -1.0, 0.0)
    k_const = jnp.where(lane < N_PIECES, 1.0, 0.0)
    for h in range(N_HEADS):
        placed = _dot(pieces, place_ref[h])
        qx_ref[h] = (placed[:, :LANE] + q_const).astype(BF16)
        kx_ref[h] = (placed[:, LANE:] + k_const).astype(BF16)


def _forget_extras(f, b):
    t, n = f.shape
    tb = min(512, t)
    tri = jnp.tril(jnp.ones((tb, tb), F32))
    h = jnp.arange(N_HEADS)[:, None, None]
    row = jnp.arange(N_PIECES * LANE)[None, :, None]
    col = jnp.arange(2 * LANE)[None, None, :]
    piece, head = row // LANE, row % LANE
    place = ((head == h) & ((col == piece) | (col == LANE + N_PIECES + piece))).astype(BF16)
    out = jax.ShapeDtypeStruct((N_HEADS, t, LANE), BF16)
    return pl.pallas_call(
        _cum_kernel,
        grid=(t // tb,),
        in_specs=[pl.BlockSpec((tb, n), lambda i: (i, 0)),
                  pl.BlockSpec((1, n), lambda i: (0, 0)),
                  pl.BlockSpec((tb, tb), lambda i: (0, 0)),
                  pl.BlockSpec((N_HEADS, N_PIECES * LANE, 2 * LANE), lambda i: (0, 0, 0))],
        out_specs=[pl.BlockSpec((N_HEADS, tb, LANE), lambda i: (0, i, 0)),
                   pl.BlockSpec((N_HEADS, tb, LANE), lambda i: (0, i, 0))],
        out_shape=[out, out],
        scratch_shapes=[pltpu.VMEM((1, n), F32)],
        compiler_params=_cparams(("arbitrary",)),
        name="forget_extras",
    )(f, b, tri, place)


def _pipelined_loop(first, trips, per_trip, scores, consume, s_ref):
    def body(t, carry):
        c = first + t * per_trip
        for u in range(per_trip):
            s_next = scores(c + u + 1)
            consume(c + u, s_ref[u % 2])
            s_ref[(u + 1) % 2] = s_next
        return carry

    lax.fori_loop(0, trips, body, 0)


def _pipelined_chunks(n_chunks, scores, consume, s_ref):
    quads = n_chunks // 4
    _pipelined_loop(0, quads, 4, scores, consume, s_ref)
    _pipelined_loop(4 * quads, (n_chunks % 4) // 2, 2, scores, consume, s_ref)


def _fox_kernel(q_ref, qx_ref, k_ref, kx_ref, vt_ref, o_ref, ka_ref, qa_ref, s_ref, m_ref, acc_ref, *, tq, tk):
    i = pl.program_id(1)
    assert tq == 2 * tk

    @pl.when(i == 0)
    def _():
        ka_ref[:, 0:HEAD_DIM] = k_ref[...]
        ka_ref[:, HEAD_DIM:] = kx_ref[...]

    qa_ref[:, 0:HEAD_DIM] = q_ref[...]
    qa_ref[:, HEAD_DIM:] = qx_ref[...]
    _flash_init(m_ref, acc_ref)

    def scores(c):
        off = pl.multiple_of(c * tk, tk)
        return _dot_nt(ka_ref[pl.ds(off, tk), :], qa_ref[...])

    def consume(c, s_t):
        off = pl.multiple_of(c * tk, tk)
        _flash_update(s_t, vt_ref[:, pl.ds(off, tk)], m_ref, acc_ref)

    s_ref[0] = scores(0)
    _pipelined_chunks(2 * i, scores, consume, s_ref)
    key = lax.broadcasted_iota(jnp.int32, (tk, tq), 0)
    qry = lax.broadcasted_iota(jnp.int32, (tk, tq), 1)
    s_last = scores(2 * i + 1)
    consume(2 * i, jnp.where(key <= qry, s_ref[0], NEG_INF))
    consume(2 * i + 1, jnp.where(key + tk <= qry, s_last, NEG_INF))
    o_ref[...] = _normalized(acc_ref[...]).T.astype(o_ref.dtype)


def _fox_attention(qkh, qx, kx, vt):
    t = qkh.shape[1]
    tq = min(1024, t)
    tk = tq // 2
    return pl.pallas_call(
        functools.partial(_fox_kernel, tq=tq, tk=tk),
        grid=(N_HEADS, t // tq),
        in_specs=[pl.BlockSpec((None, tq, HEAD_DIM), lambda h, i: (h, i, 0)),
                  pl.BlockSpec((None, tq, LANE), lambda h, i: (h, i, 0)),
                  pl.BlockSpec((None, t, HEAD_DIM), lambda h, i: (N_HEADS + h, 0, 0)),
                  pl.BlockSpec((None, t, LANE), lambda h, i: (h, 0, 0)),
                  pl.BlockSpec((None, V_ROWS, t), lambda h, i: (h, 0, 0))],
        out_specs=pl.BlockSpec((tq, HEAD_DIM), lambda h, i: (i, h)),
        out_shape=jax.ShapeDtypeStruct((t, D_MODEL), BF16),
        scratch_shapes=[pltpu.VMEM((t, 2 * HEAD_DIM), BF16),
                        pltpu.VMEM((tq, 2 * HEAD_DIM), BF16), pltpu.VMEM((2, tk, tq), F32),
                        pltpu.VMEM((1, tq), F32), pltpu.VMEM((V_ROWS, tq), F32)],
        compiler_params=_cparams(("parallel", "arbitrary")),
        name="fox_attention",
    )(qkh, qx, qkh, kx, vt)


def _fox_layer(h, g_norm, w_in, b_f, w_o, layer):
    a = _rmsnorm(h, g_norm, BF16)
    tn = 512
    qkh = _proj(a, w_in, layer, n_out=2 * D_MODEL, out_dtype=BF16, head_major=True, tn=tn,
                n_scaled=D_MODEL)
    vt = _proj_t(a, w_in, layer, n_out=D_MODEL, tn=tn, col_block=lambda j: 2 * D_MODEL // tn + j)
    w_f = jnp.pad(w_in[layer][:, 3 * D_MODEL:], ((0, 0), (0, LANE - N_HEADS)))[None]
    f = _proj(a, w_f, 0, n_out=LANE, out_dtype=F32, head_major=False, tn=LANE)
    b = jnp.pad(b_f.astype(F32), (0, LANE - N_HEADS)).reshape(1, LANE)
    qx, kx = _forget_extras(f, b)
    o = _fox_attention(qkh, qx, kx, vt)
    return _mm_res(o, w_o, layer, h, tm=1024, tn=512)


def _rel_bucket_const(dist):
    n = jnp.maximum(dist, 0)
    max_exact = REL_BUCKETS // 2
    nf = jnp.maximum(n, max_exact).astype(F32)
    large = max_exact + (jnp.log(nf / max_exact) / math.log(REL_MAX_DIST / max_exact)
                         * (REL_BUCKETS - max_exact)).astype(jnp.int32)
    large = jnp.minimum(large, REL_BUCKETS - 1)
    return jnp.where(n < max_exact, n, large)


def _bias_kernel(tab_ref, bkt_ref, o_ref):
    h = pl.program_id(0)
    bkt = bkt_ref[...]
    acc = jnp.zeros(bkt.shape, F32)
    for b in range(REL_BUCKETS):
        acc = jnp.where(bkt == b, tab_ref[b, h], acc)
    o_ref[...] = (acc - tab_ref[REL_BUCKETS - 1, h]) * LOG2E


def _bias_template(rel_table, bkt):
    r, c = bkt.shape
    return pl.pallas_call(
        _bias_kernel,
        grid=(N_HEADS,),
        in_specs=[pl.BlockSpec(memory_space=pltpu.SMEM),
                  pl.BlockSpec((r, c), lambda h: (0, 0))],
        out_specs=pl.BlockSpec((r, c), lambda h: (0, h)),
        out_shape=jax.ShapeDtypeStruct((r, N_HEADS * c), F32),
        compiler_params=_cparams(("arbitrary",)),
        name="bias_template",
    )(rel_table, bkt)


def _nsa_bias_templates(rel_table):
    tl = jnp.arange(Q_BLOCK)[None, :]
    bd = _bias_template(rel_table, _rel_bucket_const(tl + Q_BLOCK - jnp.arange(2 * Q_BLOCK)[:, None]))
    m = jnp.arange(CMP_NEAR)[:, None]
    bc = _bias_template(rel_table, _rel_bucket_const(
        tl - CMP_STRIDE * (m - (CMP_NEAR - 8)) - (CMP_BLOCK - 1)))
    return bd, bc


def _gelu_tanh(x):
    return 0.5 * x * (1.0 + jnp.tanh(math.sqrt(2.0 / math.pi) * (x + 0.044715 * (x * x * x))))


def _cmp_kernel(kb_ref, pos_ref, w1_ref, w2_ref, o_ref, *, nb):
    half = CMP_STRIDE * HEAD_DIM
    kb = kb_ref[...].astype(F32)
    xa = (kb + pos_ref[0:1, :]).astype(BF16)
    xb = (kb + pos_ref[1:2, :]).astype(BF16)
    a = _dot(xa, w1_ref[0:half, :])
    b = _dot(xb, w1_ref[half:2 * half, :])
    pre = a + pltpu.roll(b, nb - 1, axis=0)
    out = _dot(_gelu_tanh(pre).astype(BF16), w2_ref[...])
    row = lax.broadcasted_iota(jnp.int32, out.shape, 0)
    out = jnp.where(row < nb - 1, out, 0.0)
    o_ref[0:CMP_PAD, :] = jnp.zeros((CMP_PAD, HEAD_DIM), F32)
    o_ref[CMP_PAD:CMP_PAD + nb, :] = out


def _compress(kvh, cmp_pos, cmp_w1, cmp_w2):
    t = kvh.shape[1]
    nb = t // CMP_STRIDE
    kb = kvh[:2 * NSA_GROUPS].reshape(2 * NSA_GROUPS, nb, CMP_STRIDE * HEAD_DIM)
    pos = cmp_pos.astype(F32).reshape(2, 2, CMP_STRIDE * HEAD_DIM)
    return pl.pallas_call(
        functools.partial(_cmp_kernel, nb=nb),
        grid=(2 * NSA_GROUPS,),
        in_specs=[pl.BlockSpec((None, nb, CMP_STRIDE * HEAD_DIM), lambda j: (j, 0, 0)),
                  pl.BlockSpec((None, 2, CMP_STRIDE * HEAD_DIM), lambda j: (j // NSA_GROUPS, 0, 0)),
                  pl.BlockSpec((None, CMP_BLOCK * HEAD_DIM, HEAD_DIM), lambda j: (j // NSA_GROUPS, 0, 0)),
                  pl.BlockSpec((None, HEAD_DIM, HEAD_DIM), lambda j: (j // NSA_GROUPS, 0, 0))],
        out_specs=pl.BlockSpec((None, CMP_PAD + nb, HEAD_DIM), lambda j: (j, 0, 0)),
        out_shape=jax.ShapeDtypeStruct((2 * NSA_GROUPS, CMP_PAD + nb, HEAD_DIM), F32),
        compiler_params=_cparams(("parallel",)),
        name="compress",
    )(kb, pos, cmp_w1.astype(BF16), cmp_w2.astype(BF16))


def _dot_split_rhs(w, x):
    hi, mid, lo = _split3(x)
    return _dot(w, hi) + _dot(w, mid) + _dot(w, lo)


def _nsa_kernel(q_ref, g_ref, kc_ref, vc_ref, vct_ref, ksa_ref, vst_ref, kw_ref, vwt_ref,
                bd_ref, bc_ref, ovl_ref, o_ref,
                qa_ref, s_ref, sc_ref, imp_ref, m_ref, acc_ref, out_ref):
    i = pl.program_id(1)
    rows = NSA_ROWS
    q = q_ref[...].reshape(rows, HEAD_DIM)
    tl_lane = lax.broadcasted_iota(jnp.int32, (Q_BLOCK, rows), 1) % Q_BLOCK
    key_row = lax.broadcasted_iota(jnp.int32, (Q_BLOCK, rows), 0)
    sig_t = jax.nn.sigmoid(g_ref[...]).T

    def gate(branch):
        return jnp.concatenate([sig_t[3 * hg + branch:3 * hg + branch + 1, :] for hg in range(NSA_HG)], axis=1)

    def head_sum(p):
        acc = p[:, 0:Q_BLOCK]
        for hg in range(1, NSA_HG):
            acc = acc + p[:, hg * Q_BLOCK:(hg + 1) * Q_BLOCK]
        return acc

    n_first_near = 8 * i - (CMP_NEAR - 8)
    n_blocks = (jnp.maximum(n_first_near, 0) + CMP_ROWS - 1) // CMP_ROWS

    def far_rows(b):
        return pl.multiple_of(b * CMP_ROWS, CMP_ROWS)

    def scores_pass(b, mx):
        r0 = far_rows(b)
        kb = kc_ref[pl.ds(CMP_PAD + r0, CMP_ROWS), :].astype(BF16)
        row = r0 + lax.broadcasted_iota(jnp.int32, (CMP_ROWS, rows), 0)
        s = jnp.where(row < n_first_near, _dot_nt(kb, q), NEG_INF)
        sc_ref[pl.ds(r0, CMP_ROWS), :] = s
        return jnp.maximum(mx, jnp.max(s, axis=0, keepdims=True))

    mx_far = lax.fori_loop(0, n_blocks, scores_pass, jnp.full((1, rows), NEG_INF, F32))
    start = pl.multiple_of(8 * i + 8 + CMP_PAD - CMP_NEAR, 8)
    kcn = kc_ref[pl.ds(start, CMP_NEAR), :].astype(BF16)
    vcn_t = vc_ref[pl.ds(start, CMP_NEAR), :].T.astype(BF16)
    d_near = tl_lane - CMP_STRIDE * (key_row - (CMP_NEAR - 8)) - (CMP_BLOCK - 1)
    near_ok = (d_near >= 0) & (n_first_near + key_row >= 0)
    s_near = jnp.where(near_ok, _dot_nt(kcn, q) + bc_ref[...], NEG_INF)
    mx = jnp.maximum(mx_far, jnp.max(s_near, axis=0, keepdims=True))
    e_near = jnp.exp2(s_near - mx)

    def exp_pass(b, den):
        r0 = far_rows(b)
        e = jnp.exp2(sc_ref[pl.ds(r0, CMP_ROWS), :] - mx)
        sc_ref[pl.ds(r0, CMP_ROWS), :] = e
        return den + jnp.sum(e, axis=0, keepdims=True)

    den = lax.fori_loop(0, n_blocks, exp_pass, jnp.sum(e_near, axis=0, keepdims=True))
    inv = jnp.where(mx > 0.5 * NEG_INF, 1.0 / den, 0.0)
    p_near = e_near * inv
    sb = lax.broadcasted_iota(jnp.int32, (SEL_COLS, CMP_NEAR), 0)
    nn = n_first_near + lax.broadcasted_iota(jnp.int32, (SEL_COLS, CMP_NEAR), 1)
    ovl_near = jnp.where((nn >= 4 * sb - 1) & (nn <= 4 * sb + 3) & (nn >= 0), 1.0, 0.0).astype(BF16)
    out_ref[...] = _dot(vcn_t, p_near.astype(BF16))
    imp_ref[...] = _dot_split_rhs(ovl_near, head_sum(p_near))

    def out_pass(b, carry):
        r0 = far_rows(b)
        p = sc_ref[pl.ds(r0, CMP_ROWS), :] * inv
        out_ref[...] += _dot(vct_ref[:, pl.ds(CMP_PAD + r0, CMP_ROWS)].astype(BF16), p.astype(BF16))
        imp_ref[...] += _dot_split_rhs(ovl_ref[:, pl.ds(r0, CMP_ROWS)], head_sum(p))
        return carry

    lax.fori_loop(0, n_blocks, out_pass, 0)
    out_ref[...] = gate(0) * out_ref[...]

    imp = imp_ref[...]
    blk = lax.broadcasted_iota(jnp.int32, (SEL_COLS, Q_BLOCK), 0)
    tl = lax.broadcasted_iota(jnp.int32, (SEL_COLS, Q_BLOCK), 1)
    cur = 2 * i + (tl >= SEL_BLOCK).astype(jnp.int32)
    forced = (blk == 0) | (blk == cur) | (blk == cur - 1)
    causal_blk = blk * SEL_BLOCK <= Q_BLOCK * i + tl
    work = jnp.where(forced, -1.0, jnp.where(causal_blk, imp, -1.0))
    blk_f = blk.astype(F32)
    sel = jnp.where(forced, 1.0, 0.0)
    for _ in range(SEL_TOPK - 3):
        best = jnp.max(work, axis=0, keepdims=True)
        first = jnp.min(jnp.where(work == best, blk_f, float(SEL_COLS)), axis=0, keepdims=True)
        pick = blk_f == first
        sel = jnp.where(pick, 1.0, sel)
        work = jnp.where(pick, -2.0, work)
    amask = jnp.where(sel > 0.0, 0.0, NEG_INF)
    for half in range(SEL_COLS // LANE):
        a_t = amask[half * LANE:(half + 1) * LANE, :].T.astype(BF16)
        qa_ref[half, :, 0:HEAD_DIM] = q
        qa_ref[half, :, HEAD_DIM:2 * HEAD_DIM] = jnp.concatenate([a_t] * NSA_HG, axis=0)

    def sel_chunk(tile, n_keys):
        off = pl.multiple_of(tile * Q_BLOCK, Q_BLOCK)
        qa = qa_ref[tile // (LANE // 2)]
        return _dot_nt(ksa_ref[pl.ds(off, n_keys), :], qa), vst_ref[:, pl.ds(off, n_keys)]

    s_t, v_t = sel_chunk(i, Q_BLOCK)
    near = [(jnp.where(key_row <= tl_lane, s_t + bd_ref[Q_BLOCK:, :], NEG_INF), v_t)]
    s_t, v_t = sel_chunk(jnp.maximum(i - 1, 0), Q_BLOCK)
    near.append((jnp.where(i >= 1, s_t + bd_ref[:Q_BLOCK, :], NEG_INF), v_t))
    m_ref[...], acc_ref[...] = _softmax_tiles(near)
    n_far = jnp.maximum(i - 1, 0)
    n_chunks = n_far // FAR_TILES
    chunk_keys = FAR_TILES * Q_BLOCK

    def far_scores(c):
        tile = FAR_TILES * jnp.minimum(c, jnp.maximum(n_chunks - 1, 0))
        off = pl.multiple_of(tile * Q_BLOCK, chunk_keys)
        return _dot_nt(ksa_ref[pl.ds(off, chunk_keys), :], qa_ref[tile // (LANE // 2)])

    def far_consume(c, s_t):
        off = pl.multiple_of(c * chunk_keys, chunk_keys)
        _flash_update(s_t, vst_ref[:, pl.ds(off, chunk_keys)], m_ref, acc_ref)

    s_ref[0] = far_scores(0)
    _pipelined_chunks(n_chunks, far_scores, far_consume, s_ref)

    @pl.when(n_chunks % 2 == 1)
    def _():
        far_consume(n_chunks - 1, s_ref[0])

    left = n_far % FAR_TILES

    @pl.when(left >= 2)
    def _():
        s_t, v_t = sel_chunk(n_far - left, 2 * Q_BLOCK)
        _flash_update(s_t, v_t, m_ref, acc_ref)

    @pl.when(left % 2 == 1)
    def _():
        s_t, v_t = sel_chunk(n_far - 1, Q_BLOCK)
        _flash_update(s_t, v_t, m_ref, acc_ref)

    out_ref[...] += gate(1) * _normalized(acc_ref[...])

    def win_chunk(tile):
        off = pl.multiple_of(jnp.maximum(tile, 0) * Q_BLOCK, Q_BLOCK)
        return _dot_nt(kw_ref[pl.ds(off, Q_BLOCK), :], q), vwt_ref[:, pl.ds(off, Q_BLOCK)]

    s_t, v_t = win_chunk(i)
    win = [(jnp.where(key_row <= tl_lane, s_t + bd_ref[Q_BLOCK:, :], NEG_INF), v_t)]
    s_t, v_t = win_chunk(i - 1)
    win.append((jnp.where(i >= 1, s_t + bd_ref[:Q_BLOCK, :], NEG_INF), v_t))
    for back in (2, 3):
        s_t, v_t = win_chunk(i - back)
        win.append((jnp.where(i >= back, s_t, NEG_INF), v_t))
    s_t, v_t = win_chunk(i - 4)
    win.append((jnp.where((key_row > tl_lane) & (i >= 4), s_t, NEG_INF), v_t))
    _, acc_w = _softmax_tiles(win)
    o = out_ref[...] + gate(2) * _normalized(acc_w)
    for hg in range(NSA_HG):
        o_ref[:, hg * HEAD_DIM:(hg + 1) * HEAD_DIM] = o[:, hg * Q_BLOCK:(hg + 1) * Q_BLOCK].T.astype(o_ref.dtype)


def _nsa_attention(qkh, gates, kcv, kcv_t, ksa, vt, bd, bc, ovl):
    t = qkh.shape[1]
    ncp = t // CMP_STRIDE
    assert ncp % CMP_ROWS == 0
    resident = functools.partial(pl.BlockSpec, pipeline_mode=pl.Buffered(1))
    return pl.pallas_call(
        _nsa_kernel,
        grid=(NSA_GROUPS, t // Q_BLOCK),
        in_specs=[pl.BlockSpec((NSA_HG, Q_BLOCK, HEAD_DIM), lambda g, i: (g, i, 0)),
                  pl.BlockSpec((None, Q_BLOCK, LANE), lambda g, i: (g, i, 0)),
                  resident((None, CMP_PAD + ncp, HEAD_DIM), lambda g, i: (g, 0, 0)),
                  resident((None, CMP_PAD + ncp, HEAD_DIM), lambda g, i: (NSA_GROUPS + g, 0, 0)),
                  resident((None, HEAD_DIM, CMP_PAD + ncp), lambda g, i: (NSA_GROUPS + g, 0, 0)),
                  resident((None, t, 2 * HEAD_DIM), lambda g, i: (g, 0, 0)),
                  resident((None, V_ROWS, t), lambda g, i: (g, 0, 0)),
                  resident((None, t, HEAD_DIM), lambda g, i: (N_HEADS + 3 * NSA_GROUPS + g, 0, 0)),
                  resident((None, V_ROWS, t), lambda g, i: (NSA_GROUPS + g, 0, 0)),
                  resident((2 * Q_BLOCK, NSA_ROWS), lambda g, i: (0, g)),
                  resident((CMP_NEAR, NSA_ROWS), lambda g, i: (0, g)),
                  resident((SEL_COLS, ncp), lambda g, i: (0, 0))],
        out_specs=pl.BlockSpec((Q_BLOCK, NSA_HG * HEAD_DIM), lambda g, i: (i, g)),
        out_shape=jax.ShapeDtypeStruct((t, D_MODEL), BF16),
        scratch_shapes=[pltpu.VMEM((SEL_COLS // LANE, NSA_ROWS, 2 * HEAD_DIM), BF16),
                        pltpu.VMEM((2, FAR_TILES * Q_BLOCK, NSA_ROWS), F32),
                        pltpu.VMEM((ncp, NSA_ROWS), F32), pltpu.VMEM((SEL_COLS, Q_BLOCK), F32),
                        pltpu.VMEM((1, NSA_ROWS), F32),
                        pltpu.VMEM((V_ROWS, NSA_ROWS), F32), pltpu.VMEM((HEAD_DIM, NSA_ROWS), F32)],
        compiler_params=_cparams(("parallel", "arbitrary")),
        name="nsa_attention",
    )(qkh, gates, kcv, kcv, kcv_t, ksa, vt, qkh, vt, bd, bc, ovl)


def _nsa_layer(h, g_norm, w_in, w_o, layer, cmp_pos, cmp_w1, cmp_w2, bd, bc):
    t = h.shape[0]
    assert t // SEL_BLOCK <= SEL_COLS
    a = _rmsnorm(h, g_norm, BF16)
    tn = NSA_KV_DIM
    q_blocks = D_MODEL // tn
    qkh = _proj(a, w_in, layer, n_out=D_MODEL + 4 * NSA_KV_DIM, out_dtype=BF16, head_major=True, tn=tn,
                n_scaled=D_MODEL,
                col_block=lambda j: j + jnp.where(j >= q_blocks + 3, 1, 0))
    vt = _proj_t(a, w_in, layer, n_out=2 * NSA_KV_DIM, tn=tn, col_block=lambda j: q_blocks + 3 + 2 * j)
    w_g = w_in[layer][:, D_MODEL + 6 * NSA_KV_DIM:].reshape(D_MODEL, NSA_GROUPS, 3 * NSA_HG)
    w_g = jnp.pad(w_g, ((0, 0), (0, 0), (0, LANE - 3 * NSA_HG))).reshape(1, D_MODEL, NSA_GROUPS * LANE)
    gates = _proj(a, w_g, 0, n_out=NSA_GROUPS * LANE, out_dtype=F32, head_major=True,
                  tn=NSA_GROUPS * LANE)
    kcv = _compress(qkh[N_HEADS:N_HEADS + 2 * NSA_GROUPS], cmp_pos, cmp_w1, cmp_w2)
    kcv_t = jnp.swapaxes(kcv, 1, 2)
    key_blk = (jnp.arange(t) // SEL_BLOCK) % LANE
    onehot = (key_blk[:, None] == jnp.arange(LANE)[None, :]).astype(BF16)
    ksa = jnp.concatenate([qkh[N_HEADS + 2 * NSA_GROUPS:N_HEADS + 3 * NSA_GROUPS],
                           jnp.broadcast_to(onehot, (NSA_GROUPS, t, LANE))], axis=-1)
    n = jnp.arange(t // CMP_STRIDE)[None, :]
    sblk = jnp.arange(SEL_COLS)[:, None]
    ovl = ((n >= 4 * sblk - 1) & (n <= 4 * sblk + 3)).astype(BF16)
    o = _nsa_attention(qkh, gates, kcv, kcv_t, ksa, vt, bd, bc, ovl)
    return _mm_res(o, w_o, layer, h, tm=1024, tn=512)


def kernel(x, norm_mix, norm_ffn, norm_final, rel_table, nsa_w_in, nsa_w_o, nsa_cmp_pos, nsa_cmp_w1,
           nsa_cmp_w2, fox_w_in, fox_b_f, fox_w_o, ffn_w_gate, ffn_w_up, ffn_w_down):
    b, t, d = x.shape
    depth = norm_mix.shape[0]
    bd, bc = _nsa_bias_templates(rel_table.astype(F32))
    nsa_w_in, fox_w_in = nsa_w_in.astype(F32), fox_w_in.astype(F32)
    ffn_w_gate, ffn_w_up = ffn_w_gate.astype(F32), ffn_w_up.astype(F32)
    nsa_w_o, fox_w_o, ffn_w_down = nsa_w_o.astype(BF16), fox_w_o.astype(BF16), ffn_w_down.astype(BF16)
    outs = []
    for bi in range(b):
        h = x[bi]
        for i in range(depth):
            j = i // 2
            if i % 2 == 0:
                h = _nsa_layer(h, norm_mix[i], nsa_w_in, nsa_w_o, j, nsa_cmp_pos[j], nsa_cmp_w1[j],
                               nsa_cmp_w2[j], bd, bc)
            else:
                h = _fox_layer(h, norm_mix[i], fox_w_in, fox_b_f[j], fox_w_o, j)
            h = _ffn(h, norm_ffn[i], ffn_w_gate, ffn_w_up, ffn_w_down, i)
        outs.append(_rmsnorm(h, norm_final, F32))
    return jnp.stack(outs, axis=0)
```

```python
import functools
import math

import jax
import jax.numpy as jnp
from jax import lax
from jax.experimental import pallas as pl
from jax.experimental.pallas import tpu as pltpu

F32 = jnp.float32
BF16 = jnp.bfloat16

D_MODEL = 4096
HEAD_DIM = 128
N_HEADS = D_MODEL // HEAD_DIM
NSA_GROUPS = 4
NSA_HG = N_HEADS // NSA_GROUPS
NSA_KV_DIM = NSA_GROUPS * HEAD_DIM
CMP_BLOCK = 32
CMP_STRIDE = 16
SEL_BLOCK = 64
SEL_TOPK = 16
WINDOW = 512
REL_BUCKETS = 32
REL_MAX_DIST = 128
Q_BLOCK = 128
RMS_EPS = 1e-6
NEG_INF = -1e30
LOG2E = 1.4426950408889634
Q_SCALE = HEAD_DIM ** -0.5 * LOG2E

LANE = 128
SEL_COLS = 256
CMP_PAD = 128
CMP_NEAR = 128
CMP_ROWS = 256
FAR_TILES = 4
NSA_ROWS = NSA_HG * Q_BLOCK
VMEM_LIMIT = 56 * 1024 * 1024


def _cparams(sem):
    return pltpu.CompilerParams(dimension_semantics=sem, vmem_limit_bytes=VMEM_LIMIT)


def _dot(a, b):
    return jnp.dot(a, b, preferred_element_type=F32)


def _dot_nt(a, b):
    return lax.dot_general(a, b, (((1,), (1,)), ((), ())), preferred_element_type=F32)


def _rmsnorm_kernel(x_ref, g_ref, o_ref):
    x = x_ref[...]
    ms = jnp.mean(x * x, axis=-1, keepdims=True)
    o_ref[...] = (x * lax.rsqrt(ms + RMS_EPS) * g_ref[...]).astype(o_ref.dtype)


def _rmsnorm(x, g, out_dtype):
    t, d = x.shape
    tm = min(256, t)
    return pl.pallas_call(
        _rmsnorm_kernel,
        grid=(t // tm,),
        in_specs=[pl.BlockSpec((tm, d), lambda i: (i, 0)),
                  pl.BlockSpec((1, d), lambda i: (0, 0))],
        out_specs=pl.BlockSpec((tm, d), lambda i: (i, 0)),
        out_shape=jax.ShapeDtypeStruct((t, d), out_dtype),
        compiler_params=_cparams(("parallel",)),
        name="rmsnorm",
    )(x, g.reshape(1, d))


def _proj_kernel(x_ref, w_ref, o_ref, *, head_major, n_scaled_blocks):
    r = _dot(x_ref[...], w_ref[...].astype(BF16))
    if n_scaled_blocks:
        r = r * jnp.where(pl.program_id(1) < n_scaled_blocks, Q_SCALE, 1.0)
    if head_major:
        for s in range(o_ref.shape[0]):
            o_ref[s] = r[:, s * LANE:(s + 1) * LANE].astype(o_ref.dtype)
    else:
        o_ref[...] = r.astype(o_ref.dtype)


def _identity(j):
    return j


def _proj(x, w, layer, *, n_out, out_dtype, head_major, tn, col_block=_identity, n_scaled=0):
    t, k = x.shape
    n = n_out
    tm = min(1024, t)
    tn = min(tn, n)
    assert n_scaled % tn == 0 and n % tn == 0
    if head_major:
        out_shape = jax.ShapeDtypeStruct((n // LANE, t, LANE), out_dtype)
        out_spec = pl.BlockSpec((tn // LANE, tm, LANE), lambda i, j: (j, i, 0))
    else:
        out_shape = jax.ShapeDtypeStruct((t, n), out_dtype)
        out_spec = pl.BlockSpec((tm, tn), lambda i, j: (i, j))
    return pl.pallas_call(
        functools.partial(_proj_kernel, head_major=head_major, n_scaled_blocks=n_scaled // tn),
        grid=(t // tm, n // tn),
        in_specs=[pl.BlockSpec((tm, k), lambda i, j: (i, 0)),
                  pl.BlockSpec((None, k, tn), lambda i, j: (layer, 0, col_block(j)))],
        out_specs=out_spec,
        out_shape=out_shape,
        compiler_params=_cparams(("parallel", "parallel")),
        name="proj",
    )(x, w)


def _proj_t_kernel(x_ref, w_ref, o_ref):
    r = _dot(x_ref[...], w_ref[...].astype(BF16))
    ones = _ones_rows(r.shape[0], o_ref.dtype)
    for s in range(o_ref.shape[0]):
        o_ref[s, 0:HEAD_DIM, :] = r[:, s * LANE:(s + 1) * LANE].T.astype(o_ref.dtype)
        o_ref[s, HEAD_DIM:, :] = ones


def _proj_t(x, w, layer, *, n_out, tn, col_block=_identity):
    t, k = x.shape
    n = n_out
    tm = min(1024, t)
    tn = min(tn, n)
    return pl.pallas_call(
        _proj_t_kernel,
        grid=(t // tm, n // tn),
        in_specs=[pl.BlockSpec((tm, k), lambda i, j: (i, 0)),
                  pl.BlockSpec((None, k, tn), lambda i, j: (layer, 0, col_block(j)))],
        out_specs=pl.BlockSpec((tn // LANE, V_ROWS, tm), lambda i, j: (j, 0, i)),
        out_shape=jax.ShapeDtypeStruct((n // LANE, V_ROWS, t), BF16),
        compiler_params=_cparams(("parallel", "parallel")),
        name="proj_t",
    )(x, w)


def _mm_res_kernel(x_ref, w_ref, r_ref, o_ref):
    o_ref[...] = r_ref[...] + _dot(x_ref[...], w_ref[...])


def _mm_res(x, w, layer, res, *, tm, tn):
    t, k = x.shape
    n = w.shape[2]
    tm = min(tm, t)
    return pl.pallas_call(
        _mm_res_kernel,
        grid=(t // tm, n // tn),
        in_specs=[pl.BlockSpec((tm, k), lambda i, j: (i, 0)),
                  pl.BlockSpec((None, k, tn), lambda i, j: (layer, 0, j)),
                  pl.BlockSpec((tm, tn), lambda i, j: (i, j))],
        out_specs=pl.BlockSpec((tm, tn), lambda i, j: (i, j)),
        out_shape=jax.ShapeDtypeStruct((t, n), F32),
        compiler_params=_cparams(("parallel", "parallel")),
        name="mm_res",
    )(x, w, res)


def _gateup_kernel(x_ref, wg_ref, wu_ref, o_ref):
    x = x_ref[...]
    g = _dot(x, wg_ref[...].astype(BF16))
    u = _dot(x, wu_ref[...].astype(BF16))
    o_ref[...] = (g * jax.nn.sigmoid(g) * u).astype(o_ref.dtype)


def _gateup(x, wg, wu, layer, *, tn):
    t, k = x.shape
    n = wg.shape[2]
    tm = min(1024, t)
    return pl.pallas_call(
        _gateup_kernel,
        grid=(t // tm, n // tn),
        in_specs=[pl.BlockSpec((tm, k), lambda i, j: (i, 0)),
                  pl.BlockSpec((None, k, tn), lambda i, j: (layer, 0, j)),
                  pl.BlockSpec((None, k, tn), lambda i, j: (layer, 0, j))],
        out_specs=pl.BlockSpec((tm, tn), lambda i, j: (i, j)),
        out_shape=jax.ShapeDtypeStruct((t, n), BF16),
        compiler_params=_cparams(("parallel", "parallel")),
        name="gateup",
    )(x, wg, wu)


def _ffn(h, g_norm, w_gate, w_up, w_down, layer):
    a = _rmsnorm(h, g_norm, BF16)
    h1 = _gateup(a, w_gate, w_up, layer, tn=256)
    return _mm_res(h1, w_down, layer, h, tm=512, tn=256)


V_ROWS = HEAD_DIM + 16


def _ones_rows(n_cols, dtype):
    row = lax.broadcasted_iota(jnp.int32, (V_ROWS - HEAD_DIM, n_cols), 0)
    return jnp.where(row == 0, 1.0, 0.0).astype(dtype)


def _flash_init(m_ref, acc_ref):
    m_ref[...] = jnp.full(m_ref.shape, NEG_INF, F32)
    acc_ref[...] = jnp.zeros(acc_ref.shape, F32)


def _flash_update(s_t, v_t, m_ref, acc_ref):
    m_prev = m_ref[...]
    m_new = jnp.maximum(m_prev, jnp.max(s_t, axis=0, keepdims=True))
    alpha = jnp.exp2(m_prev - m_new)
    p = jnp.exp2(s_t - m_new)
    acc_ref[...] = alpha * acc_ref[...] + _dot(v_t, p.astype(BF16))
    m_ref[...] = m_new


def _softmax_tiles(tiles):
    mx = functools.reduce(jnp.maximum, [jnp.max(s_t, axis=0, keepdims=True) for s_t, _ in tiles])
    acc = None
    for s_t, v_t in tiles:
        a = _dot(v_t, jnp.exp2(s_t - mx).astype(BF16))
        acc = a if acc is None else acc + a
    return mx, acc


def _normalized(acc):
    return acc[0:HEAD_DIM] / acc[HEAD_DIM:HEAD_DIM + 1]


def _split3(x):
    hi = x.astype(BF16)
    r = x - hi.astype(F32)
    mid = r.astype(BF16)
    lo = (r - mid.astype(F32)).astype(BF16)
    return hi, mid, lo


N_PIECES = 3


def _cum_kernel(f_ref, b_ref, tri_ref, place_ref, qx_ref, kx_ref, carry_ref):
    @pl.when(pl.program_id(0) == 0)
    def _():
        carry_ref[...] = jnp.zeros_like(carry_ref)

    head_lane = lax.broadcasted_iota(jnp.int32, f_ref.shape, 1) < N_HEADS
    x = jnp.where(head_lane, f_ref[...] + b_ref[...], 0.0)
    log_f = jnp.minimum(x, 0.0) - jnp.log1p(jnp.exp(-jnp.abs(x)))
    c = jnp.dot(tri_ref[...], log_f, precision=lax.Precision.HIGHEST,
                preferred_element_type=F32) + carry_ref[...]
    carry_ref[...] = c[c.shape[0] - 1:, :]
    pieces = jnp.concatenate(_split3(c * LOG2E), axis=1)
    lane = lax.broadcasted_iota(jnp.int32, (c.shape[0], LANE), 1)
    q_const = jnp.where((lane >= N_PIECES) & (lane < 2 * N_PIECES), -1.0, 0.0)
    k_const = jnp.where(lane < N_PIECES, 1.0, 0.0)
    for h in range(N_HEADS):
        placed = _dot(pieces, place_ref[h])
        qx_ref[h] = (placed[:, :LANE] + q_const).astype(BF16)
        kx_ref[h] = (placed[:, LANE:] + k_const).astype(BF16)


def _forget_extras(f, b):
    t, n = f.shape
    tb = min(512, t)
    tri = jnp.tril(jnp.ones((tb, tb), F32))
    h = jnp.arange(N_HEADS)[:, None, None]
    row = jnp.arange(N_PIECES * LANE)[None, :, None]
    col = jnp.arange(2 * LANE)[None, None, :]
    piece, head = row // LANE, row % LANE
    place = ((head == h) & ((col == piece) | (col == LANE + N_PIECES + piece))).astype(BF16)
    out = jax.ShapeDtypeStruct((N_HEADS, t, LANE), BF16)
    return pl.pallas_call(
        _cum_kernel,
        grid=(t // tb,),
        in_specs=[pl.BlockSpec((tb, n), lambda i: (i, 0)),
                  pl.BlockSpec((1, n), lambda i: (0, 0)),
                  pl.BlockSpec((tb, tb), lambda i: (0, 0)),
                  pl.BlockSpec((N_HEADS, N_PIECES * LANE, 2 * LANE), lambda i: (0, 0, 0))],
        out_specs=[pl.BlockSpec((N_HEADS, tb, LANE), lambda i: (0, i, 0)),
                   pl.BlockSpec((N_HEADS, tb, LANE), lambda i: (0, i, 0))],
        out_shape=[out, out],
        scratch_shapes=[pltpu.VMEM((1, n), F32)],
        compiler_params=_cparams(("arbitrary",)),
        name="forget_extras",
    )(f, b, tri, place)


def _pipelined_loop(first, trips, per_trip, scores, consume, s_ref):
    def body(t, carry):
        c = first + t * per_trip
        for u in range(per_trip):
            s_next = scores(c + u + 1)
            consume(c + u, s_ref[u % 2])
            s_ref[(u + 1) % 2] = s_next
        return carry

    lax.fori_loop(0, trips, body, 0)


def _pipelined_chunks(n_chunks, scores, consume, s_ref):
    quads = n_chunks // 4
    _pipelined_loop(0, quads, 4, scores, consume, s_ref)
    _pipelined_loop(4 * quads, (n_chunks % 4) // 2, 2, scores, consume, s_ref)


def _fox_kernel(q_ref, qx_ref, k_ref, kx_ref, vt_ref, o_ref, ka_ref, qa_ref, s_ref, m_ref, acc_ref, *, tq, tk):
    i = pl.program_id(1)
    assert tq == 2 * tk

    @pl.when(i == 0)
    def _():
        ka_ref[:, 0:HEAD_DIM] = k_ref[...]
        ka_ref[:, HEAD_DIM:] = kx_ref[...]

    qa_ref[:, 0:HEAD_DIM] = q_ref[...]
    qa_ref[:, HEAD_DIM:] = qx_ref[...]
    _flash_init(m_ref, acc_ref)

    def scores(c):
        off = pl.multiple_of(c * tk, tk)
        return _dot_nt(ka_ref[pl.ds(off, tk), :], qa_ref[...])

    def consume(c, s_t):
        off = pl.multiple_of(c * tk, tk)
        _flash_update(s_t, vt_ref[:, pl.ds(off, tk)], m_ref, acc_ref)

    n_full = 2 * i
    s_ref[0] = scores(0)
    _pipelined_chunks(n_full, scores, consume, s_ref)
    key = lax.broadcasted_iota(jnp.int32, (tk, tq), 0)
    qry = lax.broadcasted_iota(jnp.int32, (tk, tq), 1)
    s_last = scores(n_full + 1)
    consume(n_full, jnp.where(key <= qry, s_ref[0], NEG_INF))
    consume(n_full + 1, jnp.where(key + tk <= qry, s_last, NEG_INF))
    o_ref[...] = _normalized(acc_ref[...]).T.astype(o_ref.dtype)


def _fox_attention(qkh, qx, kx, vt):
    t = qkh.shape[1]
    tq = min(1024, t)
    tk = tq // 2
    per_head = functools.partial(pl.BlockSpec, pipeline_mode=pl.Buffered(1))
    return pl.pallas_call(
        functools.partial(_fox_kernel, tq=tq, tk=tk),
        grid=(N_HEADS, t // tq),
        in_specs=[pl.BlockSpec((None, tq, HEAD_DIM), lambda h, i: (h, i, 0)),
                  pl.BlockSpec((None, tq, LANE), lambda h, i: (h, i, 0)),
                  per_head((None, t, HEAD_DIM), lambda h, i: (N_HEADS + h, 0, 0)),
                  per_head((None, t, LANE), lambda h, i: (h, 0, 0)),
                  per_head((None, V_ROWS, t), lambda h, i: (h, 0, 0))],
        out_specs=pl.BlockSpec((tq, HEAD_DIM), lambda h, i: (i, h)),
        out_shape=jax.ShapeDtypeStruct((t, D_MODEL), BF16),
        scratch_shapes=[pltpu.VMEM((t, 2 * HEAD_DIM), BF16),
                        pltpu.VMEM((tq, 2 * HEAD_DIM), BF16), pltpu.VMEM((2, tk, tq), F32),
                        pltpu.VMEM((1, tq), F32), pltpu.VMEM((V_ROWS, tq), F32)],
        compiler_params=_cparams(("parallel", "arbitrary")),
        name="fox_attention",
    )(qkh, qx, qkh, kx, vt)


def _fox_layer(h, g_norm, w_in, b_f, w_o, layer):
    a = _rmsnorm(h, g_norm, BF16)
    tn = 512
    qkh = _proj(a, w_in, layer, n_out=2 * D_MODEL, out_dtype=BF16, head_major=True, tn=tn,
                n_scaled=D_MODEL)
    vt = _proj_t(a, w_in, layer, n_out=D_MODEL, tn=tn, col_block=lambda j: 2 * D_MODEL // tn + j)
    f = _proj(a, w_in, layer, n_out=LANE, out_dtype=F32, head_major=False, tn=LANE,
              col_block=lambda j: 3 * D_MODEL // LANE + j)
    b = jnp.pad(b_f.astype(F32), (0, LANE - N_HEADS)).reshape(1, LANE)
    qx, kx = _forget_extras(f, b)
    o = _fox_attention(qkh, qx, kx, vt)
    return _mm_res(o, w_o, layer, h, tm=1024, tn=512)


def _rel_bucket_const(dist):
    n = jnp.maximum(dist, 0)
    max_exact = REL_BUCKETS // 2
    nf = jnp.maximum(n, max_exact).astype(F32)
    large = max_exact + (jnp.log(nf / max_exact) / math.log(REL_MAX_DIST / max_exact)
                         * (REL_BUCKETS - max_exact)).astype(jnp.int32)
    large = jnp.minimum(large, REL_BUCKETS - 1)
    return jnp.where(n < max_exact, n, large)


def _bias_kernel(tab_ref, bkt_ref, o_ref):
    h = pl.program_id(0)
    bkt = bkt_ref[...]
    acc = jnp.zeros(bkt.shape, F32)
    for b in range(REL_BUCKETS):
        acc = jnp.where(bkt == b, tab_ref[b, h], acc)
    o_ref[...] = (acc - tab_ref[REL_BUCKETS - 1, h]) * LOG2E


def _bias_template(rel_table, bkt):
    r, c = bkt.shape
    return pl.pallas_call(
        _bias_kernel,
        grid=(N_HEADS,),
        in_specs=[pl.BlockSpec(memory_space=pltpu.SMEM),
                  pl.BlockSpec((r, c), lambda h: (0, 0))],
        out_specs=pl.BlockSpec((r, c), lambda h: (0, h)),
        out_shape=jax.ShapeDtypeStruct((r, N_HEADS * c), F32),
        compiler_params=_cparams(("arbitrary",)),
        name="bias_template",
    )(rel_table, bkt)


def _nsa_bias_templates(rel_table):
    tl = jnp.arange(Q_BLOCK)[None, :]
    bd = _bias_template(rel_table, _rel_bucket_const(tl + Q_BLOCK - jnp.arange(2 * Q_BLOCK)[:, None]))
    m = jnp.arange(CMP_NEAR)[:, None]
    bc = _bias_template(rel_table, _rel_bucket_const(
        tl - CMP_STRIDE * (m - (CMP_NEAR - 8)) - (CMP_BLOCK - 1)))
    return bd, bc


def _gelu_tanh(x):
    return 0.5 * x * (1.0 + jnp.tanh(math.sqrt(2.0 / math.pi) * (x + 0.044715 * (x * x * x))))


def _cmp_kernel(kb_ref, pos_ref, w1_ref, w2_ref, o_ref, *, nb):
    half = CMP_STRIDE * HEAD_DIM
    kb = kb_ref[...].astype(F32)
    xa = (kb + pos_ref[0:1, :]).astype(BF16)
    xb = (kb + pos_ref[1:2, :]).astype(BF16)
    a = _dot(xa, w1_ref[0:half, :])
    b = _dot(xb, w1_ref[half:2 * half, :])
    pre = a + pltpu.roll(b, nb - 1, axis=0)
    out = _dot(_gelu_tanh(pre).astype(BF16), w2_ref[...])
    row = lax.broadcasted_iota(jnp.int32, out.shape, 0)
    out = jnp.where(row < nb - 1, out, 0.0)
    o_ref[0:CMP_PAD, :] = jnp.zeros((CMP_PAD, HEAD_DIM), F32)
    o_ref[CMP_PAD:CMP_PAD + nb, :] = out


def _compress(kvh, cmp_pos, cmp_w1, cmp_w2):
    t = kvh.shape[1]
    nb = t // CMP_STRIDE
    kb = kvh[:2 * NSA_GROUPS].reshape(2 * NSA_GROUPS, nb, CMP_STRIDE * HEAD_DIM)
    pos = cmp_pos.astype(F32).reshape(2, 2, CMP_STRIDE * HEAD_DIM)
    return pl.pallas_call(
        functools.partial(_cmp_kernel, nb=nb),
        grid=(2 * NSA_GROUPS,),
        in_specs=[pl.BlockSpec((None, nb, CMP_STRIDE * HEAD_DIM), lambda j: (j, 0, 0)),
                  pl.BlockSpec((None, 2, CMP_STRIDE * HEAD_DIM), lambda j: (j // NSA_GROUPS, 0, 0)),
                  pl.BlockSpec((None, CMP_BLOCK * HEAD_DIM, HEAD_DIM), lambda j: (j // NSA_GROUPS, 0, 0)),
                  pl.BlockSpec((None, HEAD_DIM, HEAD_DIM), lambda j: (j // NSA_GROUPS, 0, 0))],
        out_specs=pl.BlockSpec((None, CMP_PAD + nb, HEAD_DIM), lambda j: (j, 0, 0)),
        out_shape=jax.ShapeDtypeStruct((2 * NSA_GROUPS, CMP_PAD + nb, HEAD_DIM), F32),
        compiler_params=_cparams(("parallel",)),
        name="compress",
    )(kb, pos, cmp_w1.astype(BF16), cmp_w2.astype(BF16))


def _dot_split_rhs(w, x):
    hi, mid, lo = _split3(x)
    return _dot(w, hi) + _dot(w, mid) + _dot(w, lo)


def _nsa_kernel(q_ref, g_ref, kc_ref, vc_ref, vct_ref, ksa_ref, vst_ref, kw_ref, vwt_ref,
                bd_ref, bc_ref, ovl_ref, o_ref,
                qa_ref, s_ref, sc_ref, imp_ref, gate_ref, m_ref, acc_ref, out_ref):
    i = pl.program_id(1)
    rows = NSA_ROWS
    q = q_ref[...].reshape(rows, HEAD_DIM)
    tl_lane = lax.broadcasted_iota(jnp.int32, (Q_BLOCK, rows), 1) % Q_BLOCK
    key_row = lax.broadcasted_iota(jnp.int32, (Q_BLOCK, rows), 0)
    gate_ref[...] = jax.nn.sigmoid(g_ref[...]).T
    gate_row0 = 3 * NSA_HG * pl.program_id(0)

    def gate(branch):
        return jnp.concatenate([gate_ref[pl.ds(gate_row0 + 3 * hg + branch, 1), :] for hg in range(NSA_HG)], axis=1)

    def head_sum(p):
        acc = p[:, 0:Q_BLOCK]
        for hg in range(1, NSA_HG):
            acc = acc + p[:, hg * Q_BLOCK:(hg + 1) * Q_BLOCK]
        return acc

    n_first_near = 8 * i - (CMP_NEAR - 8)
    n_blocks = (jnp.maximum(n_first_near, 0) + CMP_ROWS - 1) // CMP_ROWS

    def far_rows(b):
        return pl.multiple_of(b * CMP_ROWS, CMP_ROWS)

    def scores_pass(b, mx):
        r0 = far_rows(b)
        kb = kc_ref[pl.ds(CMP_PAD + r0, CMP_ROWS), :].astype(BF16)
        row = r0 + lax.broadcasted_iota(jnp.int32, (CMP_ROWS, rows), 0)
        s = jnp.where(row < n_first_near, _dot_nt(kb, q), NEG_INF)
        sc_ref[pl.ds(r0, CMP_ROWS), :] = s
        return jnp.maximum(mx, jnp.max(s, axis=0, keepdims=True))

    mx_far = lax.fori_loop(0, n_blocks, scores_pass, jnp.full((1, rows), NEG_INF, F32))
    start = pl.multiple_of(8 * i + 8 + CMP_PAD - CMP_NEAR, 8)
    kcn = kc_ref[pl.ds(start, CMP_NEAR), :].astype(BF16)
    vcn_t = vc_ref[pl.ds(start, CMP_NEAR), :].T.astype(BF16)
    d_near = tl_lane - CMP_STRIDE * (key_row - (CMP_NEAR - 8)) - (CMP_BLOCK - 1)
    near_ok = (d_near >= 0) & (n_first_near + key_row >= 0)
    s_near = jnp.where(near_ok, _dot_nt(kcn, q) + bc_ref[...], NEG_INF)
    mx = jnp.maximum(mx_far, jnp.max(s_near, axis=0, keepdims=True))
    e_near = jnp.exp2(s_near - mx)

    def exp_pass(b, den):
        r0 = far_rows(b)
        e = jnp.exp2(sc_ref[pl.ds(r0, CMP_ROWS), :] - mx)
        sc_ref[pl.ds(r0, CMP_ROWS), :] = e
        return den + jnp.sum(e, axis=0, keepdims=True)

    den = lax.fori_loop(0, n_blocks, exp_pass, jnp.sum(e_near, axis=0, keepdims=True))
    inv = jnp.where(mx > 0.5 * NEG_INF, 1.0 / den, 0.0)
    p_near = e_near * inv
    sb = lax.broadcasted_iota(jnp.int32, (SEL_COLS, CMP_NEAR), 0)
    nn = n_first_near + lax.broadcasted_iota(jnp.int32, (SEL_COLS, CMP_NEAR), 1)
    ovl_near = jnp.where((nn >= 4 * sb - 1) & (nn <= 4 * sb + 3) & (nn >= 0), 1.0, 0.0).astype(BF16)
    out_ref[...] = _dot(vcn_t, p_near.astype(BF16))
    imp_ref[...] = _dot_split_rhs(ovl_near, head_sum(p_near))

    def out_pass(b, carry):
        r0 = far_rows(b)
        p = sc_ref[pl.ds(r0, CMP_ROWS), :] * inv
        out_ref[...] += _dot(vct_ref[:, pl.ds(CMP_PAD + r0, CMP_ROWS)].astype(BF16), p.astype(BF16))
        imp_ref[...] += _dot_split_rhs(ovl_ref[:, pl.ds(r0, CMP_ROWS)], head_sum(p))
        return carry

    lax.fori_loop(0, n_blocks, out_pass, 0)
    out_ref[...] = gate(0) * out_ref[...]

    imp = imp_ref[...]
    blk = lax.broadcasted_iota(jnp.int32, (SEL_COLS, Q_BLOCK), 0)
    tl = lax.broadcasted_iota(jnp.int32, (SEL_COLS, Q_BLOCK), 1)
    cur = 2 * i + (tl >= SEL_BLOCK).astype(jnp.int32)
    forced = (blk == 0) | (blk == cur) | (blk == cur - 1)
    causal_blk = blk * SEL_BLOCK <= Q_BLOCK * i + tl
    work = jnp.where(forced, -1.0, jnp.where(causal_blk, imp, -1.0))
    blk_f = blk.astype(F32)
    sel = jnp.where(forced, 1.0, 0.0)
    for _ in range(SEL_TOPK - 3):
        best = jnp.max(work, axis=0, keepdims=True)
        first = jnp.min(jnp.where(work == best, blk_f, float(SEL_COLS)), axis=0, keepdims=True)
        pick = blk_f == first
        sel = jnp.where(pick, 1.0, sel)
        work = jnp.where(pick, -2.0, work)
    amask = jnp.where(sel > 0.0, 0.0, NEG_INF)
    for half in range(SEL_COLS // LANE):
        a_t = amask[half * LANE:(half + 1) * LANE, :].T.astype(BF16)
        qa_ref[half, :, 0:HEAD_DIM] = q
        qa_ref[half, :, HEAD_DIM:2 * HEAD_DIM] = jnp.concatenate([a_t] * NSA_HG, axis=0)

    def sel_chunk(tile, n_keys):
        off = pl.multiple_of(tile * Q_BLOCK, Q_BLOCK)
        qa = qa_ref[tile // (LANE // 2)]
        return _dot_nt(ksa_ref[pl.ds(off, n_keys), :], qa), vst_ref[:, pl.ds(off, n_keys)]

    s_t, v_t = sel_chunk(i, Q_BLOCK)
    near = [(jnp.where(key_row <= tl_lane, s_t + bd_ref[Q_BLOCK:, :], NEG_INF), v_t)]
    s_t, v_t = sel_chunk(jnp.maximum(i - 1, 0), Q_BLOCK)
    near.append((jnp.where(i >= 1, s_t + bd_ref[:Q_BLOCK, :], NEG_INF), v_t))
    m_ref[...], acc_ref[...] = _softmax_tiles(near)
    n_far = jnp.maximum(i - 1, 0)
    n_chunks = n_far // FAR_TILES
    chunk_keys = FAR_TILES * Q_BLOCK

    def far_scores(c):
        tile = FAR_TILES * jnp.minimum(c, jnp.maximum(n_chunks - 1, 0))
        off = pl.multiple_of(tile * Q_BLOCK, chunk_keys)
        return _dot_nt(ksa_ref[pl.ds(off, chunk_keys), :], qa_ref[tile // (LANE // 2)])

    def far_consume(c, s_t):
        off = pl.multiple_of(c * chunk_keys, chunk_keys)
        _flash_update(s_t, vst_ref[:, pl.ds(off, chunk_keys)], m_ref, acc_ref)

    s_ref[0] = far_scores(0)
    _pipelined_chunks(n_chunks, far_scores, far_consume, s_ref)

    @pl.when(n_chunks % 2 == 1)
    def _():
        far_consume(n_chunks - 1, s_ref[0])

    left = n_far % FAR_TILES

    @pl.when(left >= 2)
    def _():
        s_t, v_t = sel_chunk(n_far - left, 2 * Q_BLOCK)
        _flash_update(s_t, v_t, m_ref, acc_ref)

    @pl.when(left % 2 == 1)
    def _():
        s_t, v_t = sel_chunk(n_far - 1, Q_BLOCK)
        _flash_update(s_t, v_t, m_ref, acc_ref)

    out_ref[...] += gate(1) * _normalized(acc_ref[...])

    def win_chunk(tile):
        off = pl.multiple_of(jnp.maximum(tile, 0) * Q_BLOCK, Q_BLOCK)
        return _dot_nt(kw_ref[pl.ds(off, Q_BLOCK), :], q), vwt_ref[:, pl.ds(off, Q_BLOCK)]

    s_t, v_t = win_chunk(i)
    win = [(jnp.where(key_row <= tl_lane, s_t + bd_ref[Q_BLOCK:, :], NEG_INF), v_t)]
    s_t, v_t = win_chunk(i - 1)
    win.append((jnp.where(i >= 1, s_t + bd_ref[:Q_BLOCK, :], NEG_INF), v_t))
    for back in (2, 3):
        s_t, v_t = win_chunk(i - back)
        win.append((jnp.where(i >= back, s_t, NEG_INF), v_t))
    s_t, v_t = win_chunk(i - 4)
    win.append((jnp.where((key_row > tl_lane) & (i >= 4), s_t, NEG_INF), v_t))
    _, acc_w = _softmax_tiles(win)
    o = out_ref[...] + gate(2) * _normalized(acc_w)
    for hg in range(NSA_HG):
        o_ref[:, hg * HEAD_DIM:(hg + 1) * HEAD_DIM] = o[:, hg * Q_BLOCK:(hg + 1) * Q_BLOCK].T.astype(o_ref.dtype)


def _nsa_attention(qkh, gates, kcv, kcv_t, ksa, vt, bd, bc, ovl):
    t = qkh.shape[1]
    ncp = t // CMP_STRIDE
    assert ncp % CMP_ROWS == 0
    resident = functools.partial(pl.BlockSpec, pipeline_mode=pl.Buffered(1))
    return pl.pallas_call(
        _nsa_kernel,
        grid=(NSA_GROUPS, t // Q_BLOCK),
        in_specs=[pl.BlockSpec((NSA_HG, Q_BLOCK, HEAD_DIM), lambda g, i: (g, i, 0)),
                  pl.BlockSpec((Q_BLOCK, LANE), lambda g, i: (i, 0)),
                  resident((None, CMP_PAD + ncp, HEAD_DIM), lambda g, i: (g, 0, 0)),
                  resident((None, CMP_PAD + ncp, HEAD_DIM), lambda g, i: (NSA_GROUPS + g, 0, 0)),
                  resident((None, HEAD_DIM, CMP_PAD + ncp), lambda g, i: (NSA_GROUPS + g, 0, 0)),
                  resident((None, t, 2 * HEAD_DIM), lambda g, i: (g, 0, 0)),
                  resident((None, V_ROWS, t), lambda g, i: (g, 0, 0)),
                  resident((None, t, HEAD_DIM), lambda g, i: (N_HEADS + 3 * NSA_GROUPS + g, 0, 0)),
                  resident((None, V_ROWS, t), lambda g, i: (NSA_GROUPS + g, 0, 0)),
                  resident((2 * Q_BLOCK, NSA_ROWS), lambda g, i: (0, g)),
                  resident((CMP_NEAR, NSA_ROWS), lambda g, i: (0, g)),
                  resident((SEL_COLS, ncp), lambda g, i: (0, 0))],
        out_specs=pl.BlockSpec((Q_BLOCK, NSA_HG * HEAD_DIM), lambda g, i: (i, g)),
        out_shape=jax.ShapeDtypeStruct((t, D_MODEL), BF16),
        scratch_shapes=[pltpu.VMEM((SEL_COLS // LANE, NSA_ROWS, 2 * HEAD_DIM), BF16),
                        pltpu.VMEM((2, FAR_TILES * Q_BLOCK, NSA_ROWS), F32),
                        pltpu.VMEM((ncp, NSA_ROWS), F32), pltpu.VMEM((SEL_COLS, Q_BLOCK), F32),
                        pltpu.VMEM((LANE, Q_BLOCK), F32), pltpu.VMEM((1, NSA_ROWS), F32),
                        pltpu.VMEM((V_ROWS, NSA_ROWS), F32), pltpu.VMEM((HEAD_DIM, NSA_ROWS), F32)],
        compiler_params=_cparams(("parallel", "arbitrary")),
        name="nsa_attention",
    )(qkh, gates, kcv, kcv, kcv_t, ksa, vt, qkh, vt, bd, bc, ovl)


def _nsa_layer(h, g_norm, w_in, w_o, layer, cmp_pos, cmp_w1, cmp_w2, bd, bc):
    t = h.shape[0]
    assert t // SEL_BLOCK <= SEL_COLS
    a = _rmsnorm(h, g_norm, BF16)
    tn = NSA_KV_DIM
    q_blocks = D_MODEL // tn
    qkh = _proj(a, w_in, layer, n_out=D_MODEL + 4 * NSA_KV_DIM, out_dtype=BF16, head_major=True, tn=tn,
                n_scaled=D_MODEL,
                col_block=lambda j: j + jnp.where(j >= q_blocks + 3, 1, 0))
    vt = _proj_t(a, w_in, layer, n_out=2 * NSA_KV_DIM, tn=tn, col_block=lambda j: q_blocks + 3 + 2 * j)
    gates = _proj(a, w_in, layer, n_out=LANE, out_dtype=F32, head_major=False, tn=LANE,
                  col_block=lambda j: (D_MODEL + 6 * NSA_KV_DIM) // LANE + j)
    kcv = _compress(qkh[N_HEADS:N_HEADS + 2 * NSA_GROUPS], cmp_pos, cmp_w1, cmp_w2)
    kcv_t = jnp.swapaxes(kcv, 1, 2)
    key_blk = (jnp.arange(t) // SEL_BLOCK) % LANE
    onehot = (key_blk[:, None] == jnp.arange(LANE)[None, :]).astype(BF16)
    ksa = jnp.concatenate([qkh[N_HEADS + 2 * NSA_GROUPS:N_HEADS + 3 * NSA_GROUPS],
                           jnp.broadcast_to(onehot, (NSA_GROUPS, t, LANE))], axis=-1)
    n = jnp.arange(t // CMP_STRIDE)[None, :]
    sblk = jnp.arange(SEL_COLS)[:, None]
    ovl = ((n >= 4 * sblk - 1) & (n <= 4 * sblk + 3)).astype(BF16)
    o = _nsa_attention(qkh, gates, kcv, kcv_t, ksa, vt, bd, bc, ovl)
    return _mm_res(o, w_o, layer, h, tm=1024, tn=512)


def kernel(x, norm_mix, norm_ffn, norm_final, rel_table, nsa_w_in, nsa_w_o, nsa_cmp_pos, nsa_cmp_w1,
           nsa_cmp_w2, fox_w_in, fox_b_f, fox_w_o, ffn_w_gate, ffn_w_up, ffn_w_down):
    b, t, d = x.shape
    depth = norm_mix.shape[0]
    bd, bc = _nsa_bias_templates(rel_table.astype(F32))
    nsa_w_in, fox_w_in = nsa_w_in.astype(F32), fox_w_in.astype(F32)
    ffn_w_gate, ffn_w_up = ffn_w_gate.astype(F32), ffn_w_up.astype(F32)
    nsa_w_o, fox_w_o, ffn_w_down = nsa_w_o.astype(BF16), fox_w_o.astype(BF16), ffn_w_down.astype(BF16)
    outs = []
    for bi in range(b):
        h = x[bi]
        for i in range(depth):
            j = i // 2
            if i % 2 == 0:
                h = _nsa_layer(h, norm_mix[i], nsa_w_in, nsa_w_o, j, nsa_cmp_pos[j], nsa_cmp_w1[j],
                               nsa_cmp_w2[j], bd, bc)
            else:
                h = _fox_layer(h, norm_mix[i], fox_w_in, fox_b_f[j], fox_w_o, j)
            h = _ffn(h, norm_ffn[i], ffn_w_gate, ffn_w_up, ffn_w_down, i)
        outs.append(_rmsnorm(h, norm_final, F32))
    return jnp.stack(outs, axis=0)
```

```python
import functools
import math

import jax
import jax.numpy as jnp
from jax import lax
from jax.experimental import pallas as pl
from jax.experimental.pallas import tpu as pltpu

F32 = jnp.float32
BF16 = jnp.bfloat16

D_MODEL = 4096
HEAD_DIM = 128
N_HEADS = D_MODEL // HEAD_DIM
NSA_GROUPS = 4
NSA_HG = N_HEADS // NSA_GROUPS
NSA_KV_DIM = NSA_GROUPS * HEAD_DIM
CMP_BLOCK = 32
CMP_STRIDE = 16
SEL_BLOCK = 64
SEL_TOPK = 16
WINDOW = 512
REL_BUCKETS = 32
REL_MAX_DIST = 128
Q_BLOCK = 128
RMS_EPS = 1e-6
NEG_INF = -1e30
LOG2E = 1.4426950408889634
Q_SCALE = HEAD_DIM ** -0.5 * LOG2E

LANE = 128
SEL_COLS = 256
CMP_PAD = 128
CMP_NEAR = 128
CMP_ROWS = 256
FAR_TILES = 4
NSA_ROWS = NSA_HG * Q_BLOCK
VMEM_LIMIT = 56 * 1024 * 1024


def _cparams(sem):
    return pltpu.CompilerParams(dimension_semantics=sem, vmem_limit_bytes=VMEM_LIMIT)


def _dot(a, b):
    return jnp.dot(a, b, preferred_element_type=F32)


def _dot_nt(a, b):
    return lax.dot_general(a, b, (((1,), (1,)), ((), ())), preferred_element_type=F32)


def _rmsnorm_kernel(x_ref, g_ref, o_ref):
    x = x_ref[...]
    ms = jnp.mean(x * x, axis=-1, keepdims=True)
    o_ref[...] = (x * lax.rsqrt(ms + RMS_EPS) * g_ref[...]).astype(o_ref.dtype)


def _rmsnorm(x, g, out_dtype):
    t, d = x.shape
    tm = min(256, t)
    return pl.pallas_call(
        _rmsnorm_kernel,
        grid=(t // tm,),
        in_specs=[pl.BlockSpec((tm, d), lambda i: (i, 0)),
                  pl.BlockSpec((1, d), lambda i: (0, 0))],
        out_specs=pl.BlockSpec((tm, d), lambda i: (i, 0)),
        out_shape=jax.ShapeDtypeStruct((t, d), out_dtype),
        compiler_params=_cparams(("parallel",)),
        name="rmsnorm",
    )(x, g.reshape(1, d))


def _proj_kernel(x_ref, w_ref, o_ref, *, head_major, n_scaled_blocks):
    r = _dot_nt(x_ref[...], w_ref[...].astype(BF16))
    if n_scaled_blocks:
        r = r * jnp.where(pl.program_id(1) < n_scaled_blocks, Q_SCALE, 1.0)
    if head_major:
        for s in range(o_ref.shape[0]):
            o_ref[s] = r[:, s * LANE:(s + 1) * LANE].astype(o_ref.dtype)
    else:
        o_ref[...] = r.astype(o_ref.dtype)


def _identity(j):
    return j


_row_block = functools.partial(pl.BlockSpec, pipeline_mode=pl.Buffered(1))


def _proj(x, w, layer, *, n_out, out_dtype, head_major, tn, col_block=_identity, n_scaled=0):
    t, k = x.shape
    n = n_out
    tm = min(1024, t)
    tn = min(tn, n)
    assert n_scaled % tn == 0 and n % tn == 0
    if head_major:
        out_shape = jax.ShapeDtypeStruct((n // LANE, t, LANE), out_dtype)
        out_spec = pl.BlockSpec((tn // LANE, tm, LANE), lambda i, j: (j, i, 0))
    else:
        out_shape = jax.ShapeDtypeStruct((t, n), out_dtype)
        out_spec = pl.BlockSpec((tm, tn), lambda i, j: (i, j))
    return pl.pallas_call(
        functools.partial(_proj_kernel, head_major=head_major, n_scaled_blocks=n_scaled // tn),
        grid=(t // tm, n // tn),
        in_specs=[pl.BlockSpec((tm, k), lambda i, j: (i, 0)),
                  pl.BlockSpec((None, tn, k), lambda i, j: (layer, col_block(j), 0))],
        out_specs=out_spec,
        out_shape=out_shape,
        compiler_params=_cparams(("parallel", "parallel")),
        name="proj",
    )(x, w)


def _proj_t_kernel(x_ref, w_ref, o_ref):
    r = _dot_nt(x_ref[...], w_ref[...].astype(BF16))
    ones = _ones_rows(r.shape[0], o_ref.dtype)
    for s in range(o_ref.shape[0]):
        o_ref[s, 0:HEAD_DIM, :] = r[:, s * LANE:(s + 1) * LANE].T.astype(o_ref.dtype)
        o_ref[s, HEAD_DIM:, :] = ones


def _proj_t(x, w, layer, *, n_out, tn, col_block=_identity):
    t, k = x.shape
    n = n_out
    tm = min(1024, t)
    tn = min(tn, n)
    return pl.pallas_call(
        _proj_t_kernel,
        grid=(t // tm, n // tn),
        in_specs=[pl.BlockSpec((tm, k), lambda i, j: (i, 0)),
                  pl.BlockSpec((None, tn, k), lambda i, j: (layer, col_block(j), 0))],
        out_specs=pl.BlockSpec((tn // LANE, V_ROWS, tm), lambda i, j: (j, 0, i)),
        out_shape=jax.ShapeDtypeStruct((n // LANE, V_ROWS, t), BF16),
        compiler_params=_cparams(("parallel", "parallel")),
        name="proj_t",
    )(x, w)


def _mm_res_kernel(x_ref, w_ref, r_ref, o_ref):
    o_ref[...] = r_ref[...] + _dot(x_ref[...], w_ref[...])


def _mm_res(x, w, layer, res, *, tm, tn):
    t, k = x.shape
    n = w.shape[2]
    tm = min(tm, t)
    return pl.pallas_call(
        _mm_res_kernel,
        grid=(t // tm, n // tn),
        in_specs=[_row_block((tm, k), lambda i, j: (i, 0)),
                  pl.BlockSpec((None, k, tn), lambda i, j: (layer, 0, j)),
                  pl.BlockSpec((tm, tn), lambda i, j: (i, j))],
        out_specs=pl.BlockSpec((tm, tn), lambda i, j: (i, j)),
        out_shape=jax.ShapeDtypeStruct((t, n), F32),
        compiler_params=_cparams(("parallel", "parallel")),
        name="mm_res",
    )(x, w, res)


def _gateup_kernel(x_ref, wg_ref, wu_ref, o_ref):
    x = x_ref[...]
    g = _dot(x, wg_ref[...].astype(BF16))
    u = _dot(x, wu_ref[...].astype(BF16))
    o_ref[...] = (g * jax.nn.sigmoid(g) * u).astype(o_ref.dtype)


def _gateup(x, wg, wu, layer, *, tn):
    t, k = x.shape
    n = wg.shape[2]
    tm = min(2048, t)
    return pl.pallas_call(
        _gateup_kernel,
        grid=(t // tm, n // tn),
        in_specs=[_row_block((tm, k), lambda i, j: (i, 0)),
                  pl.BlockSpec((None, k, tn), lambda i, j: (layer, 0, j)),
                  pl.BlockSpec((None, k, tn), lambda i, j: (layer, 0, j))],
        out_specs=pl.BlockSpec((tm, tn), lambda i, j: (i, j)),
        out_shape=jax.ShapeDtypeStruct((t, n), BF16),
        compiler_params=_cparams(("parallel", "parallel")),
        name="gateup",
    )(x, wg, wu)


def _ffn(h, g_norm, w_gate, w_up, w_down, layer):
    a = _rmsnorm(h, g_norm, BF16)
    h1 = _gateup(a, w_gate, w_up, layer, tn=256)
    return _mm_res(h1, w_down, layer, h, tm=1024, tn=256)


V_ROWS = HEAD_DIM + 16


def _ones_rows(n_cols, dtype):
    row = lax.broadcasted_iota(jnp.int32, (V_ROWS - HEAD_DIM, n_cols), 0)
    return jnp.where(row == 0, 1.0, 0.0).astype(dtype)


def _flash_init(m_ref, acc_ref):
    m_ref[...] = jnp.full(m_ref.shape, NEG_INF, F32)
    acc_ref[...] = jnp.zeros(acc_ref.shape, F32)


def _flash_update(s_t, v_t, m_ref, acc_ref):
    m_prev = m_ref[...]
    m_new = jnp.maximum(m_prev, jnp.max(s_t, axis=0, keepdims=True))
    alpha = jnp.exp2(m_prev - m_new)
    p = jnp.exp2(s_t - m_new)
    acc_ref[...] = alpha * acc_ref[...] + _dot(v_t, p.astype(BF16))
    m_ref[...] = m_new


def _softmax_tiles(tiles):
    mx = functools.reduce(jnp.maximum, [jnp.max(s_t, axis=0, keepdims=True) for s_t, _ in tiles])
    acc = None
    for s_t, v_t in tiles:
        a = _dot(v_t, jnp.exp2(s_t - mx).astype(BF16))
        acc = a if acc is None else acc + a
    return mx, acc


def _normalized(acc):
    return acc[0:HEAD_DIM] / acc[HEAD_DIM:HEAD_DIM + 1]


def _split3(x):
    hi = x.astype(BF16)
    r = x - hi.astype(F32)
    mid = r.astype(BF16)
    lo = (r - mid.astype(F32)).astype(BF16)
    return hi, mid, lo


N_PIECES = 3


def _cum_kernel(f_ref, b_ref, tri_ref, place_ref, qx_ref, kx_ref, carry_ref):
    @pl.when(pl.program_id(0) == 0)
    def _():
        carry_ref[...] = jnp.zeros_like(carry_ref)

    head_lane = lax.broadcasted_iota(jnp.int32, f_ref.shape, 1) < N_HEADS
    x = jnp.where(head_lane, f_ref[...] + b_ref[...], 0.0)
    log_f = jnp.minimum(x, 0.0) - jnp.log1p(jnp.exp(-jnp.abs(x)))
    c = jnp.dot(tri_ref[...], log_f, precision=lax.Precision.HIGHEST,
                preferred_element_type=F32) + carry_ref[...]
    carry_ref[...] = c[c.shape[0] - 1:, :]
    pieces = jnp.concatenate(_split3(c * LOG2E), axis=1)
    lane = lax.broadcasted_iota(jnp.int32, (c.shape[0], LANE), 1)
    q_const = jnp.where((lane >= N_PIECES) & (lane < 2 * N_PIECES), -1.0, 0.0)
    k_const = jnp.where(lane < N_PIECES, 1.0, 0.0)
    for h in range(N_HEADS):
        placed = _dot(pieces, place_ref[h])
        qx_ref[h] = (placed[:, :LANE] + q_const).astype(BF16)
        kx_ref[h] = (placed[:, LANE:] + k_const).astype(BF16)


def _forget_extras(f, b):
    t, n = f.shape
    tb = min(512, t)
    tri = jnp.tril(jnp.ones((tb, tb), F32))
    h = jnp.arange(N_HEADS)[:, None, None]
    row = jnp.arange(N_PIECES * LANE)[None, :, None]
    col = jnp.arange(2 * LANE)[None, None, :]
    piece, head = row // LANE, row % LANE
    place = ((head == h) & ((col == piece) | (col == LANE + N_PIECES + piece))).astype(BF16)
    out = jax.ShapeDtypeStruct((N_HEADS, t, LANE), BF16)
    return pl.pallas_call(
        _cum_kernel,
        grid=(t // tb,),
        in_specs=[pl.BlockSpec((tb, n), lambda i: (i, 0)),
                  pl.BlockSpec((1, n), lambda i: (0, 0)),
                  pl.BlockSpec((tb, tb), lambda i: (0, 0)),
                  pl.BlockSpec((N_HEADS, N_PIECES * LANE, 2 * LANE), lambda i: (0, 0, 0))],
        out_specs=[pl.BlockSpec((N_HEADS, tb, LANE), lambda i: (0, i, 0)),
                   pl.BlockSpec((N_HEADS, tb, LANE), lambda i: (0, i, 0))],
        out_shape=[out, out],
        scratch_shapes=[pltpu.VMEM((1, n), F32)],
        compiler_params=_cparams(("arbitrary",)),
        name="forget_extras",
    )(f, b, tri, place)


def _pipelined_loop(first, trips, per_trip, scores, consume, s_ref):
    def body(t, carry):
        c = first + t * per_trip
        for u in range(per_trip):
            s_next = scores(c + u + 1)
            consume(c + u, s_ref[u % 2])
            s_ref[(u + 1) % 2] = s_next
        return carry

    lax.fori_loop(0, trips, body, 0)


def _pipelined_chunks(n_chunks, scores, consume, s_ref):
    quads = n_chunks // 4
    _pipelined_loop(0, quads, 4, scores, consume, s_ref)
    _pipelined_loop(4 * quads, (n_chunks % 4) // 2, 2, scores, consume, s_ref)


def _fox_kernel(q_ref, qx_ref, k_ref, kx_ref, vt_ref, o_ref, ka_ref, qa_ref, s_ref, m_ref, acc_ref, *, tq, tk):
    i = pl.program_id(1)
    assert tq == 2 * tk

    @pl.when(i == 0)
    def _():
        ka_ref[:, 0:HEAD_DIM] = k_ref[...]
        ka_ref[:, HEAD_DIM:] = kx_ref[...]

    qa_ref[:, 0:HEAD_DIM] = q_ref[...]
    qa_ref[:, HEAD_DIM:] = qx_ref[...]
    _flash_init(m_ref, acc_ref)

    def scores(c):
        off = pl.multiple_of(c * tk, tk)
        return _dot_nt(ka_ref[pl.ds(off, tk), :], qa_ref[...])

    def consume(c, s_t):
        off = pl.multiple_of(c * tk, tk)
        _flash_update(s_t, vt_ref[:, pl.ds(off, tk)], m_ref, acc_ref)

    n_full = 2 * i
    s_ref[0] = scores(0)
    _pipelined_chunks(n_full, scores, consume, s_ref)
    key = lax.broadcasted_iota(jnp.int32, (tk, tq), 0)
    qry = lax.broadcasted_iota(jnp.int32, (tk, tq), 1)
    s_last = scores(n_full + 1)
    consume(n_full, jnp.where(key <= qry, s_ref[0], NEG_INF))
    consume(n_full + 1, jnp.where(key + tk <= qry, s_last, NEG_INF))
    o_ref[...] = _normalized(acc_ref[...]).T.astype(o_ref.dtype)


def _fox_attention(qkh, qx, kx, vt):
    t = qkh.shape[1]
    tq = min(1024, t)
    tk = tq // 2
    per_head = functools.partial(pl.BlockSpec, pipeline_mode=pl.Buffered(1))
    return pl.pallas_call(
        functools.partial(_fox_kernel, tq=tq, tk=tk),
        grid=(N_HEADS, t // tq),
        in_specs=[pl.BlockSpec((None, tq, HEAD_DIM), lambda h, i: (h, i, 0)),
                  pl.BlockSpec((None, tq, LANE), lambda h, i: (h, i, 0)),
                  per_head((None, t, HEAD_DIM), lambda h, i: (N_HEADS + h, 0, 0)),
                  per_head((None, t, LANE), lambda h, i: (h, 0, 0)),
                  per_head((None, V_ROWS, t), lambda h, i: (h, 0, 0))],
        out_specs=pl.BlockSpec((tq, HEAD_DIM), lambda h, i: (i, h)),
        out_shape=jax.ShapeDtypeStruct((t, D_MODEL), BF16),
        scratch_shapes=[pltpu.VMEM((t, 2 * HEAD_DIM), BF16),
                        pltpu.VMEM((tq, 2 * HEAD_DIM), BF16), pltpu.VMEM((2, tk, tq), F32),
                        pltpu.VMEM((1, tq), F32), pltpu.VMEM((V_ROWS, tq), F32)],
        compiler_params=_cparams(("parallel", "arbitrary")),
        name="fox_attention",
    )(qkh, qx, qkh, kx, vt)


def _fox_layer(h, g_norm, w_in, b_f, w_o, layer):
    a = _rmsnorm(h, g_norm, BF16)
    tn = 512
    qkh = _proj(a, w_in, layer, n_out=2 * D_MODEL, out_dtype=BF16, head_major=True, tn=tn,
                n_scaled=D_MODEL)
    vt = _proj_t(a, w_in, layer, n_out=D_MODEL, tn=tn, col_block=lambda j: 2 * D_MODEL // tn + j)
    f = _proj(a, w_in, layer, n_out=LANE, out_dtype=F32, head_major=False, tn=LANE,
              col_block=lambda j: 3 * D_MODEL // LANE + j)
    b = jnp.pad(b_f.astype(F32), (0, LANE - N_HEADS)).reshape(1, LANE)
    qx, kx = _forget_extras(f, b)
    o = _fox_attention(qkh, qx, kx, vt)
    return _mm_res(o, w_o, layer, h, tm=1024, tn=512)


def _rel_bucket_const(dist):
    n = jnp.maximum(dist, 0)
    max_exact = REL_BUCKETS // 2
    nf = jnp.maximum(n, max_exact).astype(F32)
    large = max_exact + (jnp.log(nf / max_exact) / math.log(REL_MAX_DIST / max_exact)
                         * (REL_BUCKETS - max_exact)).astype(jnp.int32)
    large = jnp.minimum(large, REL_BUCKETS - 1)
    return jnp.where(n < max_exact, n, large)


def _bias_kernel(tab_ref, bkt_ref, o_ref):
    h = pl.program_id(0)
    bkt = bkt_ref[...]
    acc = jnp.zeros(bkt.shape, F32)
    for b in range(REL_BUCKETS):
        acc = jnp.where(bkt == b, tab_ref[b, h], acc)
    o_ref[...] = (acc - tab_ref[REL_BUCKETS - 1, h]) * LOG2E


def _bias_template(rel_table, bkt):
    r, c = bkt.shape
    return pl.pallas_call(
        _bias_kernel,
        grid=(N_HEADS,),
        in_specs=[pl.BlockSpec(memory_space=pltpu.SMEM),
                  pl.BlockSpec((r, c), lambda h: (0, 0))],
        out_specs=pl.BlockSpec((r, c), lambda h: (0, h)),
        out_shape=jax.ShapeDtypeStruct((r, N_HEADS * c), F32),
        compiler_params=_cparams(("arbitrary",)),
        name="bias_template",
    )(rel_table, bkt)


def _nsa_bias_templates(rel_table):
    tl = jnp.arange(Q_BLOCK)[None, :]
    bd = _bias_template(rel_table, _rel_bucket_const(tl + Q_BLOCK - jnp.arange(2 * Q_BLOCK)[:, None]))
    m = jnp.arange(CMP_NEAR)[:, None]
    bc = _bias_template(rel_table, _rel_bucket_const(
        tl - CMP_STRIDE * (m - (CMP_NEAR - 8)) - (CMP_BLOCK - 1)))
    return bd, bc


def _gelu_tanh(x):
    return 0.5 * x * (1.0 + jnp.tanh(math.sqrt(2.0 / math.pi) * (x + 0.044715 * (x * x * x))))


def _cmp_kernel(kb_ref, pos_ref, w1_ref, w2_ref, o_ref, *, nb):
    half = CMP_STRIDE * HEAD_DIM
    kb = kb_ref[...].astype(F32)
    xa = (kb + pos_ref[0:1, :]).astype(BF16)
    xb = (kb + pos_ref[1:2, :]).astype(BF16)
    a = _dot(xa, w1_ref[0:half, :])
    b = _dot(xb, w1_ref[half:2 * half, :])
    pre = a + pltpu.roll(b, nb - 1, axis=0)
    out = _dot(_gelu_tanh(pre).astype(BF16), w2_ref[...])
    row = lax.broadcasted_iota(jnp.int32, out.shape, 0)
    out = jnp.where(row < nb - 1, out, 0.0)
    o_ref[0:CMP_PAD, :] = jnp.zeros((CMP_PAD, HEAD_DIM), F32)
    o_ref[CMP_PAD:CMP_PAD + nb, :] = out


def _compress(kvh, cmp_pos, cmp_w1, cmp_w2):
    t = kvh.shape[1]
    nb = t // CMP_STRIDE
    kb = kvh[:2 * NSA_GROUPS].reshape(2 * NSA_GROUPS, nb, CMP_STRIDE * HEAD_DIM)
    pos = cmp_pos.astype(F32).reshape(2, 2, CMP_STRIDE * HEAD_DIM)
    return pl.pallas_call(
        functools.partial(_cmp_kernel, nb=nb),
        grid=(2 * NSA_GROUPS,),
        in_specs=[pl.BlockSpec((None, nb, CMP_STRIDE * HEAD_DIM), lambda j: (j, 0, 0)),
                  pl.BlockSpec((None, 2, CMP_STRIDE * HEAD_DIM), lambda j: (j // NSA_GROUPS, 0, 0)),
                  pl.BlockSpec((None, CMP_BLOCK * HEAD_DIM, HEAD_DIM), lambda j: (j // NSA_GROUPS, 0, 0)),
                  pl.BlockSpec((None, HEAD_DIM, HEAD_DIM), lambda j: (j // NSA_GROUPS, 0, 0))],
        out_specs=pl.BlockSpec((None, CMP_PAD + nb, HEAD_DIM), lambda j: (j, 0, 0)),
        out_shape=jax.ShapeDtypeStruct((2 * NSA_GROUPS, CMP_PAD + nb, HEAD_DIM), F32),
        compiler_params=_cparams(("parallel",)),
        name="compress",
    )(kb, pos, cmp_w1.astype(BF16), cmp_w2.astype(BF16))


def _dot_split_rhs(w, x):
    hi, mid, lo = _split3(x)
    return _dot(w, hi) + _dot(w, mid) + _dot(w, lo)


def _nsa_kernel(q_ref, g_ref, kc_ref, vc_ref, vct_ref, ksa_ref, vst_ref, kw_ref, vwt_ref,
                bd_ref, bc_ref, ovl_ref, o_ref,
                qa_ref, s_ref, sc_ref, imp_ref, gate_ref, m_ref, acc_ref, out_ref):
    i = pl.program_id(1)
    rows = NSA_ROWS
    q = q_ref[...].reshape(rows, HEAD_DIM)
    tl_lane = lax.broadcasted_iota(jnp.int32, (Q_BLOCK, rows), 1) % Q_BLOCK
    key_row = lax.broadcasted_iota(jnp.int32, (Q_BLOCK, rows), 0)
    gate_ref[...] = jax.nn.sigmoid(g_ref[...]).T
    gate_row0 = 3 * NSA_HG * pl.program_id(0)

    def gate(branch):
        return jnp.concatenate([gate_ref[pl.ds(gate_row0 + 3 * hg + branch, 1), :] for hg in range(NSA_HG)], axis=1)

    def head_sum(p):
        acc = p[:, 0:Q_BLOCK]
        for hg in range(1, NSA_HG):
            acc = acc + p[:, hg * Q_BLOCK:(hg + 1) * Q_BLOCK]
        return acc

    n_first_near = 8 * i - (CMP_NEAR - 8)
    n_blocks = (jnp.maximum(n_first_near, 0) + CMP_ROWS - 1) // CMP_ROWS

    def far_rows(b):
        return pl.multiple_of(b * CMP_ROWS, CMP_ROWS)

    def scores_pass(b, mx):
        r0 = far_rows(b)
        kb = kc_ref[pl.ds(CMP_PAD + r0, CMP_ROWS), :].astype(BF16)
        row = r0 + lax.broadcasted_iota(jnp.int32, (CMP_ROWS, rows), 0)
        s = jnp.where(row < n_first_near, _dot_nt(kb, q), NEG_INF)
        sc_ref[pl.ds(r0, CMP_ROWS), :] = s
        return jnp.maximum(mx, jnp.max(s, axis=0, keepdims=True))

    mx_far = lax.fori_loop(0, n_blocks, scores_pass, jnp.full((1, rows), NEG_INF, F32))
    start = pl.multiple_of(8 * i + 8 + CMP_PAD - CMP_NEAR, 8)
    kcn = kc_ref[pl.ds(start, CMP_NEAR), :].astype(BF16)
    vcn_t = vc_ref[pl.ds(start, CMP_NEAR), :].T.astype(BF16)
    d_near = tl_lane - CMP_STRIDE * (key_row - (CMP_NEAR - 8)) - (CMP_BLOCK - 1)
    near_ok = (d_near >= 0) & (n_first_near + key_row >= 0)
    s_near = jnp.where(near_ok, _dot_nt(kcn, q) + bc_ref[...], NEG_INF)
    mx = jnp.maximum(mx_far, jnp.max(s_near, axis=0, keepdims=True))
    e_near = jnp.exp2(s_near - mx)

    def exp_pass(b, den):
        r0 = far_rows(b)
        e = jnp.exp2(sc_ref[pl.ds(r0, CMP_ROWS), :] - mx)
        sc_ref[pl.ds(r0, CMP_ROWS), :] = e
        return den + jnp.sum(e, axis=0, keepdims=True)

    den = lax.fori_loop(0, n_blocks, exp_pass, jnp.sum(e_near, axis=0, keepdims=True))
    inv = jnp.where(mx > 0.5 * NEG_INF, 1.0 / den, 0.0)
    p_near = e_near * inv
    sb = lax.broadcasted_iota(jnp.int32, (SEL_COLS, CMP_NEAR), 0)
    nn = n_first_near + lax.broadcasted_iota(jnp.int32, (SEL_COLS, CMP_NEAR), 1)
    ovl_near = jnp.where((nn >= 4 * sb - 1) & (nn <= 4 * sb + 3) & (nn >= 0), 1.0, 0.0).astype(BF16)
    out_ref[...] = _dot(vcn_t, p_near.astype(BF16))
    imp_ref[...] = _dot_split_rhs(ovl_near, head_sum(p_near))

    def out_pass(b, carry):
        r0 = far_rows(b)
        p = sc_ref[pl.ds(r0, CMP_ROWS), :] * inv
        out_ref[...] += _dot(vct_ref[:, pl.ds(CMP_PAD + r0, CMP_ROWS)].astype(BF16), p.astype(BF16))
        imp_ref[...] += _dot_split_rhs(ovl_ref[:, pl.ds(r0, CMP_ROWS)], head_sum(p))
        return carry

    lax.fori_loop(0, n_blocks, out_pass, 0)
    out_ref[...] = gate(0) * out_ref[...]

    imp = imp_ref[...]
    blk = lax.broadcasted_iota(jnp.int32, (SEL_COLS, Q_BLOCK), 0)
    tl = lax.broadcasted_iota(jnp.int32, (SEL_COLS, Q_BLOCK), 1)
    cur = 2 * i + (tl >= SEL_BLOCK).astype(jnp.int32)
    forced = (blk == 0) | (blk == cur) | (blk == cur - 1)
    causal_blk = blk * SEL_BLOCK <= Q_BLOCK * i + tl
    work = jnp.where(forced, -1.0, jnp.where(causal_blk, imp, -1.0))
    blk_f = blk.astype(F32)
    sel = jnp.where(forced, 1.0, 0.0)
    for _ in range(SEL_TOPK - 3):
        best = jnp.max(work, axis=0, keepdims=True)
        first = jnp.min(jnp.where(work == best, blk_f, float(SEL_COLS)), axis=0, keepdims=True)
        pick = blk_f == first
        sel = jnp.where(pick, 1.0, sel)
        work = jnp.where(pick, -2.0, work)
    amask = jnp.where(sel > 0.0, 0.0, NEG_INF)
    for half in range(SEL_COLS // LANE):
        a_t = amask[half * LANE:(half + 1) * LANE, :].T.astype(BF16)
        qa_ref[half, :, 0:HEAD_DIM] = q
        qa_ref[half, :, HEAD_DIM:2 * HEAD_DIM] = jnp.concatenate([a_t] * NSA_HG, axis=0)

    def sel_chunk(tile, n_keys):
        off = pl.multiple_of(tile * Q_BLOCK, Q_BLOCK)
        qa = qa_ref[tile // (LANE // 2)]
        return _dot_nt(ksa_ref[pl.ds(off, n_keys), :], qa), vst_ref[:, pl.ds(off, n_keys)]

    s_t, v_t = sel_chunk(i, Q_BLOCK)
    near = [(jnp.where(key_row <= tl_lane, s_t + bd_ref[Q_BLOCK:, :], NEG_INF), v_t)]
    s_t, v_t = sel_chunk(jnp.maximum(i - 1, 0), Q_BLOCK)
    near.append((jnp.where(i >= 1, s_t + bd_ref[:Q_BLOCK, :], NEG_INF), v_t))
    m_ref[...], acc_ref[...] = _softmax_tiles(near)
    n_far = jnp.maximum(i - 1, 0)
    n_chunks = n_far // FAR_TILES
    chunk_keys = FAR_TILES * Q_BLOCK

    def far_scores(c):
        tile = FAR_TILES * jnp.minimum(c, jnp.maximum(n_chunks - 1, 0))
        off = pl.multiple_of(tile * Q_BLOCK, chunk_keys)
        return _dot_nt(ksa_ref[pl.ds(off, chunk_keys), :], qa_ref[tile // (LANE // 2)])

    def far_consume(c, s_t):
        off = pl.multiple_of(c * chunk_keys, chunk_keys)
        _flash_update(s_t, vst_ref[:, pl.ds(off, chunk_keys)], m_ref, acc_ref)

    s_ref[0] = far_scores(0)
    _pipelined_chunks(n_chunks, far_scores, far_consume, s_ref)

    @pl.when(n_chunks % 2 == 1)
    def _():
        far_consume(n_chunks - 1, s_ref[0])

    left = n_far % FAR_TILES

    @pl.when(left >= 2)
    def _():
        s_t, v_t = sel_chunk(n_far - left, 2 * Q_BLOCK)
        _flash_update(s_t, v_t, m_ref, acc_ref)

    @pl.when(left % 2 == 1)
    def _():
        s_t, v_t = sel_chunk(n_far - 1, Q_BLOCK)
        _flash_update(s_t, v_t, m_ref, acc_ref)

    out_ref[...] += gate(1) * _normalized(acc_ref[...])

    def win_chunk(tile):
        off = pl.multiple_of(jnp.maximum(tile, 0) * Q_BLOCK, Q_BLOCK)
        return _dot_nt(kw_ref[pl.ds(off, Q_BLOCK), :], q), vwt_ref[:, pl.ds(off, Q_BLOCK)]

    s_t, v_t = win_chunk(i)
    win = [(jnp.where(key_row <= tl_lane, s_t + bd_ref[Q_BLOCK:, :], NEG_INF), v_t)]
    s_t, v_t = win_chunk(i - 1)
    win.append((jnp.where(i >= 1, s_t + bd_ref[:Q_BLOCK, :], NEG_INF), v_t))
    for back in (2, 3):
        s_t, v_t = win_chunk(i - back)
        win.append((jnp.where(i >= back, s_t, NEG_INF), v_t))
    s_t, v_t = win_chunk(i - 4)
    win.append((jnp.where((key_row > tl_lane) & (i >= 4), s_t, NEG_INF), v_t))
    _, acc_w = _softmax_tiles(win)
    o = out_ref[...] + gate(2) * _normalized(acc_w)
    for hg in range(NSA_HG):
        o_ref[:, hg * HEAD_DIM:(hg + 1) * HEAD_DIM] = o[:, hg * Q_BLOCK:(hg + 1) * Q_BLOCK].T.astype(o_ref.dtype)


def _nsa_attention(qkh, gates, kcv, kcv_t, ksa, vt, bd, bc, ovl):
    t = qkh.shape[1]
    ncp = t // CMP_STRIDE
    assert ncp % CMP_ROWS == 0
    resident = functools.partial(pl.BlockSpec, pipeline_mode=pl.Buffered(1))
    return pl.pallas_call(
        _nsa_kernel,
        grid=(NSA_GROUPS, t // Q_BLOCK),
        in_specs=[pl.BlockSpec((NSA_HG, Q_BLOCK, HEAD_DIM), lambda g, i: (g, i, 0)),
                  pl.BlockSpec((Q_BLOCK, LANE), lambda g, i: (i, 0)),
                  resident((None, CMP_PAD + ncp, HEAD_DIM), lambda g, i: (g, 0, 0)),
                  resident((None, CMP_PAD + ncp, HEAD_DIM), lambda g, i: (NSA_GROUPS + g, 0, 0)),
                  resident((None, HEAD_DIM, CMP_PAD + ncp), lambda g, i: (NSA_GROUPS + g, 0, 0)),
                  resident((None, t, 2 * HEAD_DIM), lambda g, i: (g, 0, 0)),
                  resident((None, V_ROWS, t), lambda g, i: (g, 0, 0)),
                  resident((None, t, HEAD_DIM), lambda g, i: (N_HEADS + 3 * NSA_GROUPS + g, 0, 0)),
                  resident((None, V_ROWS, t), lambda g, i: (NSA_GROUPS + g, 0, 0)),
                  resident((2 * Q_BLOCK, NSA_ROWS), lambda g, i: (0, g)),
                  resident((CMP_NEAR, NSA_ROWS), lambda g, i: (0, g)),
                  resident((SEL_COLS, ncp), lambda g, i: (0, 0))],
        out_specs=pl.BlockSpec((Q_BLOCK, NSA_HG * HEAD_DIM), lambda g, i: (i, g)),
        out_shape=jax.ShapeDtypeStruct((t, D_MODEL), BF16),
        scratch_shapes=[pltpu.VMEM((SEL_COLS // LANE, NSA_ROWS, 2 * HEAD_DIM), BF16),
                        pltpu.VMEM((2, FAR_TILES * Q_BLOCK, NSA_ROWS), F32),
                        pltpu.VMEM((ncp, NSA_ROWS), F32), pltpu.VMEM((SEL_COLS, Q_BLOCK), F32),
                        pltpu.VMEM((LANE, Q_BLOCK), F32), pltpu.VMEM((1, NSA_ROWS), F32),
                        pltpu.VMEM((V_ROWS, NSA_ROWS), F32), pltpu.VMEM((HEAD_DIM, NSA_ROWS), F32)],
        compiler_params=_cparams(("parallel", "arbitrary")),
        name="nsa_attention",
    )(qkh, gates, kcv, kcv, kcv_t, ksa, vt, qkh, vt, bd, bc, ovl)


def _nsa_layer(h, g_norm, w_in, w_o, layer, cmp_pos, cmp_w1, cmp_w2, bd, bc):
    t = h.shape[0]
    assert t // SEL_BLOCK <= SEL_COLS
    a = _rmsnorm(h, g_norm, BF16)
    tn = NSA_KV_DIM
    q_blocks = D_MODEL // tn
    qkh = _proj(a, w_in, layer, n_out=D_MODEL + 4 * NSA_KV_DIM, out_dtype=BF16, head_major=True, tn=tn,
                n_scaled=D_MODEL,
                col_block=lambda j: j + jnp.where(j >= q_blocks + 3, 1, 0))
    vt = _proj_t(a, w_in, layer, n_out=2 * NSA_KV_DIM, tn=tn, col_block=lambda j: q_blocks + 3 + 2 * j)
    gates = _proj(a, w_in, layer, n_out=LANE, out_dtype=F32, head_major=False, tn=LANE,
                  col_block=lambda j: (D_MODEL + 6 * NSA_KV_DIM) // LANE + j)
    kcv = _compress(qkh[N_HEADS:N_HEADS + 2 * NSA_GROUPS], cmp_pos, cmp_w1, cmp_w2)
    kcv_t = jnp.swapaxes(kcv, 1, 2)
    key_blk = (jnp.arange(t) // SEL_BLOCK) % LANE
    onehot = (key_blk[:, None] == jnp.arange(LANE)[None, :]).astype(BF16)
    ksa = jnp.concatenate([qkh[N_HEADS + 2 * NSA_GROUPS:N_HEADS + 3 * NSA_GROUPS],
                           jnp.broadcast_to(onehot, (NSA_GROUPS, t, LANE))], axis=-1)
    n = jnp.arange(t // CMP_STRIDE)[None, :]
    sblk = jnp.arange(SEL_COLS)[:, None]
    ovl = ((n >= 4 * sblk - 1) & (n <= 4 * sblk + 3)).astype(BF16)
    o = _nsa_attention(qkh, gates, kcv, kcv_t, ksa, vt, bd, bc, ovl)
    return _mm_res(o, w_o, layer, h, tm=1024, tn=512)


def kernel(x, norm_mix, norm_ffn, norm_final, rel_table, nsa_w_in, nsa_w_o, nsa_cmp_pos, nsa_cmp_w1,
           nsa_cmp_w2, fox_w_in, fox_b_f, fox_w_o, ffn_w_gate, ffn_w_up, ffn_w_down):
    b, t, d = x.shape
    depth = norm_mix.shape[0]
    bd, bc = _nsa_bias_templates(rel_table.astype(F32))
    nsa_w_in, fox_w_in = jnp.swapaxes(nsa_w_in.astype(F32), 1, 2), jnp.swapaxes(fox_w_in.astype(F32), 1, 2)
    ffn_w_gate, ffn_w_up = ffn_w_gate.astype(F32), ffn_w_up.astype(F32)
    nsa_w_o, fox_w_o, ffn_w_down = nsa_w_o.astype(BF16), fox_w_o.astype(BF16), ffn_w_down.astype(BF16)
    outs = []
    for bi in range(b):
        h = x[bi]
        for i in range(depth):
            j = i // 2
            if i % 2 == 0:
                h = _nsa_layer(h, norm_mix[i], nsa_w_in, nsa_w_o, j, nsa_cmp_pos[j], nsa_cmp_w1[j],
                               nsa_cmp_w2[j], bd, bc)
            else:
                h = _fox_layer(h, norm_mix[i], fox_w_in, fox_b_f[j], fox_w_o, j)
            h = _ffn(h, norm_ffn[i], ffn_w_gate, ffn_w_up, ffn_w_down, i)
        outs.append(_rmsnorm(h, norm_final, F32))
    return jnp.stack(outs, axis=0)
```

```python
import functools
import math

import jax
import jax.numpy as jnp
from jax import lax
from jax.experimental import pallas as pl
from jax.experimental.pallas import tpu as pltpu

F32 = jnp.float32
BF16 = jnp.bfloat16

D_MODEL = 4096
HEAD_DIM = 128
N_HEADS = D_MODEL // HEAD_DIM
NSA_GROUPS = 4
NSA_HG = N_HEADS // NSA_GROUPS
NSA_KV_DIM = NSA_GROUPS * HEAD_DIM
CMP_BLOCK = 32
CMP_STRIDE = 16
SEL_BLOCK = 64
SEL_TOPK = 16
WINDOW = 512
REL_BUCKETS = 32
REL_MAX_DIST = 128
Q_BLOCK = 128
RMS_EPS = 1e-6
NEG_INF = -1e30
LOG2E = 1.4426950408889634
Q_SCALE = HEAD_DIM ** -0.5 * LOG2E

LANE = 128
SEL_COLS = 256
CMP_PAD = 128
CMP_NEAR = 128
CMP_ROWS = 256
FAR_TILES = 4
NSA_ROWS = NSA_HG * Q_BLOCK
VMEM_LIMIT = 56 * 1024 * 1024


def _cparams(sem):
    return pltpu.CompilerParams(dimension_semantics=sem, vmem_limit_bytes=VMEM_LIMIT)


def _dot(a, b):
    return jnp.dot(a, b, preferred_element_type=F32)


def _dot_nt(a, b):
    return lax.dot_general(a, b, (((1,), (1,)), ((), ())), preferred_element_type=F32)


def _rmsnorm_kernel(x_ref, g_ref, o_ref):
    x = x_ref[...]
    ms = jnp.mean(x * x, axis=-1, keepdims=True)
    o_ref[...] = (x * lax.rsqrt(ms + RMS_EPS) * g_ref[...]).astype(o_ref.dtype)


def _rmsnorm(x, g, out_dtype):
    t, d = x.shape
    tm = min(256, t)
    return pl.pallas_call(
        _rmsnorm_kernel,
        grid=(t // tm,),
        in_specs=[pl.BlockSpec((tm, d), lambda i: (i, 0)),
                  pl.BlockSpec((1, d), lambda i: (0, 0))],
        out_specs=pl.BlockSpec((tm, d), lambda i: (i, 0)),
        out_shape=jax.ShapeDtypeStruct((t, d), out_dtype),
        compiler_params=_cparams(("parallel",)),
        name="rmsnorm",
    )(x, g.reshape(1, d))


def _proj_kernel(x_ref, w_ref, o_ref, *, head_major, n_scaled_blocks):
    r = _dot_nt(x_ref[...], w_ref[...].astype(BF16))
    if n_scaled_blocks:
        r = r * jnp.where(pl.program_id(1) < n_scaled_blocks, Q_SCALE, 1.0)
    if head_major:
        for s in range(o_ref.shape[0]):
            o_ref[s] = r[:, s * LANE:(s + 1) * LANE].astype(o_ref.dtype)
    else:
        o_ref[...] = r.astype(o_ref.dtype)


def _identity(j):
    return j


_row_block = functools.partial(pl.BlockSpec, pipeline_mode=pl.Buffered(1))


def _proj(x, w, layer, *, n_out, out_dtype, head_major, tn, col_block=_identity, n_scaled=0):
    t, k = x.shape
    n = n_out
    tm = min(1024, t)
    tn = min(tn, n)
    assert n_scaled % tn == 0 and n % tn == 0
    if head_major:
        out_shape = jax.ShapeDtypeStruct((n // LANE, t, LANE), out_dtype)
        out_spec = pl.BlockSpec((tn // LANE, tm, LANE), lambda i, j: (j, i, 0))
    else:
        out_shape = jax.ShapeDtypeStruct((t, n), out_dtype)
        out_spec = pl.BlockSpec((tm, tn), lambda i, j: (i, j))
    return pl.pallas_call(
        functools.partial(_proj_kernel, head_major=head_major, n_scaled_blocks=n_scaled // tn),
        grid=(t // tm, n // tn),
        in_specs=[pl.BlockSpec((tm, k), lambda i, j: (i, 0)),
                  pl.BlockSpec((None, tn, k), lambda i, j: (layer, col_block(j), 0))],
        out_specs=out_spec,
        out_shape=out_shape,
        compiler_params=_cparams(("parallel", "parallel")),
        name="proj",
    )(x, w)


def _proj_t_kernel(x_ref, w_ref, o_ref):
    r = _dot_nt(x_ref[...], w_ref[...].astype(BF16))
    ones = _ones_rows(r.shape[0], o_ref.dtype)
    for s in range(o_ref.shape[0]):
        o_ref[s, 0:HEAD_DIM, :] = r[:, s * LANE:(s + 1) * LANE].T.astype(o_ref.dtype)
        o_ref[s, HEAD_DIM:, :] = ones


def _proj_t(x, w, layer, *, n_out, tn, col_block=_identity):
    t, k = x.shape
    n = n_out
    tm = min(1024, t)
    tn = min(tn, n)
    return pl.pallas_call(
        _proj_t_kernel,
        grid=(t // tm, n // tn),
        in_specs=[pl.BlockSpec((tm, k), lambda i, j: (i, 0)),
                  pl.BlockSpec((None, tn, k), lambda i, j: (layer, col_block(j), 0))],
        out_specs=pl.BlockSpec((tn // LANE, V_ROWS, tm), lambda i, j: (j, 0, i)),
        out_shape=jax.ShapeDtypeStruct((n // LANE, V_ROWS, t), BF16),
        compiler_params=_cparams(("parallel", "parallel")),
        name="proj_t",
    )(x, w)


def _mm_res_kernel(x_ref, w_ref, r_ref, o_ref):
    o_ref[...] = r_ref[...] + _dot(x_ref[...], w_ref[...])


def _mm_res(x, w, layer, res, *, tm, tn, x_spec=pl.BlockSpec):
    t, k = x.shape
    n = w.shape[2]
    tm = min(tm, t)
    return pl.pallas_call(
        _mm_res_kernel,
        grid=(t // tm, n // tn),
        in_specs=[x_spec((tm, k), lambda i, j: (i, 0)),
                  pl.BlockSpec((None, k, tn), lambda i, j: (layer, 0, j)),
                  pl.BlockSpec((tm, tn), lambda i, j: (i, j))],
        out_specs=pl.BlockSpec((tm, tn), lambda i, j: (i, j)),
        out_shape=jax.ShapeDtypeStruct((t, n), F32),
        compiler_params=_cparams(("parallel", "parallel")),
        name="mm_res",
    )(x, w, res)


def _gateup_kernel(x_ref, wg_ref, wu_ref, o_ref):
    x = x_ref[...]
    g = _dot(x, wg_ref[...].astype(BF16))
    u = _dot(x, wu_ref[...].astype(BF16))
    o_ref[...] = (g * jax.nn.sigmoid(g) * u).astype(o_ref.dtype)


def _gateup(x, wg, wu, layer, *, tn):
    t, k = x.shape
    n = wg.shape[2]
    tm = min(2048, t)
    return pl.pallas_call(
        _gateup_kernel,
        grid=(t // tm, n // tn),
        in_specs=[_row_block((tm, k), lambda i, j: (i, 0)),
                  pl.BlockSpec((None, k, tn), lambda i, j: (layer, 0, j)),
                  pl.BlockSpec((None, k, tn), lambda i, j: (layer, 0, j))],
        out_specs=pl.BlockSpec((tm, tn), lambda i, j: (i, j)),
        out_shape=jax.ShapeDtypeStruct((t, n), BF16),
        compiler_params=_cparams(("parallel", "parallel")),
        name="gateup",
    )(x, wg, wu)


def _ffn(h, g_norm, w_gate, w_up, w_down, layer):
    a = _rmsnorm(h, g_norm, BF16)
    h1 = _gateup(a, w_gate, w_up, layer, tn=256)
    return _mm_res(h1, w_down, layer, h, tm=1024, tn=256, x_spec=_row_block)


V_ROWS = HEAD_DIM + 16
KEY_BLOCK = 256


def _ones_rows(n_cols, dtype):
    row = lax.broadcasted_iota(jnp.int32, (V_ROWS - HEAD_DIM, n_cols), 0)
    return jnp.where(row == 0, 1.0, 0.0).astype(dtype)


def _flash_init(m_ref, acc_ref):
    m_ref[...] = jnp.full(m_ref.shape, NEG_INF, F32)
    acc_ref[...] = jnp.zeros(acc_ref.shape, F32)


def _flash_update(s_t, v_t, m_ref, acc_ref):
    m_prev = m_ref[...]
    m_new = jnp.maximum(m_prev, jnp.max(s_t, axis=0, keepdims=True))
    alpha = jnp.exp2(m_prev - m_new)
    p = jnp.exp2(s_t - m_new)
    acc_ref[...] = alpha * acc_ref[...] + _dot(v_t, p.astype(BF16))
    m_ref[...] = m_new


def _flash_update_from(s_slot, v_t, m_ref, acc_ref):
    n_keys = s_slot.shape[0]
    m_prev = m_ref[...]
    m_new = jnp.maximum(m_prev, jnp.max(s_slot[...], axis=0, keepdims=True))
    acc = jnp.exp2(m_prev - m_new) * acc_ref[...]
    for r in range(0, n_keys, KEY_BLOCK):
        p = jnp.exp2(s_slot[r:r + KEY_BLOCK, :] - m_new)
        acc = acc + _dot(v_t[:, r:r + KEY_BLOCK], p.astype(BF16))
    acc_ref[...] = acc
    m_ref[...] = m_new


def _softmax_tiles(tiles):
    mx = functools.reduce(jnp.maximum, [jnp.max(s_t, axis=0, keepdims=True) for s_t, _ in tiles])
    acc = None
    for s_t, v_t in tiles:
        a = _dot(v_t, jnp.exp2(s_t - mx).astype(BF16))
        acc = a if acc is None else acc + a
    return mx, acc


def _normalized(acc):
    return acc[0:HEAD_DIM] / acc[HEAD_DIM:HEAD_DIM + 1]


def _split3(x):
    hi = x.astype(BF16)
    r = x - hi.astype(F32)
    mid = r.astype(BF16)
    lo = (r - mid.astype(F32)).astype(BF16)
    return hi, mid, lo


N_PIECES = 3


def _cum_kernel(f_ref, b_ref, tri_ref, place_ref, qx_ref, kx_ref, carry_ref):
    @pl.when(pl.program_id(0) == 0)
    def _():
        carry_ref[...] = jnp.zeros_like(carry_ref)

    head_lane = lax.broadcasted_iota(jnp.int32, f_ref.shape, 1) < N_HEADS
    x = jnp.where(head_lane, f_ref[...] + b_ref[...], 0.0)
    log_f = jnp.minimum(x, 0.0) - jnp.log1p(jnp.exp(-jnp.abs(x)))
    c = jnp.dot(tri_ref[...], log_f, precision=lax.Precision.HIGHEST,
                preferred_element_type=F32) + carry_ref[...]
    carry_ref[...] = c[c.shape[0] - 1:, :]
    pieces = jnp.concatenate(_split3(c * LOG2E), axis=1)
    lane = lax.broadcasted_iota(jnp.int32, (c.shape[0], LANE), 1)
    q_const = jnp.where((lane >= N_PIECES) & (lane < 2 * N_PIECES), -1.0, 0.0)
    k_const = jnp.where(lane < N_PIECES, 1.0, 0.0)
    for h in range(N_HEADS):
        placed = _dot(pieces, place_ref[h])
        qx_ref[h] = (placed[:, :LANE] + q_const).astype(BF16)
        kx_ref[h] = (placed[:, LANE:] + k_const).astype(BF16)


def _forget_extras(f, b):
    t, n = f.shape
    tb = min(512, t)
    tri = jnp.tril(jnp.ones((tb, tb), F32))
    h = jnp.arange(N_HEADS)[:, None, None]
    row = jnp.arange(N_PIECES * LANE)[None, :, None]
    col = jnp.arange(2 * LANE)[None, None, :]
    piece, head = row // LANE, row % LANE
    place = ((head == h) & ((col == piece) | (col == LANE + N_PIECES + piece))).astype(BF16)
    out = jax.ShapeDtypeStruct((N_HEADS, t, LANE), BF16)
    return pl.pallas_call(
        _cum_kernel,
        grid=(t // tb,),
        in_specs=[pl.BlockSpec((tb, n), lambda i: (i, 0)),
                  pl.BlockSpec((1, n), lambda i: (0, 0)),
                  pl.BlockSpec((tb, tb), lambda i: (0, 0)),
                  pl.BlockSpec((N_HEADS, N_PIECES * LANE, 2 * LANE), lambda i: (0, 0, 0))],
        out_specs=[pl.BlockSpec((N_HEADS, tb, LANE), lambda i: (0, i, 0)),
                   pl.BlockSpec((N_HEADS, tb, LANE), lambda i: (0, i, 0))],
        out_shape=[out, out],
        scratch_shapes=[pltpu.VMEM((1, n), F32)],
        compiler_params=_cparams(("arbitrary",)),
        name="forget_extras",
    )(f, b, tri, place)


def _pipelined_loop(first, trips, per_trip, scores, consume, s_ref):
    def body(t, carry):
        c = first + t * per_trip
        for u in range(per_trip):
            s_next = scores(c + u + 1)
            consume(c + u, s_ref.at[u % 2])
            s_ref[(u + 1) % 2] = s_next
        return carry

    lax.fori_loop(0, trips, body, 0)


def _pipelined_chunks(n_chunks, scores, consume, s_ref):
    quads = n_chunks // 4
    _pipelined_loop(0, quads, 4, scores, consume, s_ref)
    _pipelined_loop(4 * quads, (n_chunks % 4) // 2, 2, scores, consume, s_ref)


def _fox_kernel(q_ref, qx_ref, k_ref, kx_ref, vt_ref, o_ref, ka_ref, qa_ref, s_ref, m_ref, acc_ref, *, tq, tk):
    i = pl.program_id(1)
    assert tq == 2 * tk

    @pl.when(i == 0)
    def _():
        ka_ref[:, 0:HEAD_DIM] = k_ref[...]
        ka_ref[:, HEAD_DIM:] = kx_ref[...]

    qa_ref[:, 0:HEAD_DIM] = q_ref[...]
    qa_ref[:, HEAD_DIM:] = qx_ref[...]
    _flash_init(m_ref, acc_ref)

    def scores(c):
        off = pl.multiple_of(c * tk, tk)
        return _dot_nt(ka_ref[pl.ds(off, tk), :], qa_ref[...])

    def values(c):
        return vt_ref[:, pl.ds(pl.multiple_of(c * tk, tk), tk)]

    def consume(c, s_slot):
        _flash_update_from(s_slot, values(c), m_ref, acc_ref)

    n_full = 2 * i
    s_ref[0] = scores(0)
    _pipelined_chunks(n_full, scores, consume, s_ref)
    key = lax.broadcasted_iota(jnp.int32, (tk, tq), 0)
    qry = lax.broadcasted_iota(jnp.int32, (tk, tq), 1)
    s_last = scores(n_full + 1)
    _flash_update(jnp.where(key <= qry, s_ref[0], NEG_INF), values(n_full), m_ref, acc_ref)
    _flash_update(jnp.where(key + tk <= qry, s_last, NEG_INF), values(n_full + 1), m_ref, acc_ref)
    o_ref[...] = _normalized(acc_ref[...]).T.astype(o_ref.dtype)


def _fox_attention(qkh, qx, kx, vt):
    t = qkh.shape[1]
    tq = min(1024, t)
    tk = tq // 2
    return pl.pallas_call(
        functools.partial(_fox_kernel, tq=tq, tk=tk),
        grid=(N_HEADS, t // tq),
        in_specs=[pl.BlockSpec((None, tq, HEAD_DIM), lambda h, i: (h, i, 0)),
                  pl.BlockSpec((None, tq, LANE), lambda h, i: (h, i, 0)),
                  pl.BlockSpec((None,t, HEAD_DIM), lambda h, i: (N_HEADS + h, 0, 0)),
                  pl.BlockSpec((None,t, LANE), lambda h, i: (h, 0, 0)),
                  pl.BlockSpec((None,V_ROWS, t), lambda h, i: (h, 0, 0))],
        out_specs=pl.BlockSpec((tq, HEAD_DIM), lambda h, i: (i, h)),
        out_shape=jax.ShapeDtypeStruct((t, D_MODEL), BF16),
        scratch_shapes=[pltpu.VMEM((t, 2 * HEAD_DIM), BF16),
                        pltpu.VMEM((tq, 2 * HEAD_DIM), BF16), pltpu.VMEM((2, tk, tq), F32),
                        pltpu.VMEM((1, tq), F32), pltpu.VMEM((V_ROWS, tq), F32)],
        compiler_params=_cparams(("parallel", "arbitrary")),
        name="fox_attention",
    )(qkh, qx, qkh, kx, vt)


def _fox_layer(h, g_norm, w_in, b_f, w_o, layer):
    a = _rmsnorm(h, g_norm, BF16)
    tn = 512
    qkh = _proj(a, w_in, layer, n_out=2 * D_MODEL, out_dtype=BF16, head_major=True, tn=tn,
                n_scaled=D_MODEL)
    vt = _proj_t(a, w_in, layer, n_out=D_MODEL, tn=tn, col_block=lambda j: 2 * D_MODEL // tn + j)
    f = _proj(a, w_in, layer, n_out=LANE, out_dtype=F32, head_major=False, tn=LANE,
              col_block=lambda j: 3 * D_MODEL // LANE + j)
    b = jnp.pad(b_f.astype(F32), (0, LANE - N_HEADS)).reshape(1, LANE)
    qx, kx = _forget_extras(f, b)
    o = _fox_attention(qkh, qx, kx, vt)
    return _mm_res(o, w_o, layer, h, tm=1024, tn=512)


def _rel_bucket_const(dist):
    n = jnp.maximum(dist, 0)
    max_exact = REL_BUCKETS // 2
    nf = jnp.maximum(n, max_exact).astype(F32)
    large = max_exact + (jnp.log(nf / max_exact) / math.log(REL_MAX_DIST / max_exact)
                         * (REL_BUCKETS - max_exact)).astype(jnp.int32)
    large = jnp.minimum(large, REL_BUCKETS - 1)
    return jnp.where(n < max_exact, n, large)


def _bias_kernel(tab_ref, bkt_ref, o_ref):
    h = pl.program_id(0)
    bkt = bkt_ref[...]
    acc = jnp.zeros(bkt.shape, F32)
    for b in range(REL_BUCKETS):
        acc = jnp.where(bkt == b, tab_ref[b, h], acc)
    o_ref[...] = (acc - tab_ref[REL_BUCKETS - 1, h]) * LOG2E


def _bias_template(rel_table, bkt):
    r, c = bkt.shape
    return pl.pallas_call(
        _bias_kernel,
        grid=(N_HEADS,),
        in_specs=[pl.BlockSpec(memory_space=pltpu.SMEM),
                  pl.BlockSpec((r, c), lambda h: (0, 0))],
        out_specs=pl.BlockSpec((r, c), lambda h: (0, h)),
        out_shape=jax.ShapeDtypeStruct((r, N_HEADS * c), F32),
        compiler_params=_cparams(("arbitrary",)),
        name="bias_template",
    )(rel_table, bkt)


def _nsa_bias_templates(rel_table):
    tl = jnp.arange(Q_BLOCK)[None, :]
    bd = _bias_template(rel_table, _rel_bucket_const(tl + Q_BLOCK - jnp.arange(2 * Q_BLOCK)[:, None]))
    m = jnp.arange(CMP_NEAR)[:, None]
    bc = _bias_template(rel_table, _rel_bucket_const(
        tl - CMP_STRIDE * (m - (CMP_NEAR - 8)) - (CMP_BLOCK - 1)))
    return bd, bc


def _gelu_tanh(x):
    return 0.5 * x * (1.0 + jnp.tanh(math.sqrt(2.0 / math.pi) * (x + 0.044715 * (x * x * x))))


def _cmp_kernel(kb_ref, pos_ref, w1_ref, w2_ref, o_ref, *, nb):
    half = CMP_STRIDE * HEAD_DIM
    kb = kb_ref[...].astype(F32)
    xa = (kb + pos_ref[0:1, :]).astype(BF16)
    xb = (kb + pos_ref[1:2, :]).astype(BF16)
    a = _dot(xa, w1_ref[0:half, :])
    b = _dot(xb, w1_ref[half:2 * half, :])
    pre = a + pltpu.roll(b, nb - 1, axis=0)
    out = _dot(_gelu_tanh(pre).astype(BF16), w2_ref[...])
    row = lax.broadcasted_iota(jnp.int32, out.shape, 0)
    out = jnp.where(row < nb - 1, out, 0.0)
    o_ref[0:CMP_PAD, :] = jnp.zeros((CMP_PAD, HEAD_DIM), F32)
    o_ref[CMP_PAD:CMP_PAD + nb, :] = out


def _compress(kvh, cmp_pos, cmp_w1, cmp_w2):
    t = kvh.shape[1]
    nb = t // CMP_STRIDE
    kb = kvh[:2 * NSA_GROUPS].reshape(2 * NSA_GROUPS, nb, CMP_STRIDE * HEAD_DIM)
    pos = cmp_pos.astype(F32).reshape(2, 2, CMP_STRIDE * HEAD_DIM)
    return pl.pallas_call(
        functools.partial(_cmp_kernel, nb=nb),
        grid=(2 * NSA_GROUPS,),
        in_specs=[pl.BlockSpec((None, nb, CMP_STRIDE * HEAD_DIM), lambda j: (j, 0, 0)),
                  pl.BlockSpec((None, 2, CMP_STRIDE * HEAD_DIM), lambda j: (j // NSA_GROUPS, 0, 0)),
                  pl.BlockSpec((None, CMP_BLOCK * HEAD_DIM, HEAD_DIM), lambda j: (j // NSA_GROUPS, 0, 0)),
                  pl.BlockSpec((None, HEAD_DIM, HEAD_DIM), lambda j: (j // NSA_GROUPS, 0, 0))],
        out_specs=pl.BlockSpec((None, CMP_PAD + nb, HEAD_DIM), lambda j: (j, 0, 0)),
        out_shape=jax.ShapeDtypeStruct((2 * NSA_GROUPS, CMP_PAD + nb, HEAD_DIM), F32),
        compiler_params=_cparams(("parallel",)),
        name="compress",
    )(kb, pos, cmp_w1.astype(BF16), cmp_w2.astype(BF16))


def _dot_split_rhs(w, x):
    hi, mid, lo = _split3(x)
    return _dot(w, hi) + _dot(w, mid) + _dot(w, lo)


def _nsa_kernel(q_ref, g_ref, kc_ref, vc_ref, vct_ref, ksa_ref, vst_ref, kw_ref, vwt_ref,
                bd_ref, bc_ref, ovl_ref, o_ref,
                qa_ref, s_ref, sc_ref, imp_ref, gate_ref, m_ref, acc_ref, out_ref):
    i = pl.program_id(1)
    rows = NSA_ROWS
    q = q_ref[...].reshape(rows, HEAD_DIM)
    tl_lane = lax.broadcasted_iota(jnp.int32, (Q_BLOCK, rows), 1) % Q_BLOCK
    key_row = lax.broadcasted_iota(jnp.int32, (Q_BLOCK, rows), 0)
    gate_ref[...] = jax.nn.sigmoid(g_ref[...]).T
    gate_row0 = 3 * NSA_HG * pl.program_id(0)

    def gate(branch):
        return jnp.concatenate([gate_ref[pl.ds(gate_row0 + 3 * hg + branch, 1), :] for hg in range(NSA_HG)], axis=1)

    def head_sum(p):
        acc = p[:, 0:Q_BLOCK]
        for hg in range(1, NSA_HG):
            acc = acc + p[:, hg * Q_BLOCK:(hg + 1) * Q_BLOCK]
        return acc

    n_first_near = 8 * i - (CMP_NEAR - 8)
    n_blocks = (jnp.maximum(n_first_near, 0) + CMP_ROWS - 1) // CMP_ROWS

    def far_rows(b):
        return pl.multiple_of(b * CMP_ROWS, CMP_ROWS)

    def scores_pass(b, mx):
        r0 = far_rows(b)
        kb = kc_ref[pl.ds(CMP_PAD + r0, CMP_ROWS), :].astype(BF16)
        row = r0 + lax.broadcasted_iota(jnp.int32, (CMP_ROWS, rows), 0)
        s = jnp.where(row < n_first_near, _dot_nt(kb, q), NEG_INF)
        sc_ref[pl.ds(r0, CMP_ROWS), :] = s
        return jnp.maximum(mx, jnp.max(s, axis=0, keepdims=True))

    mx_far = lax.fori_loop(0, n_blocks, scores_pass, jnp.full((1, rows), NEG_INF, F32))
    start = pl.multiple_of(8 * i + 8 + CMP_PAD - CMP_NEAR, 8)
    kcn = kc_ref[pl.ds(start, CMP_NEAR), :].astype(BF16)
    vcn_t = vc_ref[pl.ds(start, CMP_NEAR), :].T.astype(BF16)
    d_near = tl_lane - CMP_STRIDE * (key_row - (CMP_NEAR - 8)) - (CMP_BLOCK - 1)
    near_ok = (d_near >= 0) & (n_first_near + key_row >= 0)
    s_near = jnp.where(near_ok, _dot_nt(kcn, q) + bc_ref[...], NEG_INF)
    mx = jnp.maximum(mx_far, jnp.max(s_near, axis=0, keepdims=True))
    e_near = jnp.exp2(s_near - mx)

    def exp_pass(b, den):
        r0 = far_rows(b)
        e = jnp.exp2(sc_ref[pl.ds(r0, CMP_ROWS), :] - mx)
        sc_ref[pl.ds(r0, CMP_ROWS), :] = e
        return den + jnp.sum(e, axis=0, keepdims=True)

    den = lax.fori_loop(0, n_blocks, exp_pass, jnp.sum(e_near, axis=0, keepdims=True))
    inv = jnp.where(mx > 0.5 * NEG_INF, 1.0 / den, 0.0)
    p_near = e_near * inv
    sb = lax.broadcasted_iota(jnp.int32, (SEL_COLS, CMP_NEAR), 0)
    nn = n_first_near + lax.broadcasted_iota(jnp.int32, (SEL_COLS, CMP_NEAR), 1)
    ovl_near = jnp.where((nn >= 4 * sb - 1) & (nn <= 4 * sb + 3) & (nn >= 0), 1.0, 0.0).astype(BF16)
    out_ref[...] = _dot(vcn_t, p_near.astype(BF16))
    imp_ref[...] = _dot_split_rhs(ovl_near, head_sum(p_near))

    def out_pass(b, carry):
        r0 = far_rows(b)
        p = sc_ref[pl.ds(r0, CMP_ROWS), :] * inv
        out_ref[...] += _dot(vct_ref[:, pl.ds(CMP_PAD + r0, CMP_ROWS)].astype(BF16), p.astype(BF16))
        imp_ref[...] += _dot_split_rhs(ovl_ref[:, pl.ds(r0, CMP_ROWS)], head_sum(p))
        return carry

    lax.fori_loop(0, n_blocks, out_pass, 0)
    out_ref[...] = gate(0) * out_ref[...]

    imp = imp_ref[...]
    blk = lax.broadcasted_iota(jnp.int32, (SEL_COLS, Q_BLOCK), 0)
    tl = lax.broadcasted_iota(jnp.int32, (SEL_COLS, Q_BLOCK), 1)
    cur = 2 * i + (tl >= SEL_BLOCK).astype(jnp.int32)
    forced = (blk == 0) | (blk == cur) | (blk == cur - 1)
    causal_blk = blk * SEL_BLOCK <= Q_BLOCK * i + tl
    work = jnp.where(forced, -1.0, jnp.where(causal_blk, imp, -1.0))
    blk_f = blk.astype(F32)
    sel = jnp.where(forced, 1.0, 0.0)
    for _ in range(SEL_TOPK - 3):
        best = jnp.max(work, axis=0, keepdims=True)
        first = jnp.min(jnp.where(work == best, blk_f, float(SEL_COLS)), axis=0, keepdims=True)
        pick = blk_f == first
        sel = jnp.where(pick, 1.0, sel)
        work = jnp.where(pick, -2.0, work)
    amask = jnp.where(sel > 0.0, 0.0, NEG_INF)
    for half in range(SEL_COLS // LANE):
        a_t = amask[half * LANE:(half + 1) * LANE, :].T.astype(BF16)
        qa_ref[half, :, 0:HEAD_DIM] = q
        qa_ref[half, :, HEAD_DIM:2 * HEAD_DIM] = jnp.concatenate([a_t] * NSA_HG, axis=0)

    def sel_chunk(tile, n_keys):
        off = pl.multiple_of(tile * Q_BLOCK, Q_BLOCK)
        qa = qa_ref[tile // (LANE // 2)]
        return _dot_nt(ksa_ref[pl.ds(off, n_keys), :], qa), vst_ref[:, pl.ds(off, n_keys)]

    s_t, v_t = sel_chunk(i, Q_BLOCK)
    near = [(jnp.where(key_row <= tl_lane, s_t + bd_ref[Q_BLOCK:, :], NEG_INF), v_t)]
    s_t, v_t = sel_chunk(jnp.maximum(i - 1, 0), Q_BLOCK)
    near.append((jnp.where(i >= 1, s_t + bd_ref[:Q_BLOCK, :], NEG_INF), v_t))
    m_ref[...], acc_ref[...] = _softmax_tiles(near)
    n_far = jnp.maximum(i - 1, 0)
    n_chunks = n_far // FAR_TILES
    chunk_keys = FAR_TILES * Q_BLOCK

    def far_scores(c):
        tile = FAR_TILES * jnp.minimum(c, jnp.maximum(n_chunks - 1, 0))
        off = pl.multiple_of(tile * Q_BLOCK, chunk_keys)
        return _dot_nt(ksa_ref[pl.ds(off, chunk_keys), :], qa_ref[tile // (LANE // 2)])

    def far_consume(c, s_slot):
        off = pl.multiple_of(c * chunk_keys, chunk_keys)
        _flash_update_from(s_slot, vst_ref[:, pl.ds(off, chunk_keys)], m_ref, acc_ref)

    s_ref[0] = far_scores(0)
    _pipelined_chunks(n_chunks, far_scores, far_consume, s_ref)

    @pl.when(n_chunks % 2 == 1)
    def _():
        far_consume(n_chunks - 1, s_ref.at[0])

    left = n_far % FAR_TILES

    @pl.when(left >= 2)
    def _():
        s_t, v_t = sel_chunk(n_far - left, 2 * Q_BLOCK)
        _flash_update(s_t, v_t, m_ref, acc_ref)

    @pl.when(left % 2 == 1)
    def _():
        s_t, v_t = sel_chunk(n_far - 1, Q_BLOCK)
        _flash_update(s_t, v_t, m_ref, acc_ref)

    out_ref[...] += gate(1) * _normalized(acc_ref[...])

    def win_chunk(tile):
        off = pl.multiple_of(jnp.maximum(tile, 0) * Q_BLOCK, Q_BLOCK)
        return _dot_nt(kw_ref[pl.ds(off, Q_BLOCK), :], q), vwt_ref[:, pl.ds(off, Q_BLOCK)]

    s_t, v_t = win_chunk(i)
    win = [(jnp.where(key_row <= tl_lane, s_t + bd_ref[Q_BLOCK:, :], NEG_INF), v_t)]
    s_t, v_t = win_chunk(i - 1)
    win.append((jnp.where(i >= 1, s_t + bd_ref[:Q_BLOCK, :], NEG_INF), v_t))
    for back in (2, 3):
        s_t, v_t = win_chunk(i - back)
        win.append((jnp.where(i >= back, s_t, NEG_INF), v_t))
    s_t, v_t = win_chunk(i - 4)
    win.append((jnp.where((key_row > tl_lane) & (i >= 4), s_t, NEG_INF), v_t))
    _, acc_w = _softmax_tiles(win)
    o = out_ref[...] + gate(2) * _normalized(acc_w)
    for hg in range(NSA_HG):
        o_ref[:, hg * HEAD_DIM:(hg + 1) * HEAD_DIM] = o[:, hg * Q_BLOCK:(hg + 1) * Q_BLOCK].T.astype(o_ref.dtype)


def _nsa_attention(qkh, gates, kcv, kcv_t, ksa, vt, bd, bc, ovl):
    t = qkh.shape[1]
    ncp = t // CMP_STRIDE
    assert ncp % CMP_ROWS == 0
    resident = functools.partial(pl.BlockSpec, pipeline_mode=pl.Buffered(1))
    return pl.pallas_call(
        _nsa_kernel,
        grid=(NSA_GROUPS, t // Q_BLOCK),
        in_specs=[pl.BlockSpec((NSA_HG, Q_BLOCK, HEAD_DIM), lambda g, i: (g, i, 0)),
                  pl.BlockSpec((Q_BLOCK, LANE), lambda g, i: (i, 0)),
                  resident((None, CMP_PAD + ncp, HEAD_DIM), lambda g, i: (g, 0, 0)),
                  resident((None, CMP_PAD + ncp, HEAD_DIM), lambda g, i: (NSA_GROUPS + g, 0, 0)),
                  resident((None, HEAD_DIM, CMP_PAD + ncp), lambda g, i: (NSA_GROUPS + g, 0, 0)),
                  resident((None, t, 2 * HEAD_DIM), lambda g, i: (g, 0, 0)),
                  resident((None, V_ROWS, t), lambda g, i: (g, 0, 0)),
                  resident((None, t, HEAD_DIM), lambda g, i: (N_HEADS + 3 * NSA_GROUPS + g, 0, 0)),
                  resident((None, V_ROWS, t), lambda g, i: (NSA_GROUPS + g, 0, 0)),
                  resident((2 * Q_BLOCK, NSA_ROWS), lambda g, i: (0, g)),
                  resident((CMP_NEAR, NSA_ROWS), lambda g, i: (0, g)),
                  resident((SEL_COLS, ncp), lambda g, i: (0, 0))],
        out_specs=pl.BlockSpec((Q_BLOCK, NSA_HG * HEAD_DIM), lambda g, i: (i, g)),
        out_shape=jax.ShapeDtypeStruct((t, D_MODEL), BF16),
        scratch_shapes=[pltpu.VMEM((SEL_COLS // LANE, NSA_ROWS, 2 * HEAD_DIM), BF16),
                        pltpu.VMEM((2, FAR_TILES * Q_BLOCK, NSA_ROWS), F32),
                        pltpu.VMEM((ncp, NSA_ROWS), F32), pltpu.VMEM((SEL_COLS, Q_BLOCK), F32),
                        pltpu.VMEM((LANE, Q_BLOCK), F32), pltpu.VMEM((1, NSA_ROWS), F32),
                        pltpu.VMEM((V_ROWS, NSA_ROWS), F32), pltpu.VMEM((HEAD_DIM, NSA_ROWS), F32)],
        compiler_params=_cparams(("parallel", "arbitrary")),
        name="nsa_attention",
    )(qkh, gates, kcv, kcv, kcv_t, ksa, vt, qkh, vt, bd, bc, ovl)


def _nsa_layer(h, g_norm, w_in, w_o, layer, cmp_pos, cmp_w1, cmp_w2, bd, bc):
    t = h.shape[0]
    assert t // SEL_BLOCK <= SEL_COLS
    a = _rmsnorm(h, g_norm, BF16)
    tn = NSA_KV_DIM
    q_blocks = D_MODEL // tn
    qkh = _proj(a, w_in, layer, n_out=D_MODEL + 4 * NSA_KV_DIM, out_dtype=BF16, head_major=True, tn=tn,
                n_scaled=D_MODEL,
                col_block=lambda j: j + jnp.where(j >= q_blocks + 3, 1, 0))
    vt = _proj_t(a, w_in, layer, n_out=2 * NSA_KV_DIM, tn=tn, col_block=lambda j: q_blocks + 3 + 2 * j)
    gates = _proj(a, w_in, layer, n_out=LANE, out_dtype=F32, head_major=False, tn=LANE,
                  col_block=lambda j: (D_MODEL + 6 * NSA_KV_DIM) // LANE + j)
    kcv = _compress(qkh[N_HEADS:N_HEADS + 2 * NSA_GROUPS], cmp_pos, cmp_w1, cmp_w2)
    kcv_t = jnp.swapaxes(kcv, 1, 2)
    key_blk = (jnp.arange(t) // SEL_BLOCK) % LANE
    onehot = (key_blk[:, None] == jnp.arange(LANE)[None, :]).astype(BF16)
    ksa = jnp.concatenate([qkh[N_HEADS + 2 * NSA_GROUPS:N_HEADS + 3 * NSA_GROUPS],
                           jnp.broadcast_to(onehot, (NSA_GROUPS, t, LANE))], axis=-1)
    n = jnp.arange(t // CMP_STRIDE)[None, :]
    sblk = jnp.arange(SEL_COLS)[:, None]
    ovl = ((n >= 4 * sblk - 1) & (n <= 4 * sblk + 3)).astype(BF16)
    o = _nsa_attention(qkh, gates, kcv, kcv_t, ksa, vt, bd, bc, ovl)
    return _mm_res(o, w_o, layer, h, tm=1024, tn=512)


def kernel(x, norm_mix, norm_ffn, norm_final, rel_table, nsa_w_in, nsa_w_o, nsa_cmp_pos, nsa_cmp_w1,
           nsa_cmp_w2, fox_w_in, fox_b_f, fox_w_o, ffn_w_gate, ffn_w_up, ffn_w_down):
    b, t, d = x.shape
    depth = norm_mix.shape[0]
    bd, bc = _nsa_bias_templates(rel_table.astype(F32))
    nsa_w_in, fox_w_in = jnp.swapaxes(nsa_w_in.astype(F32), 1, 2), jnp.swapaxes(fox_w_in.astype(F32), 1, 2)
    ffn_w_gate, ffn_w_up = ffn_w_gate.astype(F32), ffn_w_up.astype(F32)
    nsa_w_o, fox_w_o, ffn_w_down = nsa_w_o.astype(BF16), fox_w_o.astype(BF16), ffn_w_down.astype(BF16)
    outs = []
    for bi in range(b):
        h = x[bi]
        for i in range(depth):
            j = i // 2
            if i % 2 == 0:
                h = _nsa_layer(h, norm_mix[i], nsa_w_in, nsa_w_o, j, nsa_cmp_pos[j], nsa_cmp_w1[j],
                               nsa_cmp_w2[j], bd, bc)
            else:
                h = _fox_layer(h, norm_mix[i], fox_w_in, fox_b_f[j], fox_w_o, j)
            h = _ffn(h, norm_ffn[i], ffn_w_gate, ffn_w_up, ffn_w_down, i)
        outs.append(_rmsnorm(h, norm_final, F32))
    return jnp.stack(outs, axis=0)
```

```python
import functools
import math

import jax
import jax.numpy as jnp
from jax import lax
from jax.experimental import pallas as pl
from jax.experimental.pallas import tpu as pltpu

F32 = jnp.float32
BF16 = jnp.bfloat16

D_MODEL = 4096
HEAD_DIM = 128
N_HEADS = D_MODEL // HEAD_DIM
NSA_GROUPS = 4
NSA_HG = N_HEADS // NSA_GROUPS
NSA_KV_DIM = NSA_GROUPS * HEAD_DIM
CMP_BLOCK = 32
CMP_STRIDE = 16
SEL_BLOCK = 64
SEL_TOPK = 16
WINDOW = 512
REL_BUCKETS = 32
REL_MAX_DIST = 128
Q_BLOCK = 128
RMS_EPS = 1e-6
NEG_INF = -1e30
LOG2E = 1.4426950408889634
Q_SCALE = HEAD_DIM ** -0.5 * LOG2E

LANE = 128
SEL_COLS = 256
CMP_PAD = 128
CMP_NEAR = 128
CMP_ROWS = 256
FAR_TILES = 4
NSA_ROWS = NSA_HG * Q_BLOCK
V_ROWS = HEAD_DIM + 16
KEY_BLOCK = 256
N_PIECES = 3

RMS_ROWS = 256
PROJ_TILE = (1024, 512)
GATEUP_TILE = (2048, 256)
DOWN_TILE = (1024, 256)
WO_TILE = (1024, 512)
FOX_Q_TILE = 1024
CUM_ROWS = 512
VMEM_LIMIT = 56 * 1024 * 1024


def _cparams(sem):
    return pltpu.CompilerParams(dimension_semantics=sem, vmem_limit_bytes=VMEM_LIMIT)


def _dot(a, b):
    return jnp.dot(a, b, preferred_element_type=F32)


def _dot_nt(a, b):
    return lax.dot_general(a, b, (((1,), (1,)), ((), ())), preferred_element_type=F32)


def _rmsnorm_kernel(x_ref, g_ref, o_ref):
    x = x_ref[...]
    ms = jnp.mean(x * x, axis=-1, keepdims=True)
    o_ref[...] = (x * lax.rsqrt(ms + RMS_EPS) * g_ref[...]).astype(o_ref.dtype)


def _rmsnorm(x, g, out_dtype):
    t, d = x.shape
    tm = min(RMS_ROWS, t)
    return pl.pallas_call(
        _rmsnorm_kernel,
        grid=(t // tm,),
        in_specs=[pl.BlockSpec((tm, d), lambda i: (i, 0)),
                  pl.BlockSpec((1, d), lambda i: (0, 0))],
        out_specs=pl.BlockSpec((tm, d), lambda i: (i, 0)),
        out_shape=jax.ShapeDtypeStruct((t, d), out_dtype),
        compiler_params=_cparams(("parallel",)),
        name="rmsnorm",
    )(x, g.reshape(1, d))


def _proj_kernel(x_ref, w_ref, o_ref, *, head_major, n_scaled_blocks):
    r = _dot_nt(x_ref[...], w_ref[...].astype(BF16))
    if n_scaled_blocks:
        r = r * jnp.where(pl.program_id(1) < n_scaled_blocks, Q_SCALE, 1.0)
    if head_major:
        for s in range(o_ref.shape[0]):
            o_ref[s] = r[:, s * LANE:(s + 1) * LANE].astype(o_ref.dtype)
    else:
        o_ref[...] = r.astype(o_ref.dtype)


def _identity(j):
    return j


_row_block = functools.partial(pl.BlockSpec, pipeline_mode=pl.Buffered(1))


def _proj(x, w, layer, *, n_out, out_dtype, head_major, tn=PROJ_TILE[1], col_block=_identity, n_scaled=0):
    t, k = x.shape
    n = n_out
    tm = min(PROJ_TILE[0], t)
    tn = min(tn, n)
    assert n_scaled % tn == 0 and n % tn == 0
    if head_major:
        out_shape = jax.ShapeDtypeStruct((n // LANE, t, LANE), out_dtype)
        out_spec = pl.BlockSpec((tn // LANE, tm, LANE), lambda i, j: (j, i, 0))
    else:
        out_shape = jax.ShapeDtypeStruct((t, n), out_dtype)
        out_spec = pl.BlockSpec((tm, tn), lambda i, j: (i, j))
    return pl.pallas_call(
        functools.partial(_proj_kernel, head_major=head_major, n_scaled_blocks=n_scaled // tn),
        grid=(t // tm, n // tn),
        in_specs=[pl.BlockSpec((tm, k), lambda i, j: (i, 0)),
                  pl.BlockSpec((None, tn, k), lambda i, j: (layer, col_block(j), 0))],
        out_specs=out_spec,
        out_shape=out_shape,
        compiler_params=_cparams(("parallel", "parallel")),
        name="proj",
    )(x, w)


def _proj_t_kernel(x_ref, w_ref, o_ref):
    r = _dot_nt(x_ref[...], w_ref[...].astype(BF16))
    ones = _ones_rows(r.shape[0], o_ref.dtype)
    for s in range(o_ref.shape[0]):
        o_ref[s, 0:HEAD_DIM, :] = r[:, s * LANE:(s + 1) * LANE].T.astype(o_ref.dtype)
        o_ref[s, HEAD_DIM:, :] = ones


def _proj_t(x, w, layer, *, n_out, tn=PROJ_TILE[1], col_block=_identity):
    t, k = x.shape
    n = n_out
    tm = min(PROJ_TILE[0], t)
    tn = min(tn, n)
    return pl.pallas_call(
        _proj_t_kernel,
        grid=(t // tm, n // tn),
        in_specs=[pl.BlockSpec((tm, k), lambda i, j: (i, 0)),
                  pl.BlockSpec((None, tn, k), lambda i, j: (layer, col_block(j), 0))],
        out_specs=pl.BlockSpec((tn // LANE, V_ROWS, tm), lambda i, j: (j, 0, i)),
        out_shape=jax.ShapeDtypeStruct((n // LANE, V_ROWS, t), BF16),
        compiler_params=_cparams(("parallel", "parallel")),
        name="proj_t",
    )(x, w)


def _mm_res_kernel(x_ref, w_ref, r_ref, o_ref):
    o_ref[...] = r_ref[...] + _dot(x_ref[...], w_ref[...])


def _mm_res(x, w, layer, res, *, tile, x_spec=pl.BlockSpec):
    t, k = x.shape
    n = w.shape[2]
    tm, tn = min(tile[0], t), tile[1]
    return pl.pallas_call(
        _mm_res_kernel,
        grid=(t // tm, n // tn),
        in_specs=[x_spec((tm, k), lambda i, j: (i, 0)),
                  pl.BlockSpec((None, k, tn), lambda i, j: (layer, 0, j)),
                  pl.BlockSpec((tm, tn), lambda i, j: (i, j))],
        out_specs=pl.BlockSpec((tm, tn), lambda i, j: (i, j)),
        out_shape=jax.ShapeDtypeStruct((t, n), F32),
        compiler_params=_cparams(("parallel", "parallel")),
        name="mm_res",
    )(x, w, res)


def _gateup_kernel(x_ref, wg_ref, wu_ref, o_ref):
    x = x_ref[...]
    g = _dot(x, wg_ref[...].astype(BF16))
    u = _dot(x, wu_ref[...].astype(BF16))
    o_ref[...] = (g * jax.nn.sigmoid(g) * u).astype(o_ref.dtype)


def _gateup(x, wg, wu, layer):
    t, k = x.shape
    n = wg.shape[2]
    tm, tn = min(GATEUP_TILE[0], t), GATEUP_TILE[1]
    return pl.pallas_call(
        _gateup_kernel,
        grid=(t // tm, n // tn),
        in_specs=[_row_block((tm, k), lambda i, j: (i, 0)),
                  pl.BlockSpec((None, k, tn), lambda i, j: (layer, 0, j)),
                  pl.BlockSpec((None, k, tn), lambda i, j: (layer, 0, j))],
        out_specs=pl.BlockSpec((tm, tn), lambda i, j: (i, j)),
        out_shape=jax.ShapeDtypeStruct((t, n), BF16),
        compiler_params=_cparams(("parallel", "parallel")),
        name="gateup",
    )(x, wg, wu)


def _ffn(h, g_norm, w_gate, w_up, w_down, layer):
    a = _rmsnorm(h, g_norm, BF16)
    h1 = _gateup(a, w_gate, w_up, layer)
    return _mm_res(h1, w_down, layer, h, tile=DOWN_TILE, x_spec=_row_block)


def _ones_rows(n_cols, dtype):
    row = lax.broadcasted_iota(jnp.int32, (V_ROWS - HEAD_DIM, n_cols), 0)
    return jnp.where(row == 0, 1.0, 0.0).astype(dtype)


def _flash_init(m_ref, acc_ref):
    m_ref[...] = jnp.full(m_ref.shape, NEG_INF, F32)
    acc_ref[...] = jnp.zeros(acc_ref.shape, F32)


def _flash_update(s_t, v_t, m_ref, acc_ref):
    m_prev = m_ref[...]
    m_new = jnp.maximum(m_prev, jnp.max(s_t, axis=0, keepdims=True))
    alpha = jnp.exp2(m_prev - m_new)
    p = jnp.exp2(s_t - m_new)
    acc_ref[...] = alpha * acc_ref[...] + _dot(v_t, p.astype(BF16))
    m_ref[...] = m_new


def _flash_update_from(s_slot, v_t, m_ref, acc_ref):
    n_keys = s_slot.shape[0]
    m_prev = m_ref[...]
    m_new = jnp.maximum(m_prev, jnp.max(s_slot[...], axis=0, keepdims=True))
    acc = jnp.exp2(m_prev - m_new) * acc_ref[...]
    for r in range(0, n_keys, KEY_BLOCK):
        p = jnp.exp2(s_slot[r:r + KEY_BLOCK, :] - m_new)
        acc = acc + _dot(v_t[:, r:r + KEY_BLOCK], p.astype(BF16))
    acc_ref[...] = acc
    m_ref[...] = m_new


def _softmax_tiles(tiles):
    mx = functools.reduce(jnp.maximum, [jnp.max(s_t, axis=0, keepdims=True) for s_t, _ in tiles])
    acc = None
    for s_t, v_t in tiles:
        a = _dot(v_t, jnp.exp2(s_t - mx).astype(BF16))
        acc = a if acc is None else acc + a
    return mx, acc


def _normalized(acc):
    return acc[0:HEAD_DIM] / acc[HEAD_DIM:HEAD_DIM + 1]


def _split3(x):
    hi = x.astype(BF16)
    r = x - hi.astype(F32)
    mid = r.astype(BF16)
    lo = (r - mid.astype(F32)).astype(BF16)
    return hi, mid, lo


def _cum_kernel(f_ref, b_ref, tri_ref, place_ref, qx_ref, kx_ref, carry_ref):
    @pl.when(pl.program_id(0) == 0)
    def _():
        carry_ref[...] = jnp.zeros_like(carry_ref)

    head_lane = lax.broadcasted_iota(jnp.int32, f_ref.shape, 1) < N_HEADS
    x = jnp.where(head_lane, f_ref[...] + b_ref[...], 0.0)
    log_f = jnp.minimum(x, 0.0) - jnp.log1p(jnp.exp(-jnp.abs(x)))
    c = jnp.dot(tri_ref[...], log_f, precision=lax.Precision.HIGHEST,
                preferred_element_type=F32) + carry_ref[...]
    carry_ref[...] = c[c.shape[0] - 1:, :]
    pieces = jnp.concatenate(_split3(c * LOG2E), axis=1)
    lane = lax.broadcasted_iota(jnp.int32, (c.shape[0], LANE), 1)
    q_const = jnp.where((lane >= N_PIECES) & (lane < 2 * N_PIECES), -1.0, 0.0)
    k_const = jnp.where(lane < N_PIECES, 1.0, 0.0)
    for h in range(N_HEADS):
        placed = _dot(pieces, place_ref[h])
        qx_ref[h] = (placed[:, :LANE] + q_const).astype(BF16)
        kx_ref[h] = (placed[:, LANE:] + k_const).astype(BF16)


def _forget_extras(f, b):
    t, n = f.shape
    tb = min(CUM_ROWS, t)
    tri = jnp.tril(jnp.ones((tb, tb), F32))
    h = jnp.arange(N_HEADS)[:, None, None]
    row = jnp.arange(N_PIECES * LANE)[None, :, None]
    col = jnp.arange(2 * LANE)[None, None, :]
    piece, head = row // LANE, row % LANE
    place = ((head == h) & ((col == piece) | (col == LANE + N_PIECES + piece))).astype(BF16)
    out = jax.ShapeDtypeStruct((N_HEADS, t, LANE), BF16)
    return pl.pallas_call(
        _cum_kernel,
        grid=(t // tb,),
        in_specs=[pl.BlockSpec((tb, n), lambda i: (i, 0)),
                  pl.BlockSpec((1, n), lambda i: (0, 0)),
                  pl.BlockSpec((tb, tb), lambda i: (0, 0)),
                  pl.BlockSpec((N_HEADS, N_PIECES * LANE, 2 * LANE), lambda i: (0, 0, 0))],
        out_specs=[pl.BlockSpec((N_HEADS, tb, LANE), lambda i: (0, i, 0)),
                   pl.BlockSpec((N_HEADS, tb, LANE), lambda i: (0, i, 0))],
        out_shape=[out, out],
        scratch_shapes=[pltpu.VMEM((1, n), F32)],
        compiler_params=_cparams(("arbitrary",)),
        name="forget_extras",
    )(f, b, tri, place)


def _pipelined_loop(first, trips, per_trip, scores, consume, s_ref):
    def body(t, carry):
        c = first + t * per_trip
        for u in range(per_trip):
            s_next = scores(c + u + 1)
            consume(c + u, s_ref.at[u % 2])
            s_ref[(u + 1) % 2] = s_next
        return carry

    lax.fori_loop(0, trips, body, 0)


def _pipelined_chunks(n_chunks, scores, consume, s_ref):
    quads = n_chunks // 4
    _pipelined_loop(0, quads, 4, scores, consume, s_ref)
    _pipelined_loop(4 * quads, (n_chunks % 4) // 2, 2, scores, consume, s_ref)


def _fox_kernel(q_ref, qx_ref, k_ref, kx_ref, vt_ref, o_ref, ka_ref, qa_ref, s_ref, m_ref, acc_ref, *, tq, tk):
    i = pl.program_id(1)
    assert tq == 2 * tk

    @pl.when(i == 0)
    def _():
        ka_ref[:, 0:HEAD_DIM] = k_ref[...]
        ka_ref[:, HEAD_DIM:] = kx_ref[...]

    qa_ref[:, 0:HEAD_DIM] = q_ref[...]
    qa_ref[:, HEAD_DIM:] = qx_ref[...]
    _flash_init(m_ref, acc_ref)

    def scores(c):
        off = pl.multiple_of(c * tk, tk)
        return _dot_nt(ka_ref[pl.ds(off, tk), :], qa_ref[...])

    def values(c):
        return vt_ref[:, pl.ds(pl.multiple_of(c * tk, tk), tk)]

    def consume(c, s_slot):
        _flash_update_from(s_slot, values(c), m_ref, acc_ref)

    n_full = 2 * i
    s_ref[0] = scores(0)
    _pipelined_chunks(n_full, scores, consume, s_ref)
    key = lax.broadcasted_iota(jnp.int32, (tk, tq), 0)
    qry = lax.broadcasted_iota(jnp.int32, (tk, tq), 1)
    s_last = scores(n_full + 1)
    _flash_update(jnp.where(key <= qry, s_ref[0], NEG_INF), values(n_full), m_ref, acc_ref)
    _flash_update(jnp.where(key + tk <= qry, s_last, NEG_INF), values(n_full + 1), m_ref, acc_ref)
    o_ref[...] = _normalized(acc_ref[...]).T.astype(o_ref.dtype)


def _fox_attention(qkh, qx, kx, vt):
    t = qkh.shape[1]
    tq = min(FOX_Q_TILE, t)
    tk = tq // 2
    return pl.pallas_call(
        functools.partial(_fox_kernel, tq=tq, tk=tk),
        grid=(N_HEADS, t // tq),
        in_specs=[pl.BlockSpec((None, tq, HEAD_DIM), lambda h, i: (h, i, 0)),
                  pl.BlockSpec((None, tq, LANE), lambda h, i: (h, i, 0)),
                  pl.BlockSpec((None, t, HEAD_DIM), lambda h, i: (N_HEADS + h, 0, 0)),
                  pl.BlockSpec((None, t, LANE), lambda h, i: (h, 0, 0)),
                  pl.BlockSpec((None, V_ROWS, t), lambda h, i: (h, 0, 0))],
        out_specs=pl.BlockSpec((tq, HEAD_DIM), lambda h, i: (i, h)),
        out_shape=jax.ShapeDtypeStruct((t, D_MODEL), BF16),
        scratch_shapes=[pltpu.VMEM((t, 2 * HEAD_DIM), BF16),
                        pltpu.VMEM((tq, 2 * HEAD_DIM), BF16), pltpu.VMEM((2, tk, tq), F32),
                        pltpu.VMEM((1, tq), F32), pltpu.VMEM((V_ROWS, tq), F32)],
        compiler_params=_cparams(("parallel", "arbitrary")),
        name="fox_attention",
    )(qkh, qx, qkh, kx, vt)


def _fox_layer(h, g_norm, w_in, b_f, w_o, layer):
    a = _rmsnorm(h, g_norm, BF16)
    tn = PROJ_TILE[1]
    qkh = _proj(a, w_in, layer, n_out=2 * D_MODEL, out_dtype=BF16, head_major=True, tn=tn,
                n_scaled=D_MODEL)
    vt = _proj_t(a, w_in, layer, n_out=D_MODEL, tn=tn, col_block=lambda j: 2 * D_MODEL // tn + j)
    f = _proj(a, w_in, layer, n_out=LANE, out_dtype=F32, head_major=False, tn=LANE,
              col_block=lambda j: 3 * D_MODEL // LANE + j)
    b = jnp.pad(b_f.astype(F32), (0, LANE - N_HEADS)).reshape(1, LANE)
    qx, kx = _forget_extras(f, b)
    o = _fox_attention(qkh, qx, kx, vt)
    return _mm_res(o, w_o, layer, h, tile=WO_TILE)


def _rel_bucket_const(dist):
    n = jnp.maximum(dist, 0)
    max_exact = REL_BUCKETS // 2
    nf = jnp.maximum(n, max_exact).astype(F32)
    large = max_exact + (jnp.log(nf / max_exact) / math.log(REL_MAX_DIST / max_exact)
                         * (REL_BUCKETS - max_exact)).astype(jnp.int32)
    large = jnp.minimum(large, REL_BUCKETS - 1)
    return jnp.where(n < max_exact, n, large)


def _bias_kernel(tab_ref, bkt_ref, o_ref):
    h = pl.program_id(0)
    bkt = bkt_ref[...]
    acc = jnp.zeros(bkt.shape, F32)
    for b in range(REL_BUCKETS):
        acc = jnp.where(bkt == b, tab_ref[b, h], acc)
    o_ref[...] = (acc - tab_ref[REL_BUCKETS - 1, h]) * LOG2E


def _bias_template(rel_table, bkt):
    r, c = bkt.shape
    return pl.pallas_call(
        _bias_kernel,
        grid=(N_HEADS,),
        in_specs=[pl.BlockSpec(memory_space=pltpu.SMEM),
                  pl.BlockSpec((r, c), lambda h: (0, 0))],
        out_specs=pl.BlockSpec((r, c), lambda h: (0, h)),
        out_shape=jax.ShapeDtypeStruct((r, N_HEADS * c), F32),
        compiler_params=_cparams(("arbitrary",)),
        name="bias_template",
    )(rel_table, bkt)


def _nsa_bias_templates(rel_table):
    tl = jnp.arange(Q_BLOCK)[None, :]
    bd = _bias_template(rel_table, _rel_bucket_const(tl + Q_BLOCK - jnp.arange(2 * Q_BLOCK)[:, None]))
    m = jnp.arange(CMP_NEAR)[:, None]
    bc = _bias_template(rel_table, _rel_bucket_const(
        tl - CMP_STRIDE * (m - (CMP_NEAR - 8)) - (CMP_BLOCK - 1)))
    return bd, bc


def _gelu_tanh(x):
    return 0.5 * x * (1.0 + jnp.tanh(math.sqrt(2.0 / math.pi) * (x + 0.044715 * (x * x * x))))


def _cmp_kernel(kb_ref, pos_ref, w1_ref, w2_ref, o_ref, *, nb):
    half = CMP_STRIDE * HEAD_DIM
    kb = kb_ref[...].astype(F32)
    xa = (kb + pos_ref[0:1, :]).astype(BF16)
    xb = (kb + pos_ref[1:2, :]).astype(BF16)
    a = _dot(xa, w1_ref[0:half, :])
    b = _dot(xb, w1_ref[half:2 * half, :])
    pre = a + pltpu.roll(b, nb - 1, axis=0)
    out = _dot(_gelu_tanh(pre).astype(BF16), w2_ref[...])
    row = lax.broadcasted_iota(jnp.int32, out.shape, 0)
    out = jnp.where(row < nb - 1, out, 0.0)
    o_ref[0:CMP_PAD, :] = jnp.zeros((CMP_PAD, HEAD_DIM), F32)
    o_ref[CMP_PAD:CMP_PAD + nb, :] = out


def _compress(kvh, cmp_pos, cmp_w1, cmp_w2):
    t = kvh.shape[1]
    nb = t // CMP_STRIDE
    kb = kvh[:2 * NSA_GROUPS].reshape(2 * NSA_GROUPS, nb, CMP_STRIDE * HEAD_DIM)
    pos = cmp_pos.astype(F32).reshape(2, 2, CMP_STRIDE * HEAD_DIM)
    return pl.pallas_call(
        functools.partial(_cmp_kernel, nb=nb),
        grid=(2 * NSA_GROUPS,),
        in_specs=[pl.BlockSpec((None, nb, CMP_STRIDE * HEAD_DIM), lambda j: (j, 0, 0)),
                  pl.BlockSpec((None, 2, CMP_STRIDE * HEAD_DIM), lambda j: (j // NSA_GROUPS, 0, 0)),
                  pl.BlockSpec((None, CMP_BLOCK * HEAD_DIM, HEAD_DIM), lambda j: (j // NSA_GROUPS, 0, 0)),
                  pl.BlockSpec((None, HEAD_DIM, HEAD_DIM), lambda j: (j // NSA_GROUPS, 0, 0))],
        out_specs=pl.BlockSpec((None, CMP_PAD + nb, HEAD_DIM), lambda j: (j, 0, 0)),
        out_shape=jax.ShapeDtypeStruct((2 * NSA_GROUPS, CMP_PAD + nb, HEAD_DIM), F32),
        compiler_params=_cparams(("parallel",)),
        name="compress",
    )(kb, pos, cmp_w1.astype(BF16), cmp_w2.astype(BF16))


def _dot_split_rhs(w, x):
    hi, mid, lo = _split3(x)
    return _dot(w, hi) + _dot(w, mid) + _dot(w, lo)


def _nsa_kernel(q_ref, g_ref, kc_ref, vc_ref, vct_ref, ksa_ref, vst_ref, kw_ref, vwt_ref,
                bd_ref, bc_ref, ovl_ref, o_ref,
                qa_ref, s_ref, sc_ref, imp_ref, gate_ref, m_ref, acc_ref, out_ref):
    i = pl.program_id(1)
    rows = NSA_ROWS
    q = q_ref[...].reshape(rows, HEAD_DIM)
    tl_lane = lax.broadcasted_iota(jnp.int32, (Q_BLOCK, rows), 1) % Q_BLOCK
    key_row = lax.broadcasted_iota(jnp.int32, (Q_BLOCK, rows), 0)
    gate_ref[...] = jax.nn.sigmoid(g_ref[...]).T
    gate_row0 = 3 * NSA_HG * pl.program_id(0)

    def gate(branch):
        return jnp.concatenate([gate_ref[pl.ds(gate_row0 + 3 * hg + branch, 1), :] for hg in range(NSA_HG)], axis=1)

    def head_sum(p):
        acc = p[:, 0:Q_BLOCK]
        for hg in range(1, NSA_HG):
            acc = acc + p[:, hg * Q_BLOCK:(hg + 1) * Q_BLOCK]
        return acc

    n_first_near = 8 * i - (CMP_NEAR - 8)
    n_blocks = (jnp.maximum(n_first_near, 0) + CMP_ROWS - 1) // CMP_ROWS

    def far_rows(b):
        return pl.multiple_of(b * CMP_ROWS, CMP_ROWS)

    def scores_pass(b, mx):
        r0 = far_rows(b)
        kb = kc_ref[pl.ds(CMP_PAD + r0, CMP_ROWS), :].astype(BF16)
        row = r0 + lax.broadcasted_iota(jnp.int32, (CMP_ROWS, rows), 0)
        s = jnp.where(row < n_first_near, _dot_nt(kb, q), NEG_INF)
        sc_ref[pl.ds(r0, CMP_ROWS), :] = s
        return jnp.maximum(mx, jnp.max(s, axis=0, keepdims=True))

    mx_far = lax.fori_loop(0, n_blocks, scores_pass, jnp.full((1, rows), NEG_INF, F32))
    start = pl.multiple_of(8 * i + 8 + CMP_PAD - CMP_NEAR, 8)
    kcn = kc_ref[pl.ds(start, CMP_NEAR), :].astype(BF16)
    vcn_t = vc_ref[pl.ds(start, CMP_NEAR), :].T.astype(BF16)
    d_near = tl_lane - CMP_STRIDE * (key_row - (CMP_NEAR - 8)) - (CMP_BLOCK - 1)
    near_ok = (d_near >= 0) & (n_first_near + key_row >= 0)
    s_near = jnp.where(near_ok, _dot_nt(kcn, q) + bc_ref[...], NEG_INF)
    mx = jnp.maximum(mx_far, jnp.max(s_near, axis=0, keepdims=True))
    e_near = jnp.exp2(s_near - mx)

    def exp_pass(b, den):
        r0 = far_rows(b)
        e = jnp.exp2(sc_ref[pl.ds(r0, CMP_ROWS), :] - mx)
        sc_ref[pl.ds(r0, CMP_ROWS), :] = e
        return den + jnp.sum(e, axis=0, keepdims=True)

    den = lax.fori_loop(0, n_blocks, exp_pass, jnp.sum(e_near, axis=0, keepdims=True))
    inv = jnp.where(mx > 0.5 * NEG_INF, 1.0 / den, 0.0)
    p_near = e_near * inv
    sb = lax.broadcasted_iota(jnp.int32, (SEL_COLS, CMP_NEAR), 0)
    nn = n_first_near + lax.broadcasted_iota(jnp.int32, (SEL_COLS, CMP_NEAR), 1)
    ovl_near = jnp.where((nn >= 4 * sb - 1) & (nn <= 4 * sb + 3) & (nn >= 0), 1.0, 0.0).astype(BF16)
    out_ref[...] = _dot(vcn_t, p_near.astype(BF16))
    imp_ref[...] = _dot_split_rhs(ovl_near, head_sum(p_near))

    def out_pass(b, carry):
        r0 = far_rows(b)
        p = sc_ref[pl.ds(r0, CMP_ROWS), :] * inv
        out_ref[...] += _dot(vct_ref[:, pl.ds(CMP_PAD + r0, CMP_ROWS)].astype(BF16), p.astype(BF16))
        imp_ref[...] += _dot_split_rhs(ovl_ref[:, pl.ds(r0, CMP_ROWS)], head_sum(p))
        return carry

    lax.fori_loop(0, n_blocks, out_pass, 0)
    out_ref[...] = gate(0) * out_ref[...]

    imp = imp_ref[...]
    blk = lax.broadcasted_iota(jnp.int32, (SEL_COLS, Q_BLOCK), 0)
    tl = lax.broadcasted_iota(jnp.int32, (SEL_COLS, Q_BLOCK), 1)
    cur = 2 * i + (tl >= SEL_BLOCK).astype(jnp.int32)
    forced = (blk == 0) | (blk == cur) | (blk == cur - 1)
    causal_blk = blk * SEL_BLOCK <= Q_BLOCK * i + tl
    work = jnp.where(forced, -1.0, jnp.where(causal_blk, imp, -1.0))
    blk_f = blk.astype(F32)
    sel = jnp.where(forced, 1.0, 0.0)
    for _ in range(SEL_TOPK - 3):
        best = jnp.max(work, axis=0, keepdims=True)
        first = jnp.min(jnp.where(work == best, blk_f, float(SEL_COLS)), axis=0, keepdims=True)
        pick = blk_f == first
        sel = jnp.where(pick, 1.0, sel)
        work = jnp.where(pick, -2.0, work)
    amask = jnp.where(sel > 0.0, 0.0, NEG_INF)
    for half in range(SEL_COLS // LANE):
        a_t = amask[half * LANE:(half + 1) * LANE, :].T.astype(BF16)
        qa_ref[half, :, 0:HEAD_DIM] = q
        qa_ref[half, :, HEAD_DIM:2 * HEAD_DIM] = jnp.concatenate([a_t] * NSA_HG, axis=0)

    def sel_chunk(tile, n_keys):
        off = pl.multiple_of(tile * Q_BLOCK, Q_BLOCK)
        qa = qa_ref[tile // (LANE // 2)]
        return _dot_nt(ksa_ref[pl.ds(off, n_keys), :], qa), vst_ref[:, pl.ds(off, n_keys)]

    s_t, v_t = sel_chunk(i, Q_BLOCK)
    near = [(jnp.where(key_row <= tl_lane, s_t + bd_ref[Q_BLOCK:, :], NEG_INF), v_t)]
    s_t, v_t = sel_chunk(jnp.maximum(i - 1, 0), Q_BLOCK)
    near.append((jnp.where(i >= 1, s_t + bd_ref[:Q_BLOCK, :], NEG_INF), v_t))
    m_ref[...], acc_ref[...] = _softmax_tiles(near)
    n_far = jnp.maximum(i - 1, 0)
    n_chunks = n_far // FAR_TILES
    chunk_keys = FAR_TILES * Q_BLOCK

    def far_scores(c):
        tile = FAR_TILES * jnp.minimum(c, jnp.maximum(n_chunks - 1, 0))
        off = pl.multiple_of(tile * Q_BLOCK, chunk_keys)
        return _dot_nt(ksa_ref[pl.ds(off, chunk_keys), :], qa_ref[tile // (LANE // 2)])

    def far_consume(c, s_slot):
        off = pl.multiple_of(c * chunk_keys, chunk_keys)
        _flash_update_from(s_slot, vst_ref[:, pl.ds(off, chunk_keys)], m_ref, acc_ref)

    s_ref[0] = far_scores(0)
    _pipelined_chunks(n_chunks, far_scores, far_consume, s_ref)

    @pl.when(n_chunks % 2 == 1)
    def _():
        far_consume(n_chunks - 1, s_ref.at[0])

    left = n_far % FAR_TILES

    @pl.when(left >= 2)
    def _():
        s_t, v_t = sel_chunk(n_far - left, 2 * Q_BLOCK)
        _flash_update(s_t, v_t, m_ref, acc_ref)

    @pl.when(left % 2 == 1)
    def _():
        s_t, v_t = sel_chunk(n_far - 1, Q_BLOCK)
        _flash_update(s_t, v_t, m_ref, acc_ref)

    out_ref[...] += gate(1) * _normalized(acc_ref[...])

    def win_chunk(tile):
        off = pl.multiple_of(jnp.maximum(tile, 0) * Q_BLOCK, Q_BLOCK)
        return _dot_nt(kw_ref[pl.ds(off, Q_BLOCK), :], q), vwt_ref[:, pl.ds(off, Q_BLOCK)]

    s_t, v_t = win_chunk(i)
    win = [(jnp.where(key_row <= tl_lane, s_t + bd_ref[Q_BLOCK:, :], NEG_INF), v_t)]
    s_t, v_t = win_chunk(i - 1)
    win.append((jnp.where(i >= 1, s_t + bd_ref[:Q_BLOCK, :], NEG_INF), v_t))
    for back in (2, 3):
        s_t, v_t = win_chunk(i - back)
        win.append((jnp.where(i >= back, s_t, NEG_INF), v_t))
    s_t, v_t = win_chunk(i - 4)
    win.append((jnp.where((key_row > tl_lane) & (i >= 4), s_t, NEG_INF), v_t))
    _, acc_w = _softmax_tiles(win)
    o = out_ref[...] + gate(2) * _normalized(acc_w)
    for hg in range(NSA_HG):
        o_ref[:, hg * HEAD_DIM:(hg + 1) * HEAD_DIM] = o[:, hg * Q_BLOCK:(hg + 1) * Q_BLOCK].T.astype(o_ref.dtype)


def _nsa_attention(qkh, gates, kcv, kcv_t, ksa, vt, bd, bc, ovl):
    t = qkh.shape[1]
    ncp = t // CMP_STRIDE
    assert ncp % CMP_ROWS == 0
    resident = functools.partial(pl.BlockSpec, pipeline_mode=pl.Buffered(1))
    return pl.pallas_call(
        _nsa_kernel,
        grid=(NSA_GROUPS, t // Q_BLOCK),
        in_specs=[pl.BlockSpec((NSA_HG, Q_BLOCK, HEAD_DIM), lambda g, i: (g, i, 0)),
                  pl.BlockSpec((Q_BLOCK, LANE), lambda g, i: (i, 0)),
                  resident((None, CMP_PAD + ncp, HEAD_DIM), lambda g, i: (g, 0, 0)),
                  resident((None, CMP_PAD + ncp, HEAD_DIM), lambda g, i: (NSA_GROUPS + g, 0, 0)),
                  resident((None, HEAD_DIM, CMP_PAD + ncp), lambda g, i: (NSA_GROUPS + g, 0, 0)),
                  resident((None, t, 2 * HEAD_DIM), lambda g, i: (g, 0, 0)),
                  resident((None, V_ROWS, t), lambda g, i: (g, 0, 0)),
                  resident((None, t, HEAD_DIM), lambda g, i: (N_HEADS + 3 * NSA_GROUPS + g, 0, 0)),
                  resident((None, V_ROWS, t), lambda g, i: (NSA_GROUPS + g, 0, 0)),
                  resident((2 * Q_BLOCK, NSA_ROWS), lambda g, i: (0, g)),
                  resident((CMP_NEAR, NSA_ROWS), lambda g, i: (0, g)),
                  resident((SEL_COLS, ncp), lambda g, i: (0, 0))],
        out_specs=pl.BlockSpec((Q_BLOCK, NSA_HG * HEAD_DIM), lambda g, i: (i, g)),
        out_shape=jax.ShapeDtypeStruct((t, D_MODEL), BF16),
        scratch_shapes=[pltpu.VMEM((SEL_COLS // LANE, NSA_ROWS, 2 * HEAD_DIM), BF16),
                        pltpu.VMEM((2, FAR_TILES * Q_BLOCK, NSA_ROWS), F32),
                        pltpu.VMEM((ncp, NSA_ROWS), F32), pltpu.VMEM((SEL_COLS, Q_BLOCK), F32),
                        pltpu.VMEM((LANE, Q_BLOCK), F32), pltpu.VMEM((1, NSA_ROWS), F32),
                        pltpu.VMEM((V_ROWS, NSA_ROWS), F32), pltpu.VMEM((HEAD_DIM, NSA_ROWS), F32)],
        compiler_params=_cparams(("parallel", "arbitrary")),
        name="nsa_attention",
    )(qkh, gates, kcv, kcv, kcv_t, ksa, vt, qkh, vt, bd, bc, ovl)


def _nsa_layer(h, g_norm, w_in, w_o, layer, cmp_pos, cmp_w1, cmp_w2, bd, bc):
    t = h.shape[0]
    assert t // SEL_BLOCK <= SEL_COLS
    a = _rmsnorm(h, g_norm, BF16)
    tn = NSA_KV_DIM
    q_blocks = D_MODEL // tn
    qkh = _proj(a, w_in, layer, n_out=D_MODEL + 4 * NSA_KV_DIM, out_dtype=BF16, head_major=True, tn=tn,
                n_scaled=D_MODEL,
                col_block=lambda j: j + jnp.where(j >= q_blocks + 3, 1, 0))
    vt = _proj_t(a, w_in, layer, n_out=2 * NSA_KV_DIM, tn=tn, col_block=lambda j: q_blocks + 3 + 2 * j)
    gates = _proj(a, w_in, layer, n_out=LANE, out_dtype=F32, head_major=False, tn=LANE,
                  col_block=lambda j: (D_MODEL + 6 * NSA_KV_DIM) // LANE + j)
    kcv = _compress(qkh[N_HEADS:N_HEADS + 2 * NSA_GROUPS], cmp_pos, cmp_w1, cmp_w2)
    kcv_t = jnp.swapaxes(kcv, 1, 2)
    key_blk = (jnp.arange(t) // SEL_BLOCK) % LANE
    onehot = (key_blk[:, None] == jnp.arange(LANE)[None, :]).astype(BF16)
    ksa = jnp.concatenate([qkh[N_HEADS + 2 * NSA_GROUPS:N_HEADS + 3 * NSA_GROUPS],
                           jnp.broadcast_to(onehot, (NSA_GROUPS, t, LANE))], axis=-1)
    n = jnp.arange(t // CMP_STRIDE)[None, :]
    sblk = jnp.arange(SEL_COLS)[:, None]
    ovl = ((n >= 4 * sblk - 1) & (n <= 4 * sblk + 3)).astype(BF16)
    o = _nsa_attention(qkh, gates, kcv, kcv_t, ksa, vt, bd, bc, ovl)
    return _mm_res(o, w_o, layer, h, tile=WO_TILE)


def kernel(x, norm_mix, norm_ffn, norm_final, rel_table, nsa_w_in, nsa_w_o, nsa_cmp_pos, nsa_cmp_w1,
           nsa_cmp_w2, fox_w_in, fox_b_f, fox_w_o, ffn_w_gate, ffn_w_up, ffn_w_down):
    depth = norm_mix.shape[0]
    bd, bc = _nsa_bias_templates(rel_table.astype(F32))
    nsa_w_in, fox_w_in = jnp.swapaxes(nsa_w_in.astype(F32), 1, 2), jnp.swapaxes(fox_w_in.astype(F32), 1, 2)
    ffn_w_gate, ffn_w_up = ffn_w_gate.astype(F32), ffn_w_up.astype(F32)
    nsa_w_o, fox_w_o, ffn_w_down = nsa_w_o.astype(BF16), fox_w_o.astype(BF16), ffn_w_down.astype(BF16)
    outs = []
    for bi in range(x.shape[0]):
        h = x[bi]
        for i in range(depth):
            j = i // 2
            if i % 2 == 0:
                h = _nsa_layer(h, norm_mix[i], nsa_w_in, nsa_w_o, j, nsa_cmp_pos[j], nsa_cmp_w1[j],
                               nsa_cmp_w2[j], bd, bc)
            else:
                h = _fox_layer(h, norm_mix[i], fox_w_in, fox_b_f[j], fox_w_o, j)
            h = _ffn(h, norm_ffn[i], ffn_w_gate, ffn_w_up, ffn_w_down, i)
        outs.append(_rmsnorm(h, norm_final, F32))
    return jnp.stack(outs, axis=0)
```

```python
import functools
import math

import jax
import jax.numpy as jnp
from jax import lax
from jax.experimental import pallas as pl
from jax.experimental.pallas import tpu as pltpu

F32 = jnp.float32
BF16 = jnp.bfloat16

D_MODEL = 4096
HEAD_DIM = 128
N_HEADS = D_MODEL // HEAD_DIM
NSA_GROUPS = 4
NSA_HG = N_HEADS // NSA_GROUPS
NSA_KV_DIM = NSA_GROUPS * HEAD_DIM
CMP_BLOCK = 32
CMP_STRIDE = 16
SEL_BLOCK = 64
SEL_TOPK = 16
WINDOW = 512
REL_BUCKETS = 32
REL_MAX_DIST = 128
Q_BLOCK = 128
RMS_EPS = 1e-6
NEG_INF = -1e30
LOG2E = 1.4426950408889634
Q_SCALE = HEAD_DIM ** -0.5 * LOG2E

LANE = 128
SEL_COLS = 256
CMP_PAD = 128
CMP_NEAR = 128
CMP_ROWS = 256
FAR_TILES = 4
NSA_ROWS = NSA_HG * Q_BLOCK
V_ROWS = HEAD_DIM + 16
KEY_BLOCK = 256
N_PIECES = 3

RMS_ROWS = 256
PROJ_TILE = (1024, 512)
GATEUP_TILE = (2048, 256)
DOWN_TILE = (1024, 256)
WO_TILE = (1024, 512)
FOX_Q_TILE = 1024
CUM_ROWS = 512
VMEM_LIMIT = 56 * 1024 * 1024


def _cparams(sem):
    return pltpu.CompilerParams(dimension_semantics=sem, vmem_limit_bytes=VMEM_LIMIT)


def _dot(a, b):
    return jnp.dot(a, b, preferred_element_type=F32)


def _dot_nt(a, b):
    return lax.dot_general(a, b, (((1,), (1,)), ((), ())), preferred_element_type=F32)


def _rmsnorm_kernel(x_ref, g_ref, o_ref):
    x = x_ref[...]
    ms = jnp.mean(x * x, axis=-1, keepdims=True)
    o_ref[...] = (x * lax.rsqrt(ms + RMS_EPS) * g_ref[...]).astype(o_ref.dtype)


def _rmsnorm(x, g, out_dtype):
    t, d = x.shape
    tm = min(RMS_ROWS, t)
    return pl.pallas_call(
        _rmsnorm_kernel,
        grid=(t // tm,),
        in_specs=[pl.BlockSpec((tm, d), lambda i: (i, 0)),
                  pl.BlockSpec((1, d), lambda i: (0, 0))],
        out_specs=pl.BlockSpec((tm, d), lambda i: (i, 0)),
        out_shape=jax.ShapeDtypeStruct((t, d), out_dtype),
        compiler_params=_cparams(("parallel",)),
        name="rmsnorm",
    )(x, g.reshape(1, d))


def _proj_kernel(x_ref, w_ref, o_ref, *, head_major, n_scaled_blocks, valid_cols):
    r = _dot_nt(x_ref[...], w_ref[...].astype(BF16))
    if valid_cols is not None:
        r = jnp.where(lax.broadcasted_iota(jnp.int32, r.shape, 1) < valid_cols, r, 0.0)
    if n_scaled_blocks:
        r = r * jnp.where(pl.program_id(1) < n_scaled_blocks, Q_SCALE, 1.0)
    if head_major:
        for s in range(o_ref.shape[0]):
            o_ref[s] = r[:, s * LANE:(s + 1) * LANE].astype(o_ref.dtype)
    else:
        o_ref[...] = r.astype(o_ref.dtype)


def _identity(j):
    return j


_row_block = functools.partial(pl.BlockSpec, pipeline_mode=pl.Buffered(1))


def _proj(x, w, layer, *, n_out, out_dtype, head_major, tn=PROJ_TILE[1], col_block=_identity, n_scaled=0,
          valid_cols=None):
    t, k = x.shape
    n = n_out
    tm = min(PROJ_TILE[0], t)
    tn = min(tn, n)
    assert n_scaled % tn == 0 and n % tn == 0
    if head_major:
        out_shape = jax.ShapeDtypeStruct((n // LANE, t, LANE), out_dtype)
        out_spec = pl.BlockSpec((tn // LANE, tm, LANE), lambda i, j: (j, i, 0))
    else:
        out_shape = jax.ShapeDtypeStruct((t, n), out_dtype)
        out_spec = pl.BlockSpec((tm, tn), lambda i, j: (i, j))
    return pl.pallas_call(
        functools.partial(_proj_kernel, head_major=head_major, n_scaled_blocks=n_scaled // tn,
                          valid_cols=valid_cols),
        grid=(t // tm, n // tn),
        in_specs=[pl.BlockSpec((tm, k), lambda i, j: (i, 0)),
                  pl.BlockSpec((None, tn, k), lambda i, j: (layer, col_block(j), 0))],
        out_specs=out_spec,
        out_shape=out_shape,
        compiler_params=_cparams(("parallel", "parallel")),
        name="proj",
    )(x, w)


def _proj_t_kernel(x_ref, w_ref, o_ref):
    r = _dot_nt(x_ref[...], w_ref[...].astype(BF16))
    ones = _ones_rows(r.shape[0], o_ref.dtype)
    for s in range(o_ref.shape[0]):
        o_ref[s, 0:HEAD_DIM, :] = r[:, s * LANE:(s + 1) * LANE].T.astype(o_ref.dtype)
        o_ref[s, HEAD_DIM:, :] = ones


def _proj_t(x, w, layer, *, n_out, tn=PROJ_TILE[1], col_block=_identity):
    t, k = x.shape
    n = n_out
    tm = min(PROJ_TILE[0], t)
    tn = min(tn, n)
    return pl.pallas_call(
        _proj_t_kernel,
        grid=(t // tm, n // tn),
        in_specs=[pl.BlockSpec((tm, k), lambda i, j: (i, 0)),
                  pl.BlockSpec((None, tn, k), lambda i, j: (layer, col_block(j), 0))],
        out_specs=pl.BlockSpec((tn // LANE, V_ROWS, tm), lambda i, j: (j, 0, i)),
        out_shape=jax.ShapeDtypeStruct((n // LANE, V_ROWS, t), BF16),
        compiler_params=_cparams(("parallel", "parallel")),
        name="proj_t",
    )(x, w)


def _mm_res_kernel(x_ref, w_ref, r_ref, o_ref):
    o_ref[...] = r_ref[...] + _dot(x_ref[...], w_ref[...])


def _mm_res(x, w, layer, res, *, tile, x_spec=pl.BlockSpec):
    t, k = x.shape
    n = w.shape[2]
    tm, tn = min(tile[0], t), tile[1]
    return pl.pallas_call(
        _mm_res_kernel,
        grid=(t // tm, n // tn),
        in_specs=[x_spec((tm, k), lambda i, j: (i, 0)),
                  pl.BlockSpec((None, k, tn), lambda i, j: (layer, 0, j)),
                  pl.BlockSpec((tm, tn), lambda i, j: (i, j))],
        out_specs=pl.BlockSpec((tm, tn), lambda i, j: (i, j)),
        out_shape=jax.ShapeDtypeStruct((t, n), F32),
        compiler_params=_cparams(("parallel", "parallel")),
        name="mm_res",
    )(x, w, res)


def _gateup_kernel(x_ref, wg_ref, wu_ref, o_ref):
    x = x_ref[...]
    g = _dot(x, wg_ref[...].astype(BF16))
    u = _dot(x, wu_ref[...].astype(BF16))
    o_ref[...] = (g * jax.nn.sigmoid(g) * u).astype(o_ref.dtype)


def _gateup(x, wg, wu, layer):
    t, k = x.shape
    n = wg.shape[2]
    tm, tn = min(GATEUP_TILE[0], t), GATEUP_TILE[1]
    return pl.pallas_call(
        _gateup_kernel,
        grid=(t // tm, n // tn),
        in_specs=[_row_block((tm, k), lambda i, j: (i, 0)),
                  pl.BlockSpec((None, k, tn), lambda i, j: (layer, 0, j)),
                  pl.BlockSpec((None, k, tn), lambda i, j: (layer, 0, j))],
        out_specs=pl.BlockSpec((tm, tn), lambda i, j: (i, j)),
        out_shape=jax.ShapeDtypeStruct((t, n), BF16),
        compiler_params=_cparams(("parallel", "parallel")),
        name="gateup",
    )(x, wg, wu)


def _ffn(h, g_norm, w_gate, w_up, w_down, layer):
    a = _rmsnorm(h, g_norm, BF16)
    h1 = _gateup(a, w_gate, w_up, layer)
    return _mm_res(h1, w_down, layer, h, tile=DOWN_TILE, x_spec=_row_block)


def _ones_rows(n_cols, dtype):
    row = lax.broadcasted_iota(jnp.int32, (V_ROWS - HEAD_DIM, n_cols), 0)
    return jnp.where(row == 0, 1.0, 0.0).astype(dtype)


def _flash_init(m_ref, acc_ref):
    m_ref[...] = jnp.full(m_ref.shape, NEG_INF, F32)
    acc_ref[...] = jnp.zeros(acc_ref.shape, F32)


def _flash_update(s_t, v_t, m_ref, acc_ref):
    m_prev = m_ref[...]
    m_new = jnp.maximum(m_prev, jnp.max(s_t, axis=0, keepdims=True))
    alpha = jnp.exp2(m_prev - m_new)
    p = jnp.exp2(s_t - m_new)
    acc_ref[...] = alpha * acc_ref[...] + _dot(v_t, p.astype(BF16))
    m_ref[...] = m_new


def _flash_update_from(s_slot, v_t, m_ref, acc_ref):
    n_keys = s_slot.shape[0]
    m_prev = m_ref[...]
    m_new = jnp.maximum(m_prev, jnp.max(s_slot[...], axis=0, keepdims=True))
    acc = jnp.exp2(m_prev - m_new) * acc_ref[...]
    for r in range(0, n_keys, KEY_BLOCK):
        p = jnp.exp2(s_slot[r:r + KEY_BLOCK, :] - m_new)
        acc = acc + _dot(v_t[:, r:r + KEY_BLOCK], p.astype(BF16))
    acc_ref[...] = acc
    m_ref[...] = m_new


def _softmax_tiles(tiles):
    mx = functools.reduce(jnp.maximum, [jnp.max(s_t, axis=0, keepdims=True) for s_t, _ in tiles])
    acc = None
    for s_t, v_t in tiles:
        a = _dot(v_t, jnp.exp2(s_t - mx).astype(BF16))
        acc = a if acc is None else acc + a
    return mx, acc


def _normalized(acc):
    return acc[0:HEAD_DIM] / acc[HEAD_DIM:HEAD_DIM + 1]


def _split3(x):
    hi = x.astype(BF16)
    r = x - hi.astype(F32)
    mid = r.astype(BF16)
    lo = (r - mid.astype(F32)).astype(BF16)
    return hi, mid, lo


def _cum_kernel(f_ref, b_ref, tri_ref, place_ref, qx_ref, kx_ref, carry_ref):
    @pl.when(pl.program_id(0) == 0)
    def _():
        carry_ref[...] = jnp.zeros_like(carry_ref)

    head_lane = lax.broadcasted_iota(jnp.int32, f_ref.shape, 1) < N_HEADS
    x = jnp.where(head_lane, f_ref[...] + b_ref[...], 0.0)
    log_f = jnp.minimum(x, 0.0) - jnp.log1p(jnp.exp(-jnp.abs(x)))
    c = jnp.dot(tri_ref[...], log_f, precision=lax.Precision.HIGHEST,
                preferred_element_type=F32) + carry_ref[...]
    carry_ref[...] = c[c.shape[0] - 1:, :]
    pieces = jnp.concatenate(_split3(c * LOG2E), axis=1)
    lane = lax.broadcasted_iota(jnp.int32, (c.shape[0], LANE), 1)
    q_const = jnp.where((lane >= N_PIECES) & (lane < 2 * N_PIECES), -1.0, 0.0)
    k_const = jnp.where(lane < N_PIECES, 1.0, 0.0)
    for h in range(N_HEADS):
        placed = _dot(pieces, place_ref[h])
        qx_ref[h] = (placed[:, :LANE] + q_const).astype(BF16)
        kx_ref[h] = (placed[:, LANE:] + k_const).astype(BF16)


def _forget_extras(f, b):
    t, n = f.shape
    tb = min(CUM_ROWS, t)
    tri = jnp.tril(jnp.ones((tb, tb), F32))
    h = jnp.arange(N_HEADS)[:, None, None]
    row = jnp.arange(N_PIECES * LANE)[None, :, None]
    col = jnp.arange(2 * LANE)[None, None, :]
    piece, head = row // LANE, row % LANE
    place = ((head == h) & ((col == piece) | (col == LANE + N_PIECES + piece))).astype(BF16)
    out = jax.ShapeDtypeStruct((N_HEADS, t, LANE), BF16)
    return pl.pallas_call(
        _cum_kernel,
        grid=(t // tb,),
        in_specs=[pl.BlockSpec((tb, n), lambda i: (i, 0)),
                  pl.BlockSpec((1, n), lambda i: (0, 0)),
                  pl.BlockSpec((tb, tb), lambda i: (0, 0)),
                  pl.BlockSpec((N_HEADS, N_PIECES * LANE, 2 * LANE), lambda i: (0, 0, 0))],
        out_specs=[pl.BlockSpec((N_HEADS, tb, LANE), lambda i: (0, i, 0)),
                   pl.BlockSpec((N_HEADS, tb, LANE), lambda i: (0, i, 0))],
        out_shape=[out, out],
        scratch_shapes=[pltpu.VMEM((1, n), F32)],
        compiler_params=_cparams(("arbitrary",)),
        name="forget_extras",
    )(f, b, tri, place)


def _pipelined_loop(first, trips, per_trip, scores, consume, s_ref):
    def body(t, carry):
        c = first + t * per_trip
        for u in range(per_trip):
            s_next = scores(c + u + 1)
            consume(c + u, s_ref.at[u % 2])
            s_ref[(u + 1) % 2] = s_next
        return carry

    lax.fori_loop(0, trips, body, 0)


def _pipelined_chunks(n_chunks, scores, consume, s_ref):
    quads = n_chunks // 4
    _pipelined_loop(0, quads, 4, scores, consume, s_ref)
    _pipelined_loop(4 * quads, (n_chunks % 4) // 2, 2, scores, consume, s_ref)


def _fox_kernel(q_ref, qx_ref, k_ref, kx_ref, vt_ref, o_ref, ka_ref, qa_ref, s_ref, m_ref, acc_ref, *, tq, tk):
    i = pl.program_id(1)
    assert tq == 2 * tk

    @pl.when(i == 0)
    def _():
        ka_ref[:, 0:HEAD_DIM] = k_ref[...]
        ka_ref[:, HEAD_DIM:] = kx_ref[...]

    qa_ref[:, 0:HEAD_DIM] = q_ref[...]
    qa_ref[:, HEAD_DIM:] = qx_ref[...]
    _flash_init(m_ref, acc_ref)

    def scores(c):
        off = pl.multiple_of(c * tk, tk)
        return _dot_nt(ka_ref[pl.ds(off, tk), :], qa_ref[...])

    def values(c):
        return vt_ref[:, pl.ds(pl.multiple_of(c * tk, tk), tk)]

    def consume(c, s_slot):
        _flash_update_from(s_slot, values(c), m_ref, acc_ref)

    n_full = 2 * i
    s_ref[0] = scores(0)
    _pipelined_chunks(n_full, scores, consume, s_ref)
    key = lax.broadcasted_iota(jnp.int32, (tk, tq), 0)
    qry = lax.broadcasted_iota(jnp.int32, (tk, tq), 1)
    s_last = scores(n_full + 1)
    _flash_update(jnp.where(key <= qry, s_ref[0], NEG_INF), values(n_full), m_ref, acc_ref)
    _flash_update(jnp.where(key + tk <= qry, s_last, NEG_INF), values(n_full + 1), m_ref, acc_ref)
    o_ref[...] = _normalized(acc_ref[...]).T.astype(o_ref.dtype)


def _fox_attention(qkh, qx, kx, vt):
    t = qkh.shape[1]
    tq = min(FOX_Q_TILE, t)
    tk = tq // 2
    return pl.pallas_call(
        functools.partial(_fox_kernel, tq=tq, tk=tk),
        grid=(N_HEADS, t // tq),
        in_specs=[pl.BlockSpec((None, tq, HEAD_DIM), lambda h, i: (h, i, 0)),
                  pl.BlockSpec((None, tq, LANE), lambda h, i: (h, i, 0)),
                  pl.BlockSpec((None, t, HEAD_DIM), lambda h, i: (N_HEADS + h, 0, 0)),
                  pl.BlockSpec((None, t, LANE), lambda h, i: (h, 0, 0)),
                  pl.BlockSpec((None, V_ROWS, t), lambda h, i: (h, 0, 0))],
        out_specs=pl.BlockSpec((tq, HEAD_DIM), lambda h, i: (i, h)),
        out_shape=jax.ShapeDtypeStruct((t, D_MODEL), BF16),
        scratch_shapes=[pltpu.VMEM((t, 2 * HEAD_DIM), BF16),
                        pltpu.VMEM((tq, 2 * HEAD_DIM), BF16), pltpu.VMEM((2, tk, tq), F32),
                        pltpu.VMEM((1, tq), F32), pltpu.VMEM((V_ROWS, tq), F32)],
        compiler_params=_cparams(("parallel", "arbitrary")),
        name="fox_attention",
    )(qkh, qx, qkh, kx, vt)


def _fox_layer(h, g_norm, w_in, b_f, w_o, layer):
    a = _rmsnorm(h, g_norm, BF16)
    tn = PROJ_TILE[1]
    qkh = _proj(a, w_in, layer, n_out=2 * D_MODEL, out_dtype=BF16, head_major=True, tn=tn,
                n_scaled=D_MODEL)
    vt = _proj_t(a, w_in, layer, n_out=D_MODEL, tn=tn, col_block=lambda j: 2 * D_MODEL // tn + j)
    f = _proj(a, w_in, layer, n_out=LANE, out_dtype=F32, head_major=False, tn=LANE,
              col_block=lambda j: 3 * D_MODEL // LANE + j, valid_cols=N_HEADS)
    b = jnp.pad(b_f.astype(F32), (0, LANE - N_HEADS)).reshape(1, LANE)
    qx, kx = _forget_extras(f, b)
    o = _fox_attention(qkh, qx, kx, vt)
    return _mm_res(o, w_o, layer, h, tile=WO_TILE)


def _rel_bucket_const(dist):
    n = jnp.maximum(dist, 0)
    max_exact = REL_BUCKETS // 2
    nf = jnp.maximum(n, max_exact).astype(F32)
    large = max_exact + (jnp.log(nf / max_exact) / math.log(REL_MAX_DIST / max_exact)
                         * (REL_BUCKETS - max_exact)).astype(jnp.int32)
    large = jnp.minimum(large, REL_BUCKETS - 1)
    return jnp.where(n < max_exact, n, large)


def _bias_kernel(tab_ref, bkt_ref, o_ref):
    h = pl.program_id(0)
    bkt = bkt_ref[...]
    acc = jnp.zeros(bkt.shape, F32)
    for b in range(REL_BUCKETS):
        acc = jnp.where(bkt == b, tab_ref[b, h], acc)
    o_ref[...] = (acc - tab_ref[REL_BUCKETS - 1, h]) * LOG2E


def _bias_template(rel_table, bkt):
    r, c = bkt.shape
    return pl.pallas_call(
        _bias_kernel,
        grid=(N_HEADS,),
        in_specs=[pl.BlockSpec(memory_space=pltpu.SMEM),
                  pl.BlockSpec((r, c), lambda h: (0, 0))],
        out_specs=pl.BlockSpec((r, c), lambda h: (0, h)),
        out_shape=jax.ShapeDtypeStruct((r, N_HEADS * c), F32),
        compiler_params=_cparams(("arbitrary",)),
        name="bias_template",
    )(rel_table, bkt)


def _nsa_bias_templates(rel_table):
    tl = jnp.arange(Q_BLOCK)[None, :]
    bd = _bias_template(rel_table, _rel_bucket_const(tl + Q_BLOCK - jnp.arange(2 * Q_BLOCK)[:, None]))
    m = jnp.arange(CMP_NEAR)[:, None]
    bc = _bias_template(rel_table, _rel_bucket_const(
        tl - CMP_STRIDE * (m - (CMP_NEAR - 8)) - (CMP_BLOCK - 1)))
    return bd, bc


def _gelu_tanh(x):
    return 0.5 * x * (1.0 + jnp.tanh(math.sqrt(2.0 / math.pi) * (x + 0.044715 * (x * x * x))))


def _cmp_kernel(kb_ref, pos_ref, w1_ref, w2_ref, o_ref, *, nb):
    half = CMP_STRIDE * HEAD_DIM
    kb = kb_ref[...].astype(F32)
    xa = (kb + pos_ref[0:1, :]).astype(BF16)
    xb = (kb + pos_ref[1:2, :]).astype(BF16)
    a = _dot(xa, w1_ref[0:half, :])
    b = _dot(xb, w1_ref[half:2 * half, :])
    pre = a + pltpu.roll(b, nb - 1, axis=0)
    out = _dot(_gelu_tanh(pre).astype(BF16), w2_ref[...])
    row = lax.broadcasted_iota(jnp.int32, out.shape, 0)
    out = jnp.where(row < nb - 1, out, 0.0)
    o_ref[0:CMP_PAD, :] = jnp.zeros((CMP_PAD, HEAD_DIM), F32)
    o_ref[CMP_PAD:CMP_PAD + nb, :] = out


def _compress(kvh, cmp_pos, cmp_w1, cmp_w2):
    t = kvh.shape[1]
    nb = t // CMP_STRIDE
    kb = kvh[:2 * NSA_GROUPS].reshape(2 * NSA_GROUPS, nb, CMP_STRIDE * HEAD_DIM)
    pos = cmp_pos.astype(F32).reshape(2, 2, CMP_STRIDE * HEAD_DIM)
    return pl.pallas_call(
        functools.partial(_cmp_kernel, nb=nb),
        grid=(2 * NSA_GROUPS,),
        in_specs=[pl.BlockSpec((None, nb, CMP_STRIDE * HEAD_DIM), lambda j: (j, 0, 0)),
                  pl.BlockSpec((None, 2, CMP_STRIDE * HEAD_DIM), lambda j: (j // NSA_GROUPS, 0, 0)),
                  pl.BlockSpec((None, CMP_BLOCK * HEAD_DIM, HEAD_DIM), lambda j: (j // NSA_GROUPS, 0, 0)),
                  pl.BlockSpec((None, HEAD_DIM, HEAD_DIM), lambda j: (j // NSA_GROUPS, 0, 0))],
        out_specs=pl.BlockSpec((None, CMP_PAD + nb, HEAD_DIM), lambda j: (j, 0, 0)),
        out_shape=jax.ShapeDtypeStruct((2 * NSA_GROUPS, CMP_PAD + nb, HEAD_DIM), F32),
        compiler_params=_cparams(("parallel",)),
        name="compress",
    )(kb, pos, cmp_w1.astype(BF16), cmp_w2.astype(BF16))


def _dot_split_rhs(w, x):
    hi, mid, lo = _split3(x)
    return _dot(w, hi) + _dot(w, mid) + _dot(w, lo)


def _nsa_kernel(q_ref, g_ref, kc_ref, vc_ref, vct_ref, ksa_ref, vst_ref, kw_ref, vwt_ref,
                bd_ref, bc_ref, ovl_ref, o_ref,
                qa_ref, s_ref, sc_ref, imp_ref, gate_ref, m_ref, acc_ref, out_ref):
    i = pl.program_id(1)
    rows = NSA_ROWS
    q = q_ref[...].reshape(rows, HEAD_DIM)
    tl_lane = lax.broadcasted_iota(jnp.int32, (Q_BLOCK, rows), 1) % Q_BLOCK
    key_row = lax.broadcasted_iota(jnp.int32, (Q_BLOCK, rows), 0)
    gate_ref[...] = jax.nn.sigmoid(g_ref[...]).T
    gate_row0 = 3 * NSA_HG * pl.program_id(0)

    def gate(branch):
        return jnp.concatenate([gate_ref[pl.ds(gate_row0 + 3 * hg + branch, 1), :] for hg in range(NSA_HG)], axis=1)

    def head_sum(p):
        acc = p[:, 0:Q_BLOCK]
        for hg in range(1, NSA_HG):
            acc = acc + p[:, hg * Q_BLOCK:(hg + 1) * Q_BLOCK]
        return acc

    n_first_near = 8 * i - (CMP_NEAR - 8)
    n_blocks = (jnp.maximum(n_first_near, 0) + CMP_ROWS - 1) // CMP_ROWS

    def far_rows(b):
        return pl.multiple_of(b * CMP_ROWS, CMP_ROWS)

    def scores_pass(b, mx):
        r0 = far_rows(b)
        kb = kc_ref[pl.ds(CMP_PAD + r0, CMP_ROWS), :].astype(BF16)
        row = r0 + lax.broadcasted_iota(jnp.int32, (CMP_ROWS, rows), 0)
        s = jnp.where(row < n_first_near, _dot_nt(kb, q), NEG_INF)
        sc_ref[pl.ds(r0, CMP_ROWS), :] = s
        return jnp.maximum(mx, jnp.max(s, axis=0, keepdims=True))

    mx_far = lax.fori_loop(0, n_blocks, scores_pass, jnp.full((1, rows), NEG_INF, F32))
    start = pl.multiple_of(8 * i + 8 + CMP_PAD - CMP_NEAR, 8)
    kcn = kc_ref[pl.ds(start, CMP_NEAR), :].astype(BF16)
    vcn_t = vc_ref[pl.ds(start, CMP_NEAR), :].T.astype(BF16)
    d_near = tl_lane - CMP_STRIDE * (key_row - (CMP_NEAR - 8)) - (CMP_BLOCK - 1)
    near_ok = (d_near >= 0) & (n_first_near + key_row >= 0)
    s_near = jnp.where(near_ok, _dot_nt(kcn, q) + bc_ref[...], NEG_INF)
    mx = jnp.maximum(mx_far, jnp.max(s_near, axis=0, keepdims=True))
    e_near = jnp.exp2(s_near - mx)

    def exp_pass(b, den):
        r0 = far_rows(b)
        e = jnp.exp2(sc_ref[pl.ds(r0, CMP_ROWS), :] - mx)
        sc_ref[pl.ds(r0, CMP_ROWS), :] = e
        return den + jnp.sum(e, axis=0, keepdims=True)

    den = lax.fori_loop(0, n_blocks, exp_pass, jnp.sum(e_near, axis=0, keepdims=True))
    inv = jnp.where(mx > 0.5 * NEG_INF, 1.0 / den, 0.0)
    p_near = e_near * inv
    sb = lax.broadcasted_iota(jnp.int32, (SEL_COLS, CMP_NEAR), 0)
    nn = n_first_near + lax.broadcasted_iota(jnp.int32, (SEL_COLS, CMP_NEAR), 1)
    ovl_near = jnp.where((nn >= 4 * sb - 1) & (nn <= 4 * sb + 3) & (nn >= 0), 1.0, 0.0).astype(BF16)
    out_ref[...] = _dot(vcn_t, p_near.astype(BF16))
    imp_ref[...] = _dot_split_rhs(ovl_near, head_sum(p_near))

    def out_pass(b, carry):
        r0 = far_rows(b)
        p = sc_ref[pl.ds(r0, CMP_ROWS), :] * inv
        out_ref[...] += _dot(vct_ref[:, pl.ds(CMP_PAD + r0, CMP_ROWS)].astype(BF16), p.astype(BF16))
        imp_ref[...] += _dot_split_rhs(ovl_ref[:, pl.ds(r0, CMP_ROWS)], head_sum(p))
        return carry

    lax.fori_loop(0, n_blocks, out_pass, 0)
    out_ref[...] = gate(0) * out_ref[...]

    imp = imp_ref[...]
    blk = lax.broadcasted_iota(jnp.int32, (SEL_COLS, Q_BLOCK), 0)
    tl = lax.broadcasted_iota(jnp.int32, (SEL_COLS, Q_BLOCK), 1)
    cur = 2 * i + (tl >= SEL_BLOCK).astype(jnp.int32)
    forced = (blk == 0) | (blk == cur) | (blk == cur - 1)
    causal_blk = blk * SEL_BLOCK <= Q_BLOCK * i + tl
    work = jnp.where(forced, -1.0, jnp.where(causal_blk, imp, -1.0))
    blk_f = blk.astype(F32)
    sel = jnp.where(forced, 1.0, 0.0)
    for _ in range(SEL_TOPK - 3):
        best = jnp.max(work, axis=0, keepdims=True)
        first = jnp.min(jnp.where(work == best, blk_f, float(SEL_COLS)), axis=0, keepdims=True)
        pick = blk_f == first
        sel = jnp.where(pick, 1.0, sel)
        work = jnp.where(pick, -2.0, work)
    amask = jnp.where(sel > 0.0, 0.0, NEG_INF)
    for half in range(SEL_COLS // LANE):
        a_t = amask[half * LANE:(half + 1) * LANE, :].T.astype(BF16)
        qa_ref[half, :, 0:HEAD_DIM] = q
        qa_ref[half, :, HEAD_DIM:2 * HEAD_DIM] = jnp.concatenate([a_t] * NSA_HG, axis=0)

    def sel_chunk(tile, n_keys):
        off = pl.multiple_of(tile * Q_BLOCK, Q_BLOCK)
        qa = qa_ref[tile // (LANE // 2)]
        return _dot_nt(ksa_ref[pl.ds(off, n_keys), :], qa), vst_ref[:, pl.ds(off, n_keys)]

    s_t, v_t = sel_chunk(i, Q_BLOCK)
    near = [(jnp.where(key_row <= tl_lane, s_t + bd_ref[Q_BLOCK:, :], NEG_INF), v_t)]
    s_t, v_t = sel_chunk(jnp.maximum(i - 1, 0), Q_BLOCK)
    near.append((jnp.where(i >= 1, s_t + bd_ref[:Q_BLOCK, :], NEG_INF), v_t))
    m_ref[...], acc_ref[...] = _softmax_tiles(near)
    n_far = jnp.maximum(i - 1, 0)
    n_chunks = n_far // FAR_TILES
    chunk_keys = FAR_TILES * Q_BLOCK

    def far_scores(c):
        tile = FAR_TILES * jnp.minimum(c, jnp.maximum(n_chunks - 1, 0))
        off = pl.multiple_of(tile * Q_BLOCK, chunk_keys)
        return _dot_nt(ksa_ref[pl.ds(off, chunk_keys), :], qa_ref[tile // (LANE // 2)])

    def far_consume(c, s_slot):
        off = pl.multiple_of(c * chunk_keys, chunk_keys)
        _flash_update_from(s_slot, vst_ref[:, pl.ds(off, chunk_keys)], m_ref, acc_ref)

    s_ref[0] = far_scores(0)
    _pipelined_chunks(n_chunks, far_scores, far_consume, s_ref)

    @pl.when(n_chunks % 2 == 1)
    def _():
        far_consume(n_chunks - 1, s_ref.at[0])

    left = n_far % FAR_TILES

    @pl.when(left >= 2)
    def _():
        s_t, v_t = sel_chunk(n_far - left, 2 * Q_BLOCK)
        _flash_update(s_t, v_t, m_ref, acc_ref)

    @pl.when(left % 2 == 1)
    def _():
        s_t, v_t = sel_chunk(n_far - 1, Q_BLOCK)
        _flash_update(s_t, v_t, m_ref, acc_ref)

    out_ref[...] += gate(1) * _normalized(acc_ref[...])

    def win_chunk(tile):
        off = pl.multiple_of(jnp.maximum(tile, 0) * Q_BLOCK, Q_BLOCK)
        return _dot_nt(kw_ref[pl.ds(off, Q_BLOCK), :], q), vwt_ref[:, pl.ds(off, Q_BLOCK)]

    s_t, v_t = win_chunk(i)
    win = [(jnp.where(key_row <= tl_lane, s_t + bd_ref[Q_BLOCK:, :], NEG_INF), v_t)]
    s_t, v_t = win_chunk(i - 1)
    win.append((jnp.where(i >= 1, s_t + bd_ref[:Q_BLOCK, :], NEG_INF), v_t))
    for back in (2, 3):
        s_t, v_t = win_chunk(i - back)
        win.append((jnp.where(i >= back, s_t, NEG_INF), v_t))
    s_t, v_t = win_chunk(i - 4)
    win.append((jnp.where((key_row > tl_lane) & (i >= 4), s_t, NEG_INF), v_t))
    _, acc_w = _softmax_tiles(win)
    o = out_ref[...] + gate(2) * _normalized(acc_w)
    for hg in range(NSA_HG):
        o_ref[:, hg * HEAD_DIM:(hg + 1) * HEAD_DIM] = o[:, hg * Q_BLOCK:(hg + 1) * Q_BLOCK].T.astype(o_ref.dtype)


def _nsa_attention(qkh, gates, kcv, kcv_t, ksa, vt, bd, bc, ovl):
    t = qkh.shape[1]
    ncp = t // CMP_STRIDE
    assert ncp % CMP_ROWS == 0
    resident = functools.partial(pl.BlockSpec, pipeline_mode=pl.Buffered(1))
    return pl.pallas_call(
        _nsa_kernel,
        grid=(NSA_GROUPS, t // Q_BLOCK),
        in_specs=[pl.BlockSpec((NSA_HG, Q_BLOCK, HEAD_DIM), lambda g, i: (g, i, 0)),
                  pl.BlockSpec((Q_BLOCK, LANE), lambda g, i: (i, 0)),
                  resident((None, CMP_PAD + ncp, HEAD_DIM), lambda g, i: (g, 0, 0)),
                  resident((None, CMP_PAD + ncp, HEAD_DIM), lambda g, i: (NSA_GROUPS + g, 0, 0)),
                  resident((None, HEAD_DIM, CMP_PAD + ncp), lambda g, i: (NSA_GROUPS + g, 0, 0)),
                  resident((None, t, 2 * HEAD_DIM), lambda g, i: (g, 0, 0)),
                  resident((None, V_ROWS, t), lambda g, i: (g, 0, 0)),
                  resident((None, t, HEAD_DIM), lambda g, i: (N_HEADS + 3 * NSA_GROUPS + g, 0, 0)),
                  resident((None, V_ROWS, t), lambda g, i: (NSA_GROUPS + g, 0, 0)),
                  resident((2 * Q_BLOCK, NSA_ROWS), lambda g, i: (0, g)),
                  resident((CMP_NEAR, NSA_ROWS), lambda g, i: (0, g)),
                  resident((SEL_COLS, ncp), lambda g, i: (0, 0))],
        out_specs=pl.BlockSpec((Q_BLOCK, NSA_HG * HEAD_DIM), lambda g, i: (i, g)),
        out_shape=jax.ShapeDtypeStruct((t, D_MODEL), BF16),
        scratch_shapes=[pltpu.VMEM((SEL_COLS // LANE, NSA_ROWS, 2 * HEAD_DIM), BF16),
                        pltpu.VMEM((2, FAR_TILES * Q_BLOCK, NSA_ROWS), F32),
                        pltpu.VMEM((ncp, NSA_ROWS), F32), pltpu.VMEM((SEL_COLS, Q_BLOCK), F32),
                        pltpu.VMEM((LANE, Q_BLOCK), F32), pltpu.VMEM((1, NSA_ROWS), F32),
                        pltpu.VMEM((V_ROWS, NSA_ROWS), F32), pltpu.VMEM((HEAD_DIM, NSA_ROWS), F32)],
        compiler_params=_cparams(("parallel", "arbitrary")),
        name="nsa_attention",
    )(qkh, gates, kcv, kcv, kcv_t, ksa, vt, qkh, vt, bd, bc, ovl)


def _nsa_layer(h, g_norm, w_in, w_o, layer, cmp_pos, cmp_w1, cmp_w2, bd, bc):
    t = h.shape[0]
    assert t // SEL_BLOCK <= SEL_COLS
    a = _rmsnorm(h, g_norm, BF16)
    tn = NSA_KV_DIM
    q_blocks = D_MODEL // tn
    qkh = _proj(a, w_in, layer, n_out=D_MODEL + 4 * NSA_KV_DIM, out_dtype=BF16, head_major=True, tn=tn,
                n_scaled=D_MODEL,
                col_block=lambda j: j + jnp.where(j >= q_blocks + 3, 1, 0))
    vt = _proj_t(a, w_in, layer, n_out=2 * NSA_KV_DIM, tn=tn, col_block=lambda j: q_blocks + 3 + 2 * j)
    gates = _proj(a, w_in, layer, n_out=LANE, out_dtype=F32, head_major=False, tn=LANE,
                  col_block=lambda j: (D_MODEL + 6 * NSA_KV_DIM) // LANE + j,
                  valid_cols=3 * N_HEADS)
    kcv = _compress(qkh[N_HEADS:N_HEADS + 2 * NSA_GROUPS], cmp_pos, cmp_w1, cmp_w2)
    kcv_t = jnp.swapaxes(kcv, 1, 2)
    key_blk = (jnp.arange(t) // SEL_BLOCK) % LANE
    onehot = (key_blk[:, None] == jnp.arange(LANE)[None, :]).astype(BF16)
    ksa = jnp.concatenate([qkh[N_HEADS + 2 * NSA_GROUPS:N_HEADS + 3 * NSA_GROUPS],
                           jnp.broadcast_to(onehot, (NSA_GROUPS, t, LANE))], axis=-1)
    n = jnp.arange(t // CMP_STRIDE)[None, :]
    sblk = jnp.arange(SEL_COLS)[:, None]
    ovl = ((n >= 4 * sblk - 1) & (n <= 4 * sblk + 3)).astype(BF16)
    o = _nsa_attention(qkh, gates, kcv, kcv_t, ksa, vt, bd, bc, ovl)
    return _mm_res(o, w_o, layer, h, tile=WO_TILE)


def kernel(x, norm_mix, norm_ffn, norm_final, rel_table, nsa_w_in, nsa_w_o, nsa_cmp_pos, nsa_cmp_w1,
           nsa_cmp_w2, fox_w_in, fox_b_f, fox_w_o, ffn_w_gate, ffn_w_up, ffn_w_down):
    depth = norm_mix.shape[0]
    bd, bc = _nsa_bias_templates(rel_table.astype(F32))
    nsa_w_in, fox_w_in = jnp.swapaxes(nsa_w_in.astype(F32), 1, 2), jnp.swapaxes(fox_w_in.astype(F32), 1, 2)
    ffn_w_gate, ffn_w_up = ffn_w_gate.astype(F32), ffn_w_up.astype(F32)
    nsa_w_o, fox_w_o, ffn_w_down = nsa_w_o.astype(BF16), fox_w_o.astype(BF16), ffn_w_down.astype(BF16)
    outs = []
    for bi in range(x.shape[0]):
        h = x[bi]
        for i in range(depth):
            j = i // 2
            if i % 2 == 0:
                h = _nsa_layer(h, norm_mix[i], nsa_w_in, nsa_w_o, j, nsa_cmp_pos[j], nsa_cmp_w1[j],
                               nsa_cmp_w2[j], bd, bc)
            else:
                h = _fox_layer(h, norm_mix[i], fox_w_in, fox_b_f[j], fox_w_o, j)
            h = _ffn(h, norm_ffn[i], ffn_w_gate, ffn_w_up, ffn_w_down, i)
        outs.append(_rmsnorm(h, norm_final, F32))
    return jnp.stack(outs, axis=0)
```

```python
import functools
import math

import jax
import jax.numpy as jnp
from jax import lax
from jax.experimental import pallas as pl
from jax.experimental.pallas import tpu as pltpu

F32 = jnp.float32
BF16 = jnp.bfloat16

D_MODEL = 4096
HEAD_DIM = 128
N_HEADS = D_MODEL // HEAD_DIM
NSA_GROUPS = 4
NSA_HG = N_HEADS // NSA_GROUPS
NSA_KV_DIM = NSA_GROUPS * HEAD_DIM
CMP_BLOCK = 32
CMP_STRIDE = 16
SEL_BLOCK = 64
SEL_TOPK = 16
WINDOW = 512
REL_BUCKETS = 32
REL_MAX_DIST = 128
Q_BLOCK = 128
RMS_EPS = 1e-6
NEG_INF = -1e30
LOG2E = 1.4426950408889634
Q_SCALE = HEAD_DIM ** -0.5 * LOG2E

LANE = 128
SEL_COLS = 256
CMP_PAD = 128
CMP_NEAR = 128
CMP_ROWS = 256
FAR_TILES = 4
NSA_ROWS = NSA_HG * Q_BLOCK
V_ROWS = HEAD_DIM + 16
KEY_BLOCK = 256
N_PIECES = 3

RMS_ROWS = 256
PROJ_TILE = (1024, 512)
GATEUP_TILE = (2048, 256)
DOWN_TILE = (512, 512)
WO_TILE = (1024, 512)
FOX_Q_TILE = 1024
CUM_ROWS = 512
VMEM_LIMIT = 56 * 1024 * 1024


def _cparams(sem):
    return pltpu.CompilerParams(dimension_semantics=sem, vmem_limit_bytes=VMEM_LIMIT)


def _dot(a, b):
    return jnp.dot(a, b, preferred_element_type=F32)


def _dot_nt(a, b):
    return lax.dot_general(a, b, (((1,), (1,)), ((), ())), preferred_element_type=F32)


def _rmsnorm_kernel(x_ref, g_ref, o_ref):
    x = x_ref[...]
    ms = jnp.mean(x * x, axis=-1, keepdims=True)
    o_ref[...] = (x * lax.rsqrt(ms + RMS_EPS) * g_ref[...]).astype(o_ref.dtype)


def _rmsnorm(x, g, out_dtype):
    t, d = x.shape
    tm = min(RMS_ROWS, t)
    return pl.pallas_call(
        _rmsnorm_kernel,
        grid=(t // tm,),
        in_specs=[pl.BlockSpec((tm, d), lambda i: (i, 0)),
                  pl.BlockSpec((1, d), lambda i: (0, 0))],
        out_specs=pl.BlockSpec((tm, d), lambda i: (i, 0)),
        out_shape=jax.ShapeDtypeStruct((t, d), out_dtype),
        compiler_params=_cparams(("parallel",)),
        name="rmsnorm",
    )(x, g.reshape(1, d))


def _proj_kernel(x_ref, w_ref, o_ref, *, head_major, n_scaled_blocks, valid_cols):
    r = _dot_nt(x_ref[...], w_ref[...].astype(BF16))
    if valid_cols is not None:
        r = jnp.where(lax.broadcasted_iota(jnp.int32, r.shape, 1) < valid_cols, r, 0.0)
    if n_scaled_blocks:
        r = r * jnp.where(pl.program_id(1) < n_scaled_blocks, Q_SCALE, 1.0)
    if head_major:
        for s in range(o_ref.shape[0]):
            o_ref[s] = r[:, s * LANE:(s + 1) * LANE].astype(o_ref.dtype)
    else:
        o_ref[...] = r.astype(o_ref.dtype)


def _identity(j):
    return j


_row_block = functools.partial(pl.BlockSpec, pipeline_mode=pl.Buffered(1))


def _proj(x, w, layer, *, n_out, out_dtype, head_major, tn=PROJ_TILE[1], col_block=_identity, n_scaled=0,
          valid_cols=None):
    t, k = x.shape
    n = n_out
    tm = min(PROJ_TILE[0], t)
    tn = min(tn, n)
    assert n_scaled % tn == 0 and n % tn == 0
    if head_major:
        out_shape = jax.ShapeDtypeStruct((n // LANE, t, LANE), out_dtype)
        out_spec = pl.BlockSpec((tn // LANE, tm, LANE), lambda i, j: (j, i, 0))
    else:
        out_shape = jax.ShapeDtypeStruct((t, n), out_dtype)
        out_spec = pl.BlockSpec((tm, tn), lambda i, j: (i, j))
    return pl.pallas_call(
        functools.partial(_proj_kernel, head_major=head_major, n_scaled_blocks=n_scaled // tn,
                          valid_cols=valid_cols),
        grid=(t // tm, n // tn),
        in_specs=[pl.BlockSpec((tm, k), lambda i, j: (i, 0)),
                  pl.BlockSpec((None, tn, k), lambda i, j: (layer, col_block(j), 0))],
        out_specs=out_spec,
        out_shape=out_shape,
        compiler_params=_cparams(("parallel", "parallel")),
        name="proj",
    )(x, w)


def _proj_t_kernel(x_ref, w_ref, o_ref):
    r = _dot_nt(x_ref[...], w_ref[...].astype(BF16))
    ones = _ones_rows(r.shape[0], o_ref.dtype)
    for s in range(o_ref.shape[0]):
        o_ref[s, 0:HEAD_DIM, :] = r[:, s * LANE:(s + 1) * LANE].T.astype(o_ref.dtype)
        o_ref[s, HEAD_DIM:, :] = ones


def _proj_t(x, w, layer, *, n_out, tn=PROJ_TILE[1], col_block=_identity):
    t, k = x.shape
    n = n_out
    tm = min(PROJ_TILE[0], t)
    tn = min(tn, n)
    return pl.pallas_call(
        _proj_t_kernel,
        grid=(t // tm, n // tn),
        in_specs=[pl.BlockSpec((tm, k), lambda i, j: (i, 0)),
                  pl.BlockSpec((None, tn, k), lambda i, j: (layer, col_block(j), 0))],
        out_specs=pl.BlockSpec((tn // LANE, V_ROWS, tm), lambda i, j: (j, 0, i)),
        out_shape=jax.ShapeDtypeStruct((n // LANE, V_ROWS, t), BF16),
        compiler_params=_cparams(("parallel", "parallel")),
        name="proj_t",
    )(x, w)


def _mm_res_kernel(x_ref, w_ref, r_ref, o_ref):
    o_ref[...] = r_ref[...] + _dot(x_ref[...], w_ref[...])


def _mm_res(x, w, layer, res, *, tile, x_spec=pl.BlockSpec):
    t, k = x.shape
    n = w.shape[2]
    tm, tn = min(tile[0], t), tile[1]
    return pl.pallas_call(
        _mm_res_kernel,
        grid=(t // tm, n // tn),
        in_specs=[x_spec((tm, k), lambda i, j: (i, 0)),
                  pl.BlockSpec((None, k, tn), lambda i, j: (layer, 0, j)),
                  pl.BlockSpec((tm, tn), lambda i, j: (i, j))],
        out_specs=pl.BlockSpec((tm, tn), lambda i, j: (i, j)),
        out_shape=jax.ShapeDtypeStruct((t, n), F32),
        compiler_params=_cparams(("parallel", "parallel")),
        name="mm_res",
    )(x, w, res)


def _gateup_kernel(x_ref, wg_ref, wu_ref, o_ref):
    x = x_ref[...]
    g = _dot(x, wg_ref[...].astype(BF16))
    u = _dot(x, wu_ref[...].astype(BF16))
    o_ref[...] = (g * jax.nn.sigmoid(g) * u).astype(o_ref.dtype)


def _gateup(x, wg, wu, layer):
    t, k = x.shape
    n = wg.shape[2]
    tm, tn = min(GATEUP_TILE[0], t), GATEUP_TILE[1]
    return pl.pallas_call(
        _gateup_kernel,
        grid=(t // tm, n // tn),
        in_specs=[_row_block((tm, k), lambda i, j: (i, 0)),
                  pl.BlockSpec((None, k, tn), lambda i, j: (layer, 0, j)),
                  pl.BlockSpec((None, k, tn), lambda i, j: (layer, 0, j))],
        out_specs=pl.BlockSpec((tm, tn), lambda i, j: (i, j)),
        out_shape=jax.ShapeDtypeStruct((t, n), BF16),
        compiler_params=_cparams(("parallel", "parallel")),
        name="gateup",
    )(x, wg, wu)


def _ffn(h, g_norm, w_gate, w_up, w_down, layer):
    a = _rmsnorm(h, g_norm, BF16)
    h1 = _gateup(a, w_gate, w_up, layer)
    return _mm_res(h1, w_down, layer, h, tile=DOWN_TILE)


def _ones_rows(n_cols, dtype):
    row = lax.broadcasted_iota(jnp.int32, (V_ROWS - HEAD_DIM, n_cols), 0)
    return jnp.where(row == 0, 1.0, 0.0).astype(dtype)


def _flash_init(m_ref, acc_ref):
    m_ref[...] = jnp.full(m_ref.shape, NEG_INF, F32)
    acc_ref[...] = jnp.zeros(acc_ref.shape, F32)


def _flash_update(s_t, v_t, m_ref, acc_ref):
    m_prev = m_ref[...]
    m_new = jnp.maximum(m_prev, jnp.max(s_t, axis=0, keepdims=True))
    alpha = jnp.exp2(m_prev - m_new)
    p = jnp.exp2(s_t - m_new)
    acc_ref[...] = alpha * acc_ref[...] + _dot(v_t, p.astype(BF16))
    m_ref[...] = m_new


def _flash_update_from(s_slot, v_t, m_ref, acc_ref):
    n_keys = s_slot.shape[0]
    m_prev = m_ref[...]
    m_new = jnp.maximum(m_prev, jnp.max(s_slot[...], axis=0, keepdims=True))
    acc = jnp.exp2(m_prev - m_new) * acc_ref[...]
    for r in range(0, n_keys, KEY_BLOCK):
        p = jnp.exp2(s_slot[r:r + KEY_BLOCK, :] - m_new)
        acc = acc + _dot(v_t[:, r:r + KEY_BLOCK], p.astype(BF16))
    acc_ref[...] = acc
    m_ref[...] = m_new


def _softmax_tiles(tiles):
    mx = functools.reduce(jnp.maximum, [jnp.max(s_t, axis=0, keepdims=True) for s_t, _ in tiles])
    acc = None
    for s_t, v_t in tiles:
        a = _dot(v_t, jnp.exp2(s_t - mx).astype(BF16))
        acc = a if acc is None else acc + a
    return mx, acc


def _normalized(acc):
    return acc[0:HEAD_DIM] / acc[HEAD_DIM:HEAD_DIM + 1]


def _split3(x):
    hi = x.astype(BF16)
    r = x - hi.astype(F32)
    mid = r.astype(BF16)
    lo = (r - mid.astype(F32)).astype(BF16)
    return hi, mid, lo


def _cum_kernel(f_ref, b_ref, tri_ref, place_ref, qx_ref, kx_ref, carry_ref):
    @pl.when(pl.program_id(0) == 0)
    def _():
        carry_ref[...] = jnp.zeros_like(carry_ref)

    head_lane = lax.broadcasted_iota(jnp.int32, f_ref.shape, 1) < N_HEADS
    x = jnp.where(head_lane, f_ref[...] + b_ref[...], 0.0)
    log_f = jnp.minimum(x, 0.0) - jnp.log1p(jnp.exp(-jnp.abs(x)))
    c = jnp.dot(tri_ref[...], log_f, precision=lax.Precision.HIGHEST,
                preferred_element_type=F32) + carry_ref[...]
    carry_ref[...] = c[c.shape[0] - 1:, :]
    pieces = jnp.concatenate(_split3(c * LOG2E), axis=1)
    lane = lax.broadcasted_iota(jnp.int32, (c.shape[0], LANE), 1)
    q_const = jnp.where((lane >= N_PIECES) & (lane < 2 * N_PIECES), -1.0, 0.0)
    k_const = jnp.where(lane < N_PIECES, 1.0, 0.0)
    for h in range(N_HEADS):
        placed = _dot(pieces, place_ref[h])
        qx_ref[h] = (placed[:, :LANE] + q_const).astype(BF16)
        kx_ref[h] = (placed[:, LANE:] + k_const).astype(BF16)


def _forget_extras(f, b):
    t, n = f.shape
    tb = min(CUM_ROWS, t)
    tri = jnp.tril(jnp.ones((tb, tb), F32))
    h = jnp.arange(N_HEADS)[:, None, None]
    row = jnp.arange(N_PIECES * LANE)[None, :, None]
    col = jnp.arange(2 * LANE)[None, None, :]
    piece, head = row // LANE, row % LANE
    place = ((head == h) & ((col == piece) | (col == LANE + N_PIECES + piece))).astype(BF16)
    out = jax.ShapeDtypeStruct((N_HEADS, t, LANE), BF16)
    return pl.pallas_call(
        _cum_kernel,
        grid=(t // tb,),
        in_specs=[pl.BlockSpec((tb, n), lambda i: (i, 0)),
                  pl.BlockSpec((1, n), lambda i: (0, 0)),
                  pl.BlockSpec((tb, tb), lambda i: (0, 0)),
                  pl.BlockSpec((N_HEADS, N_PIECES * LANE, 2 * LANE), lambda i: (0, 0, 0))],
        out_specs=[pl.BlockSpec((N_HEADS, tb, LANE), lambda i: (0, i, 0)),
                   pl.BlockSpec((N_HEADS, tb, LANE), lambda i: (0, i, 0))],
        out_shape=[out, out],
        scratch_shapes=[pltpu.VMEM((1, n), F32)],
        compiler_params=_cparams(("arbitrary",)),
        name="forget_extras",
    )(f, b, tri, place)


def _pipelined_loop(first, trips, per_trip, scores, consume, s_ref):
    def body(t, carry):
        c = first + t * per_trip
        for u in range(per_trip):
            s_next = scores(c + u + 1)
            consume(c + u, s_ref.at[u % 2])
            s_ref[(u + 1) % 2] = s_next
        return carry

    lax.fori_loop(0, trips, body, 0)


def _pipelined_chunks(n_chunks, scores, consume, s_ref):
    quads = n_chunks // 4
    _pipelined_loop(0, quads, 4, scores, consume, s_ref)
    _pipelined_loop(4 * quads, (n_chunks % 4) // 2, 2, scores, consume, s_ref)


def _fox_kernel(q_ref, qx_ref, k_ref, kx_ref, vt_ref, o_ref, ka_ref, qa_ref, s_ref, m_ref, acc_ref, *, tq, tk):
    i = pl.program_id(1)
    assert tq == 2 * tk

    @pl.when(i == 0)
    def _():
        ka_ref[:, 0:HEAD_DIM] = k_ref[...]
        ka_ref[:, HEAD_DIM:] = kx_ref[...]

    qa_ref[:, 0:HEAD_DIM] = q_ref[...]
    qa_ref[:, HEAD_DIM:] = qx_ref[...]
    _flash_init(m_ref, acc_ref)

    def scores(c):
        off = pl.multiple_of(c * tk, tk)
        return _dot_nt(ka_ref[pl.ds(off, tk), :], qa_ref[...])

    def values(c):
        return vt_ref[:, pl.ds(pl.multiple_of(c * tk, tk), tk)]

    def consume(c, s_slot):
        _flash_update_from(s_slot, values(c), m_ref, acc_ref)

    n_full = 2 * i
    s_ref[0] = scores(0)
    _pipelined_chunks(n_full, scores, consume, s_ref)
    key = lax.broadcasted_iota(jnp.int32, (tk, tq), 0)
    qry = lax.broadcasted_iota(jnp.int32, (tk, tq), 1)
    s_last = scores(n_full + 1)
    _flash_update(jnp.where(key <= qry, s_ref[0], NEG_INF), values(n_full), m_ref, acc_ref)
    _flash_update(jnp.where(key + tk <= qry, s_last, NEG_INF), values(n_full + 1), m_ref, acc_ref)
    o_ref[...] = _normalized(acc_ref[...]).T.astype(o_ref.dtype)


def _fox_attention(qkh, qx, kx, vt):
    t = qkh.shape[1]
    tq = min(FOX_Q_TILE, t)
    tk = tq // 2
    return pl.pallas_call(
        functools.partial(_fox_kernel, tq=tq, tk=tk),
        grid=(N_HEADS, t // tq),
        in_specs=[pl.BlockSpec((None, tq, HEAD_DIM), lambda h, i: (h, i, 0)),
                  pl.BlockSpec((None, tq, LANE), lambda h, i: (h, i, 0)),
                  pl.BlockSpec((None, t, HEAD_DIM), lambda h, i: (N_HEADS + h, 0, 0)),
                  pl.BlockSpec((None, t, LANE), lambda h, i: (h, 0, 0)),
                  pl.BlockSpec((None, V_ROWS, t), lambda h, i: (h, 0, 0))],
        out_specs=pl.BlockSpec((tq, HEAD_DIM), lambda h, i: (i, h)),
        out_shape=jax.ShapeDtypeStruct((t, D_MODEL), BF16),
        scratch_shapes=[pltpu.VMEM((t, 2 * HEAD_DIM), BF16),
                        pltpu.VMEM((tq, 2 * HEAD_DIM), BF16), pltpu.VMEM((2, tk, tq), F32),
                        pltpu.VMEM((1, tq), F32), pltpu.VMEM((V_ROWS, tq), F32)],
        compiler_params=_cparams(("parallel", "arbitrary")),
        name="fox_attention",
    )(qkh, qx, qkh, kx, vt)


def _fox_layer(h, g_norm, w_in, b_f, w_o, layer):
    a = _rmsnorm(h, g_norm, BF16)
    tn = PROJ_TILE[1]
    qkh = _proj(a, w_in, layer, n_out=2 * D_MODEL, out_dtype=BF16, head_major=True, tn=tn,
                n_scaled=D_MODEL)
    vt = _proj_t(a, w_in, layer, n_out=D_MODEL, tn=tn, col_block=lambda j: 2 * D_MODEL // tn + j)
    f = _proj(a, w_in, layer, n_out=LANE, out_dtype=F32, head_major=False, tn=LANE,
              col_block=lambda j: 3 * D_MODEL // LANE + j, valid_cols=N_HEADS)
    b = jnp.pad(b_f.astype(F32), (0, LANE - N_HEADS)).reshape(1, LANE)
    qx, kx = _forget_extras(f, b)
    o = _fox_attention(qkh, qx, kx, vt)
    return _mm_res(o, w_o, layer, h, tile=WO_TILE)


def _rel_bucket_const(dist):
    n = jnp.maximum(dist, 0)
    max_exact = REL_BUCKETS // 2
    nf = jnp.maximum(n, max_exact).astype(F32)
    large = max_exact + (jnp.log(nf / max_exact) / math.log(REL_MAX_DIST / max_exact)
                         * (REL_BUCKETS - max_exact)).astype(jnp.int32)
    large = jnp.minimum(large, REL_BUCKETS - 1)
    return jnp.where(n < max_exact, n, large)


def _bias_kernel(tab_ref, bkt_ref, o_ref):
    h = pl.program_id(0)
    bkt = bkt_ref[...]
    acc = jnp.zeros(bkt.shape, F32)
    for b in range(REL_BUCKETS):
        acc = jnp.where(bkt == b, tab_ref[b, h], acc)
    o_ref[...] = (acc - tab_ref[REL_BUCKETS - 1, h]) * LOG2E


def _bias_template(rel_table, bkt):
    r, c = bkt.shape
    return pl.pallas_call(
        _bias_kernel,
        grid=(N_HEADS,),
        in_specs=[pl.BlockSpec(memory_space=pltpu.SMEM),
                  pl.BlockSpec((r, c), lambda h: (0, 0))],
        out_specs=pl.BlockSpec((r, c), lambda h: (0, h)),
        out_shape=jax.ShapeDtypeStruct((r, N_HEADS * c), F32),
        compiler_params=_cparams(("arbitrary",)),
        name="bias_template",
    )(rel_table, bkt)


def _nsa_bias_templates(rel_table):
    tl = jnp.arange(Q_BLOCK)[None, :]
    bd = _bias_template(rel_table, _rel_bucket_const(tl + Q_BLOCK - jnp.arange(2 * Q_BLOCK)[:, None]))
    m = jnp.arange(CMP_NEAR)[:, None]
    bc = _bias_template(rel_table, _rel_bucket_const(
        tl - CMP_STRIDE * (m - (CMP_NEAR - 8)) - (CMP_BLOCK - 1)))
    return bd, bc


def _gelu_tanh(x):
    return 0.5 * x * (1.0 + jnp.tanh(math.sqrt(2.0 / math.pi) * (x + 0.044715 * (x * x * x))))


def _cmp_kernel(kb_ref, pos_ref, w1_ref, w2_ref, o_ref, *, nb):
    half = CMP_STRIDE * HEAD_DIM
    kb = kb_ref[...].astype(F32)
    xa = (kb + pos_ref[0:1, :]).astype(BF16)
    xb = (kb + pos_ref[1:2, :]).astype(BF16)
    a = _dot(xa, w1_ref[0:half, :])
    b = _dot(xb, w1_ref[half:2 * half, :])
    pre = a + pltpu.roll(b, nb - 1, axis=0)
    out = _dot(_gelu_tanh(pre).astype(BF16), w2_ref[...])
    row = lax.broadcasted_iota(jnp.int32, out.shape, 0)
    out = jnp.where(row < nb - 1, out, 0.0)
    o_ref[0:CMP_PAD, :] = jnp.zeros((CMP_PAD, HEAD_DIM), F32)
    o_ref[CMP_PAD:CMP_PAD + nb, :] = out


def _compress(kvh, cmp_pos, cmp_w1, cmp_w2):
    t = kvh.shape[1]
    nb = t // CMP_STRIDE
    kb = kvh[:2 * NSA_GROUPS].reshape(2 * NSA_GROUPS, nb, CMP_STRIDE * HEAD_DIM)
    pos = cmp_pos.astype(F32).reshape(2, 2, CMP_STRIDE * HEAD_DIM)
    return pl.pallas_call(
        functools.partial(_cmp_kernel, nb=nb),
        grid=(2 * NSA_GROUPS,),
        in_specs=[pl.BlockSpec((None, nb, CMP_STRIDE * HEAD_DIM), lambda j: (j, 0, 0)),
                  pl.BlockSpec((None, 2, CMP_STRIDE * HEAD_DIM), lambda j: (j // NSA_GROUPS, 0, 0)),
                  pl.BlockSpec((None, CMP_BLOCK * HEAD_DIM, HEAD_DIM), lambda j: (j // NSA_GROUPS, 0, 0)),
                  pl.BlockSpec((None, HEAD_DIM, HEAD_DIM), lambda j: (j // NSA_GROUPS, 0, 0))],
        out_specs=pl.BlockSpec((None, CMP_PAD + nb, HEAD_DIM), lambda j: (j, 0, 0)),
        out_shape=jax.ShapeDtypeStruct((2 * NSA_GROUPS, CMP_PAD + nb, HEAD_DIM), F32),
        compiler_params=_cparams(("parallel",)),
        name="compress",
    )(kb, pos, cmp_w1.astype(BF16), cmp_w2.astype(BF16))


def _dot_split_rhs(w, x):
    hi, mid, lo = _split3(x)
    return _dot(w, hi) + _dot(w, mid) + _dot(w, lo)


def _nsa_kernel(q_ref, g_ref, kc_ref, vc_ref, vct_ref, ksa_ref, vst_ref, kw_ref, vwt_ref,
                bd_ref, bc_ref, ovl_ref, o_ref,
                qa_ref, s_ref, sc_ref, imp_ref, gate_ref, m_ref, acc_ref, out_ref):
    i = pl.program_id(1)
    rows = NSA_ROWS
    q = q_ref[...].reshape(rows, HEAD_DIM)
    tl_lane = lax.broadcasted_iota(jnp.int32, (Q_BLOCK, rows), 1) % Q_BLOCK
    key_row = lax.broadcasted_iota(jnp.int32, (Q_BLOCK, rows), 0)
    gate_ref[...] = jax.nn.sigmoid(g_ref[...]).T
    gate_row0 = 3 * NSA_HG * pl.program_id(0)

    def gate(branch):
        return jnp.concatenate([gate_ref[pl.ds(gate_row0 + 3 * hg + branch, 1), :] for hg in range(NSA_HG)], axis=1)

    def head_sum(p):
        acc = p[:, 0:Q_BLOCK]
        for hg in range(1, NSA_HG):
            acc = acc + p[:, hg * Q_BLOCK:(hg + 1) * Q_BLOCK]
        return acc

    n_first_near = 8 * i - (CMP_NEAR - 8)
    n_blocks = (jnp.maximum(n_first_near, 0) + CMP_ROWS - 1) // CMP_ROWS

    def far_rows(b):
        return pl.multiple_of(b * CMP_ROWS, CMP_ROWS)

    def scores_pass(b, mx):
        r0 = far_rows(b)
        kb = kc_ref[pl.ds(CMP_PAD + r0, CMP_ROWS), :].astype(BF16)
        row = r0 + lax.broadcasted_iota(jnp.int32, (CMP_ROWS, rows), 0)
        s = jnp.where(row < n_first_near, _dot_nt(kb, q), NEG_INF)
        sc_ref[pl.ds(r0, CMP_ROWS), :] = s
        return jnp.maximum(mx, jnp.max(s, axis=0, keepdims=True))

    mx_far = lax.fori_loop(0, n_blocks, scores_pass, jnp.full((1, rows), NEG_INF, F32))
    start = pl.multiple_of(8 * i + 8 + CMP_PAD - CMP_NEAR, 8)
    kcn = kc_ref[pl.ds(start, CMP_NEAR), :].astype(BF16)
    vcn_t = vc_ref[pl.ds(start, CMP_NEAR), :].T.astype(BF16)
    d_near = tl_lane - CMP_STRIDE * (key_row - (CMP_NEAR - 8)) - (CMP_BLOCK - 1)
    near_ok = (d_near >= 0) & (n_first_near + key_row >= 0)
    s_near = jnp.where(near_ok, _dot_nt(kcn, q) + bc_ref[...], NEG_INF)
    mx = jnp.maximum(mx_far, jnp.max(s_near, axis=0, keepdims=True))
    e_near = jnp.exp2(s_near - mx)

    def exp_pass(b, den):
        r0 = far_rows(b)
        e = jnp.exp2(sc_ref[pl.ds(r0, CMP_ROWS), :] - mx)
        sc_ref[pl.ds(r0, CMP_ROWS), :] = e
        return den + jnp.sum(e, axis=0, keepdims=True)

    den = lax.fori_loop(0, n_blocks, exp_pass, jnp.sum(e_near, axis=0, keepdims=True))
    inv = jnp.where(mx > 0.5 * NEG_INF, 1.0 / den, 0.0)
    p_near = e_near * inv
    sb = lax.broadcasted_iota(jnp.int32, (SEL_COLS, CMP_NEAR), 0)
    nn = n_first_near + lax.broadcasted_iota(jnp.int32, (SEL_COLS, CMP_NEAR), 1)
    ovl_near = jnp.where((nn >= 4 * sb - 1) & (nn <= 4 * sb + 3) & (nn >= 0), 1.0, 0.0).astype(BF16)
    out_ref[...] = _dot(vcn_t, p_near.astype(BF16))
    imp_ref[...] = _dot_split_rhs(ovl_near, head_sum(p_near))

    def out_pass(b, carry):
        r0 = far_rows(b)
        p = sc_ref[pl.ds(r0, CMP_ROWS), :] * inv
        out_ref[...] += _dot(vct_ref[:, pl.ds(CMP_PAD + r0, CMP_ROWS)].astype(BF16), p.astype(BF16))
        imp_ref[...] += _dot_split_rhs(ovl_ref[:, pl.ds(r0, CMP_ROWS)], head_sum(p))
        return carry

    lax.fori_loop(0, n_blocks, out_pass, 0)
    out_ref[...] = gate(0) * out_ref[...]

    imp = imp_ref[...]
    blk = lax.broadcasted_iota(jnp.int32, (SEL_COLS, Q_BLOCK), 0)
    tl = lax.broadcasted_iota(jnp.int32, (SEL_COLS, Q_BLOCK), 1)
    cur = 2 * i + (tl >= SEL_BLOCK).astype(jnp.int32)
    forced = (blk == 0) | (blk == cur) | (blk == cur - 1)
    causal_blk = blk * SEL_BLOCK <= Q_BLOCK * i + tl
    work = jnp.where(forced, -1.0, jnp.where(causal_blk, imp, -1.0))
    blk_f = blk.astype(F32)
    sel = jnp.where(forced, 1.0, 0.0)
    for _ in range(SEL_TOPK - 3):
        best = jnp.max(work, axis=0, keepdims=True)
        first = jnp.min(jnp.where(work == best, blk_f, float(SEL_COLS)), axis=0, keepdims=True)
        pick = blk_f == first
        sel = jnp.where(pick, 1.0, sel)
        work = jnp.where(pick, -2.0, work)
    amask = jnp.where(sel > 0.0, 0.0, NEG_INF)
    for half in range(SEL_COLS // LANE):
        a_t = amask[half * LANE:(half + 1) * LANE, :].T.astype(BF16)
        qa_ref[half, :, 0:HEAD_DIM] = q
        qa_ref[half, :, HEAD_DIM:2 * HEAD_DIM] = jnp.concatenate([a_t] * NSA_HG, axis=0)

    def sel_chunk(tile, n_keys):
        off = pl.multiple_of(tile * Q_BLOCK, Q_BLOCK)
        qa = qa_ref[tile // (LANE // 2)]
        return _dot_nt(ksa_ref[pl.ds(off, n_keys), :], qa), vst_ref[:, pl.ds(off, n_keys)]

    s_t, v_t = sel_chunk(i, Q_BLOCK)
    near = [(jnp.where(key_row <= tl_lane, s_t + bd_ref[Q_BLOCK:, :], NEG_INF), v_t)]
    s_t, v_t = sel_chunk(jnp.maximum(i - 1, 0), Q_BLOCK)
    near.append((jnp.where(i >= 1, s_t + bd_ref[:Q_BLOCK, :], NEG_INF), v_t))
    m_ref[...], acc_ref[...] = _softmax_tiles(near)
    n_far = jnp.maximum(i - 1, 0)
    n_chunks = n_far // FAR_TILES
    chunk_keys = FAR_TILES * Q_BLOCK

    def far_scores(c):
        tile = FAR_TILES * jnp.minimum(c, jnp.maximum(n_chunks - 1, 0))
        off = pl.multiple_of(tile * Q_BLOCK, chunk_keys)
        return _dot_nt(ksa_ref[pl.ds(off, chunk_keys), :], qa_ref[tile // (LANE // 2)])

    def far_consume(c, s_slot):
        off = pl.multiple_of(c * chunk_keys, chunk_keys)
        _flash_update_from(s_slot, vst_ref[:, pl.ds(off, chunk_keys)], m_ref, acc_ref)

    s_ref[0] = far_scores(0)
    _pipelined_chunks(n_chunks, far_scores, far_consume, s_ref)

    @pl.when(n_chunks % 2 == 1)
    def _():
        far_consume(n_chunks - 1, s_ref.at[0])

    left = n_far % FAR_TILES

    @pl.when(left >= 2)
    def _():
        s_t, v_t = sel_chunk(n_far - left, 2 * Q_BLOCK)
        _flash_update(s_t, v_t, m_ref, acc_ref)

    @pl.when(left % 2 == 1)
    def _():
        s_t, v_t = sel_chunk(n_far - 1, Q_BLOCK)
        _flash_update(s_t, v_t, m_ref, acc_ref)

    out_ref[...] += gate(1) * _normalized(acc_ref[...])

    def win_chunk(tile):
        off = pl.multiple_of(jnp.maximum(tile, 0) * Q_BLOCK, Q_BLOCK)
        return _dot_nt(kw_ref[pl.ds(off, Q_BLOCK), :], q), vwt_ref[:, pl.ds(off, Q_BLOCK)]

    s_t, v_t = win_chunk(i)
    win = [(jnp.where(key_row <= tl_lane, s_t + bd_ref[Q_BLOCK:, :], NEG_INF), v_t)]
    s_t, v_t = win_chunk(i - 1)
    win.append((jnp.where(i >= 1, s_t + bd_ref[:Q_BLOCK, :], NEG_INF), v_t))
    for back in (2, 3):
        s_t, v_t = win_chunk(i - back)
        win.append((jnp.where(i >= back, s_t, NEG_INF), v_t))
    s_t, v_t = win_chunk(i - 4)
    win.append((jnp.where((key_row > tl_lane) & (i >= 4), s_t, NEG_INF), v_t))
    _, acc_w = _softmax_tiles(win)
    o = out_ref[...] + gate(2) * _normalized(acc_w)
    for hg in range(NSA_HG):
        o_ref[:, hg * HEAD_DIM:(hg + 1) * HEAD_DIM] = o[:, hg * Q_BLOCK:(hg + 1) * Q_BLOCK].T.astype(o_ref.dtype)


def _nsa_attention(qkh, gates, kcv, kcv_t, ksa, vt, bd, bc, ovl):
    t = qkh.shape[1]
    ncp = t // CMP_STRIDE
    assert ncp % CMP_ROWS == 0
    resident = functools.partial(pl.BlockSpec, pipeline_mode=pl.Buffered(1))
    return pl.pallas_call(
        _nsa_kernel,
        grid=(NSA_GROUPS, t // Q_BLOCK),
        in_specs=[pl.BlockSpec((NSA_HG, Q_BLOCK, HEAD_DIM), lambda g, i: (g, i, 0)),
                  pl.BlockSpec((Q_BLOCK, LANE), lambda g, i: (i, 0)),
                  resident((None, CMP_PAD + ncp, HEAD_DIM), lambda g, i: (g, 0, 0)),
                  resident((None, CMP_PAD + ncp, HEAD_DIM), lambda g, i: (NSA_GROUPS + g, 0, 0)),
                  resident((None, HEAD_DIM, CMP_PAD + ncp), lambda g, i: (NSA_GROUPS + g, 0, 0)),
                  resident((None, t, 2 * HEAD_DIM), lambda g, i: (g, 0, 0)),
                  resident((None, V_ROWS, t), lambda g, i: (g, 0, 0)),
                  resident((None, t, HEAD_DIM), lambda g, i: (N_HEADS + 3 * NSA_GROUPS + g, 0, 0)),
                  resident((None, V_ROWS, t), lambda g, i: (NSA_GROUPS + g, 0, 0)),
                  resident((2 * Q_BLOCK, NSA_ROWS), lambda g, i: (0, g)),
                  resident((CMP_NEAR, NSA_ROWS), lambda g, i: (0, g)),
                  resident((SEL_COLS, ncp), lambda g, i: (0, 0))],
        out_specs=pl.BlockSpec((Q_BLOCK, NSA_HG * HEAD_DIM), lambda g, i: (i, g)),
        out_shape=jax.ShapeDtypeStruct((t, D_MODEL), BF16),
        scratch_shapes=[pltpu.VMEM((SEL_COLS // LANE, NSA_ROWS, 2 * HEAD_DIM), BF16),
                        pltpu.VMEM((2, FAR_TILES * Q_BLOCK, NSA_ROWS), F32),
                        pltpu.VMEM((ncp, NSA_ROWS), F32), pltpu.VMEM((SEL_COLS, Q_BLOCK), F32),
                        pltpu.VMEM((LANE, Q_BLOCK), F32), pltpu.VMEM((1, NSA_ROWS), F32),
                        pltpu.VMEM((V_ROWS, NSA_ROWS), F32), pltpu.VMEM((HEAD_DIM, NSA_ROWS), F32)],
        compiler_params=_cparams(("parallel", "arbitrary")),
        name="nsa_attention",
    )(qkh, gates, kcv, kcv, kcv_t, ksa, vt, qkh, vt, bd, bc, ovl)


def _nsa_layer(h, g_norm, w_in, w_o, layer, cmp_pos, cmp_w1, cmp_w2, bd, bc):
    t = h.shape[0]
    assert t // SEL_BLOCK <= SEL_COLS
    a = _rmsnorm(h, g_norm, BF16)
    tn = NSA_KV_DIM
    q_blocks = D_MODEL // tn
    qkh = _proj(a, w_in, layer, n_out=D_MODEL + 4 * NSA_KV_DIM, out_dtype=BF16, head_major=True, tn=tn,
                n_scaled=D_MODEL,
                col_block=lambda j: j + jnp.where(j >= q_blocks + 3, 1, 0))
    vt = _proj_t(a, w_in, layer, n_out=2 * NSA_KV_DIM, tn=tn, col_block=lambda j: q_blocks + 3 + 2 * j)
    gates = _proj(a, w_in, layer, n_out=LANE, out_dtype=F32, head_major=False, tn=LANE,
                  col_block=lambda j: (D_MODEL + 6 * NSA_KV_DIM) // LANE + j,
                  valid_cols=3 * N_HEADS)
    kcv = _compress(qkh[N_HEADS:N_HEADS + 2 * NSA_GROUPS], cmp_pos, cmp_w1, cmp_w2)
    kcv_t = jnp.swapaxes(kcv, 1, 2)
    key_blk = (jnp.arange(t) // SEL_BLOCK) % LANE
    onehot = (key_blk[:, None] == jnp.arange(LANE)[None, :]).astype(BF16)
    ksa = jnp.concatenate([qkh[N_HEADS + 2 * NSA_GROUPS:N_HEADS + 3 * NSA_GROUPS],
                           jnp.broadcast_to(onehot, (NSA_GROUPS, t, LANE))], axis=-1)
    n = jnp.arange(t // CMP_STRIDE)[None, :]
    sblk = jnp.arange(SEL_COLS)[:, None]
    ovl = ((n >= 4 * sblk - 1) & (n <= 4 * sblk + 3)).astype(BF16)
    o = _nsa_attention(qkh, gates, kcv, kcv_t, ksa, vt, bd, bc, ovl)
    return _mm_res(o, w_o, layer, h, tile=WO_TILE)


def kernel(x, norm_mix, norm_ffn, norm_final, rel_table, nsa_w_in, nsa_w_o, nsa_cmp_pos, nsa_cmp_w1,
           nsa_cmp_w2, fox_w_in, fox_b_f, fox_w_o, ffn_w_gate, ffn_w_up, ffn_w_down):
    depth = norm_mix.shape[0]
    bd, bc = _nsa_bias_templates(rel_table.astype(F32))
    nsa_w_in, fox_w_in = jnp.swapaxes(nsa_w_in.astype(F32), 1, 2), jnp.swapaxes(fox_w_in.astype(F32), 1, 2)
    ffn_w_gate, ffn_w_up = ffn_w_gate.astype(F32), ffn_w_up.astype(F32)
    nsa_w_o, fox_w_o, ffn_w_down = nsa_w_o.astype(BF16), fox_w_o.astype(BF16), ffn_w_down.astype(BF16)
    outs = []
    for bi in range(x.shape[0]):
        h = x[bi]
        for i in range(depth):
            j = i // 2
            if i % 2 == 0:
                h = _nsa_layer(h, norm_mix[i], nsa_w_in, nsa_w_o, j, nsa_cmp_pos[j], nsa_cmp_w1[j],
                               nsa_cmp_w2[j], bd, bc)
            else:
                h = _fox_layer(h, norm_mix[i], fox_w_in, fox_b_f[j], fox_w_o, j)
            h = _ffn(h, norm_ffn[i], ffn_w_gate, ffn_w_up, ffn_w_down, i)
        outs.append(_rmsnorm(h, norm_final, F32))
    return jnp.stack(outs, axis=0)
```

```python
import functools
import math

import jax
import jax.numpy as jnp
from jax import lax
from jax.experimental import pallas as pl
from jax.experimental.pallas import tpu as pltpu

F32 = jnp.float32
BF16 = jnp.bfloat16

D_MODEL = 4096
HEAD_DIM = 128
N_HEADS = D_MODEL // HEAD_DIM
NSA_GROUPS = 4
NSA_HG = N_HEADS // NSA_GROUPS
NSA_KV_DIM = NSA_GROUPS * HEAD_DIM
CMP_BLOCK = 32
CMP_STRIDE = 16
SEL_BLOCK = 64
SEL_TOPK = 16
WINDOW = 512
REL_BUCKETS = 32
REL_MAX_DIST = 128
Q_BLOCK = 128
RMS_EPS = 1e-6
NEG_INF = -1e30
LOG2E = 1.4426950408889634
Q_SCALE = HEAD_DIM ** -0.5 * LOG2E

LANE = 128
SEL_COLS = 256
CMP_PAD = 128
CMP_NEAR = 128
CMP_ROWS = 256
FAR_TILES = 4
NSA_ROWS = NSA_HG * Q_BLOCK
V_ROWS = HEAD_DIM + 16
KEY_BLOCK = 256
N_PIECES = 3

RMS_ROWS = 256
PROJ_TILE = (1024, 512)
GATEUP_TILE = (2048, 256)
DOWN_TILE = (512, 512)
WO_TILE = (1024, 512)
FOX_Q_TILE = 1024
CUM_ROWS = 512
VMEM_LIMIT = 56 * 1024 * 1024


def _cparams(sem):
    return pltpu.CompilerParams(dimension_semantics=sem, vmem_limit_bytes=VMEM_LIMIT)


def _dot(a, b):
    return jnp.dot(a, b, preferred_element_type=F32)


def _dot_nt(a, b):
    return lax.dot_general(a, b, (((1,), (1,)), ((), ())), preferred_element_type=F32)


def _rmsnorm_kernel(x_ref, g_ref, o_ref):
    x = x_ref[...]
    ms = jnp.mean(x * x, axis=-1, keepdims=True)
    o_ref[...] = (x * lax.rsqrt(ms + RMS_EPS) * g_ref[...]).astype(o_ref.dtype)


def _rmsnorm(x, g, out_dtype):
    t, d = x.shape
    tm = min(RMS_ROWS, t)
    return pl.pallas_call(
        _rmsnorm_kernel,
        grid=(t // tm,),
        in_specs=[pl.BlockSpec((tm, d), lambda i: (i, 0)),
                  pl.BlockSpec((1, d), lambda i: (0, 0))],
        out_specs=pl.BlockSpec((tm, d), lambda i: (i, 0)),
        out_shape=jax.ShapeDtypeStruct((t, d), out_dtype),
        compiler_params=_cparams(("parallel",)),
        name="rmsnorm",
    )(x, g.reshape(1, d))


def _proj_kernel(x_ref, w_ref, o_ref, *, head_major, n_scaled_blocks, valid_cols):
    r = _dot_nt(x_ref[...], w_ref[...].astype(BF16))
    if valid_cols is not None:
        r = jnp.where(lax.broadcasted_iota(jnp.int32, r.shape, 1) < valid_cols, r, 0.0)
    if n_scaled_blocks:
        r = r * jnp.where(pl.program_id(1) < n_scaled_blocks, Q_SCALE, 1.0)
    if head_major:
        for s in range(o_ref.shape[0]):
            o_ref[s] = r[:, s * LANE:(s + 1) * LANE].astype(o_ref.dtype)
    else:
        o_ref[...] = r.astype(o_ref.dtype)


def _identity(j):
    return j


_row_block = functools.partial(pl.BlockSpec, pipeline_mode=pl.Buffered(1))


def _proj(x, w, layer, *, n_out, out_dtype, head_major, tn=PROJ_TILE[1], col_block=_identity, n_scaled=0,
          valid_cols=None):
    t, k = x.shape
    n = n_out
    tm = min(PROJ_TILE[0], t)
    tn = min(tn, n)
    assert n_scaled % tn == 0 and n % tn == 0
    if head_major:
        out_shape = jax.ShapeDtypeStruct((n // LANE, t, LANE), out_dtype)
        out_spec = pl.BlockSpec((tn // LANE, tm, LANE), lambda i, j: (j, i, 0))
    else:
        out_shape = jax.ShapeDtypeStruct((t, n), out_dtype)
        out_spec = pl.BlockSpec((tm, tn), lambda i, j: (i, j))
    return pl.pallas_call(
        functools.partial(_proj_kernel, head_major=head_major, n_scaled_blocks=n_scaled // tn,
                          valid_cols=valid_cols),
        grid=(t // tm, n // tn),
        in_specs=[pl.BlockSpec((tm, k), lambda i, j: (i, 0)),
                  pl.BlockSpec((None, tn, k), lambda i, j: (layer, col_block(j), 0))],
        out_specs=out_spec,
        out_shape=out_shape,
        compiler_params=_cparams(("parallel", "parallel")),
        name="proj",
    )(x, w)


def _proj_t_kernel(x_ref, w_ref, o_ref):
    r = _dot_nt(x_ref[...], w_ref[...].astype(BF16))
    ones = _ones_rows(r.shape[0], o_ref.dtype)
    for s in range(o_ref.shape[0]):
        o_ref[s, 0:HEAD_DIM, :] = r[:, s * LANE:(s + 1) * LANE].T.astype(o_ref.dtype)
        o_ref[s, HEAD_DIM:, :] = ones


def _proj_t(x, w, layer, *, n_out, tn=PROJ_TILE[1], col_block=_identity):
    t, k = x.shape
    n = n_out
    tm = min(PROJ_TILE[0], t)
    tn = min(tn, n)
    return pl.pallas_call(
        _proj_t_kernel,
        grid=(t // tm, n // tn),
        in_specs=[pl.BlockSpec((tm, k), lambda i, j: (i, 0)),
                  pl.BlockSpec((None, tn, k), lambda i, j: (layer, col_block(j), 0))],
        out_specs=pl.BlockSpec((tn // LANE, V_ROWS, tm), lambda i, j: (j, 0, i)),
        out_shape=jax.ShapeDtypeStruct((n // LANE, V_ROWS, t), BF16),
        compiler_params=_cparams(("parallel", "parallel")),
        name="proj_t",
    )(x, w)


def _mm_res_kernel(x_ref, w_ref, r_ref, o_ref):
    o_ref[...] = r_ref[...] + _dot(x_ref[...], w_ref[...])


def _mm_res(x, w, layer, res, *, tile, x_spec=pl.BlockSpec):
    t, k = x.shape
    n = w.shape[2]
    tm, tn = min(tile[0], t), tile[1]
    return pl.pallas_call(
        _mm_res_kernel,
        grid=(t // tm, n // tn),
        in_specs=[x_spec((tm, k), lambda i, j: (i, 0)),
                  pl.BlockSpec((None, k, tn), lambda i, j: (layer, 0, j)),
                  pl.BlockSpec((tm, tn), lambda i, j: (i, j))],
        out_specs=pl.BlockSpec((tm, tn), lambda i, j: (i, j)),
        out_shape=jax.ShapeDtypeStruct((t, n), F32),
        compiler_params=_cparams(("parallel", "parallel")),
        name="mm_res",
    )(x, w, res)


def _gateup_kernel(x_ref, wg_ref, wu_ref, o_ref):
    x = x_ref[...]
    g = _dot(x, wg_ref[...].astype(BF16))
    u = _dot(x, wu_ref[...].astype(BF16))
    o_ref[...] = (g * jax.nn.sigmoid(g) * u).astype(o_ref.dtype)


def _gateup(x, wg, wu, layer):
    t, k = x.shape
    n = wg.shape[2]
    tm, tn = min(GATEUP_TILE[0], t), GATEUP_TILE[1]
    return pl.pallas_call(
        _gateup_kernel,
        grid=(t // tm, n // tn),
        in_specs=[_row_block((tm, k), lambda i, j: (i, 0)),
                  pl.BlockSpec((None, k, tn), lambda i, j: (layer, 0, j)),
                  pl.BlockSpec((None, k, tn), lambda i, j: (layer, 0, j))],
        out_specs=pl.BlockSpec((tm, tn), lambda i, j: (i, j)),
        out_shape=jax.ShapeDtypeStruct((t, n), BF16),
        compiler_params=_cparams(("parallel", "parallel")),
        name="gateup",
    )(x, wg, wu)


def _ffn(h, g_norm, w_gate, w_up, w_down, layer):
    a = _rmsnorm(h, g_norm, BF16)
    h1 = _gateup(a, w_gate, w_up, layer)
    return _mm_res(h1, w_down, layer, h, tile=DOWN_TILE)


def _ones_rows(n_cols, dtype):
    row = lax.broadcasted_iota(jnp.int32, (V_ROWS - HEAD_DIM, n_cols), 0)
    return jnp.where(row == 0, 1.0, 0.0).astype(dtype)


def _flash_init(m_ref, acc_ref):
    m_ref[...] = jnp.full(m_ref.shape, NEG_INF, F32)
    acc_ref[...] = jnp.zeros(acc_ref.shape, F32)


def _flash_update(s_t, v_t, m_ref, acc_ref):
    m_prev = m_ref[...]
    m_new = jnp.maximum(m_prev, jnp.max(s_t, axis=0, keepdims=True))
    alpha = jnp.exp2(m_prev - m_new)
    p = jnp.exp2(s_t - m_new)
    acc_ref[...] = alpha * acc_ref[...] + _dot(v_t, p.astype(BF16))
    m_ref[...] = m_new


def _flash_update_from(s_slot, v_t, m_ref, acc_ref):
    n_keys = s_slot.shape[0]
    m_prev = m_ref[...]
    m_new = jnp.maximum(m_prev, jnp.max(s_slot[...], axis=0, keepdims=True))
    acc = jnp.exp2(m_prev - m_new) * acc_ref[...]
    for r in range(0, n_keys, KEY_BLOCK):
        p = jnp.exp2(s_slot[r:r + KEY_BLOCK, :] - m_new)
        acc = acc + _dot(v_t[:, r:r + KEY_BLOCK], p.astype(BF16))
    acc_ref[...] = acc
    m_ref[...] = m_new


def _softmax_tiles(tiles):
    mx = functools.reduce(jnp.maximum, [jnp.max(s_t, axis=0, keepdims=True) for s_t, _ in tiles])
    acc = None
    for s_t, v_t in tiles:
        a = _dot(v_t, jnp.exp2(s_t - mx).astype(BF16))
        acc = a if acc is None else acc + a
    return mx, acc


def _normalized(acc):
    return acc[0:HEAD_DIM] / acc[HEAD_DIM:HEAD_DIM + 1]


def _split3(x):
    hi = x.astype(BF16)
    r = x - hi.astype(F32)
    mid = r.astype(BF16)
    lo = (r - mid.astype(F32)).astype(BF16)
    return hi, mid, lo


def _cum_kernel(f_ref, b_ref, tri_ref, place_ref, qx_ref, kx_ref, carry_ref):
    @pl.when(pl.program_id(0) == 0)
    def _():
        carry_ref[...] = jnp.zeros_like(carry_ref)

    head_lane = lax.broadcasted_iota(jnp.int32, f_ref.shape, 1) < N_HEADS
    x = jnp.where(head_lane, f_ref[...] + b_ref[...], 0.0)
    log_f = jnp.minimum(x, 0.0) - jnp.log1p(jnp.exp(-jnp.abs(x)))
    c = jnp.dot(tri_ref[...], log_f, precision=lax.Precision.HIGHEST,
                preferred_element_type=F32) + carry_ref[...]
    carry_ref[...] = c[c.shape[0] - 1:, :]
    pieces = jnp.concatenate(_split3(c * LOG2E), axis=1)
    lane = lax.broadcasted_iota(jnp.int32, (c.shape[0], LANE), 1)
    q_const = jnp.where((lane >= N_PIECES) & (lane < 2 * N_PIECES), -1.0, 0.0)
    k_const = jnp.where(lane < N_PIECES, 1.0, 0.0)
    for h in range(N_HEADS):
        placed = _dot(pieces, place_ref[h])
        qx_ref[h] = (placed[:, :LANE] + q_const).astype(BF16)
        kx_ref[h] = (placed[:, LANE:] + k_const).astype(BF16)


def _forget_extras(f, b):
    t, n = f.shape
    tb = min(CUM_ROWS, t)
    tri = jnp.tril(jnp.ones((tb, tb), F32))
    h = jnp.arange(N_HEADS)[:, None, None]
    row = jnp.arange(N_PIECES * LANE)[None, :, None]
    col = jnp.arange(2 * LANE)[None, None, :]
    piece, head = row // LANE, row % LANE
    place = ((head == h) & ((col == piece) | (col == LANE + N_PIECES + piece))).astype(BF16)
    out = jax.ShapeDtypeStruct((N_HEADS, t, LANE), BF16)
    return pl.pallas_call(
        _cum_kernel,
        grid=(t // tb,),
        in_specs=[pl.BlockSpec((tb, n), lambda i: (i, 0)),
                  pl.BlockSpec((1, n), lambda i: (0, 0)),
                  pl.BlockSpec((tb, tb), lambda i: (0, 0)),
                  pl.BlockSpec((N_HEADS, N_PIECES * LANE, 2 * LANE), lambda i: (0, 0, 0))],
        out_specs=[pl.BlockSpec((N_HEADS, tb, LANE), lambda i: (0, i, 0)),
                   pl.BlockSpec((N_HEADS, tb, LANE), lambda i: (0, i, 0))],
        out_shape=[out, out],
        scratch_shapes=[pltpu.VMEM((1, n), F32)],
        compiler_params=_cparams(("arbitrary",)),
        name="forget_extras",
    )(f, b, tri, place)


def _pipelined_loop(first, trips, per_trip, scores, consume, s_ref):
    def body(t, carry):
        c = first + t * per_trip
        for u in range(per_trip):
            s_next = scores(c + u + 1)
            consume(c + u, s_ref.at[u % 2])
            s_ref[(u + 1) % 2] = s_next
        return carry

    lax.fori_loop(0, trips, body, 0)


def _pipelined_chunks(n_chunks, scores, consume, s_ref):
    done = 0
    for per_trip in (8, 4, 2):
        trips = (n_chunks - done) // per_trip
        _pipelined_loop(done, trips, per_trip, scores, consume, s_ref)
        done = done + trips * per_trip


def _fox_kernel(q_ref, qx_ref, k_ref, kx_ref, vt_ref, o_ref, ka_ref, qa_ref, s_ref, m_ref, acc_ref, *, tq, tk):
    i = pl.program_id(1)
    assert tq == 2 * tk

    @pl.when(i == 0)
    def _():
        ka_ref[:, 0:HEAD_DIM] = k_ref[...]
        ka_ref[:, HEAD_DIM:] = kx_ref[...]

    qa_ref[:, 0:HEAD_DIM] = q_ref[...]
    qa_ref[:, HEAD_DIM:] = qx_ref[...]
    _flash_init(m_ref, acc_ref)

    def scores(c):
        off = pl.multiple_of(c * tk, tk)
        return _dot_nt(ka_ref[pl.ds(off, tk), :], qa_ref[...])

    def values(c):
        return vt_ref[:, pl.ds(pl.multiple_of(c * tk, tk), tk)]

    def consume(c, s_slot):
        _flash_update_from(s_slot, values(c), m_ref, acc_ref)

    n_full = 2 * i
    s_ref[0] = scores(0)
    _pipelined_chunks(n_full, scores, consume, s_ref)
    key = lax.broadcasted_iota(jnp.int32, (tk, tq), 0)
    qry = lax.broadcasted_iota(jnp.int32, (tk, tq), 1)
    s_last = scores(n_full + 1)
    _flash_update(jnp.where(key <= qry, s_ref[0], NEG_INF), values(n_full), m_ref, acc_ref)
    _flash_update(jnp.where(key + tk <= qry, s_last, NEG_INF), values(n_full + 1), m_ref, acc_ref)
    o_ref[...] = _normalized(acc_ref[...]).T.astype(o_ref.dtype)


def _fox_attention(qkh, qx, kx, vt):
    t = qkh.shape[1]
    tq = min(FOX_Q_TILE, t)
    tk = tq // 2
    return pl.pallas_call(
        functools.partial(_fox_kernel, tq=tq, tk=tk),
        grid=(N_HEADS, t // tq),
        in_specs=[pl.BlockSpec((None, tq, HEAD_DIM), lambda h, i: (h, i, 0)),
                  pl.BlockSpec((None, tq, LANE), lambda h, i: (h, i, 0)),
                  pl.BlockSpec((None, t, HEAD_DIM), lambda h, i: (N_HEADS + h, 0, 0)),
                  pl.BlockSpec((None, t, LANE), lambda h, i: (h, 0, 0)),
                  pl.BlockSpec((None, V_ROWS, t), lambda h, i: (h, 0, 0))],
        out_specs=pl.BlockSpec((tq, HEAD_DIM), lambda h, i: (i, h)),
        out_shape=jax.ShapeDtypeStruct((t, D_MODEL), BF16),
        scratch_shapes=[pltpu.VMEM((t, 2 * HEAD_DIM), BF16),
                        pltpu.VMEM((tq, 2 * HEAD_DIM), BF16), pltpu.VMEM((2, tk, tq), F32),
                        pltpu.VMEM((1, tq), F32), pltpu.VMEM((V_ROWS, tq), F32)],
        compiler_params=_cparams(("parallel", "arbitrary")),
        name="fox_attention",
    )(qkh, qx, qkh, kx, vt)


def _fox_layer(h, g_norm, w_in, b_f, w_o, layer):
    a = _rmsnorm(h, g_norm, BF16)
    tn = PROJ_TILE[1]
    qkh = _proj(a, w_in, layer, n_out=2 * D_MODEL, out_dtype=BF16, head_major=True, tn=tn,
                n_scaled=D_MODEL)
    vt = _proj_t(a, w_in, layer, n_out=D_MODEL, tn=tn, col_block=lambda j: 2 * D_MODEL // tn + j)
    f = _proj(a, w_in, layer, n_out=LANE, out_dtype=F32, head_major=False, tn=LANE,
              col_block=lambda j: 3 * D_MODEL // LANE + j, valid_cols=N_HEADS)
    b = jnp.pad(b_f.astype(F32), (0, LANE - N_HEADS)).reshape(1, LANE)
    qx, kx = _forget_extras(f, b)
    o = _fox_attention(qkh, qx, kx, vt)
    return _mm_res(o, w_o, layer, h, tile=WO_TILE)


def _rel_bucket_const(dist):
    n = jnp.maximum(dist, 0)
    max_exact = REL_BUCKETS // 2
    nf = jnp.maximum(n, max_exact).astype(F32)
    large = max_exact + (jnp.log(nf / max_exact) / math.log(REL_MAX_DIST / max_exact)
                         * (REL_BUCKETS - max_exact)).astype(jnp.int32)
    large = jnp.minimum(large, REL_BUCKETS - 1)
    return jnp.where(n < max_exact, n, large)


def _bias_kernel(tab_ref, bkt_ref, o_ref):
    h = pl.program_id(0)
    bkt = bkt_ref[...]
    acc = jnp.zeros(bkt.shape, F32)
    for b in range(REL_BUCKETS):
        acc = jnp.where(bkt == b, tab_ref[b, h], acc)
    o_ref[...] = (acc - tab_ref[REL_BUCKETS - 1, h]) * LOG2E


def _bias_template(rel_table, bkt):
    r, c = bkt.shape
    return pl.pallas_call(
        _bias_kernel,
        grid=(N_HEADS,),
        in_specs=[pl.BlockSpec(memory_space=pltpu.SMEM),
                  pl.BlockSpec((r, c), lambda h: (0, 0))],
        out_specs=pl.BlockSpec((r, c), lambda h: (0, h)),
        out_shape=jax.ShapeDtypeStruct((r, N_HEADS * c), F32),
        compiler_params=_cparams(("arbitrary",)),
        name="bias_template",
    )(rel_table, bkt)


def _nsa_bias_templates(rel_table):
    tl = jnp.arange(Q_BLOCK)[None, :]
    bd = _bias_template(rel_table, _rel_bucket_const(tl + Q_BLOCK - jnp.arange(2 * Q_BLOCK)[:, None]))
    m = jnp.arange(CMP_NEAR)[:, None]
    bc = _bias_template(rel_table, _rel_bucket_const(
        tl - CMP_STRIDE * (m - (CMP_NEAR - 8)) - (CMP_BLOCK - 1)))
    return bd, bc


def _gelu_tanh(x):
    return 0.5 * x * (1.0 + jnp.tanh(math.sqrt(2.0 / math.pi) * (x + 0.044715 * (x * x * x))))


def _cmp_kernel(kb_ref, pos_ref, w1_ref, w2_ref, o_ref, *, nb):
    half = CMP_STRIDE * HEAD_DIM
    kb = kb_ref[...].astype(F32)
    xa = (kb + pos_ref[0:1, :]).astype(BF16)
    xb = (kb + pos_ref[1:2, :]).astype(BF16)
    a = _dot(xa, w1_ref[0:half, :])
    b = _dot(xb, w1_ref[half:2 * half, :])
    pre = a + pltpu.roll(b, nb - 1, axis=0)
    out = _dot(_gelu_tanh(pre).astype(BF16), w2_ref[...])
    row = lax.broadcasted_iota(jnp.int32, out.shape, 0)
    out = jnp.where(row < nb - 1, out, 0.0)
    o_ref[0:CMP_PAD, :] = jnp.zeros((CMP_PAD, HEAD_DIM), F32)
    o_ref[CMP_PAD:CMP_PAD + nb, :] = out


def _compress(kvh, cmp_pos, cmp_w1, cmp_w2):
    t = kvh.shape[1]
    nb = t // CMP_STRIDE
    kb = kvh[:2 * NSA_GROUPS].reshape(2 * NSA_GROUPS, nb, CMP_STRIDE * HEAD_DIM)
    pos = cmp_pos.astype(F32).reshape(2, 2, CMP_STRIDE * HEAD_DIM)
    return pl.pallas_call(
        functools.partial(_cmp_kernel, nb=nb),
        grid=(2 * NSA_GROUPS,),
        in_specs=[pl.BlockSpec((None, nb, CMP_STRIDE * HEAD_DIM), lambda j: (j, 0, 0)),
                  pl.BlockSpec((None, 2, CMP_STRIDE * HEAD_DIM), lambda j: (j // NSA_GROUPS, 0, 0)),
                  pl.BlockSpec((None, CMP_BLOCK * HEAD_DIM, HEAD_DIM), lambda j: (j // NSA_GROUPS, 0, 0)),
                  pl.BlockSpec((None, HEAD_DIM, HEAD_DIM), lambda j: (j // NSA_GROUPS, 0, 0))],
        out_specs=pl.BlockSpec((None, CMP_PAD + nb, HEAD_DIM), lambda j: (j, 0, 0)),
        out_shape=jax.ShapeDtypeStruct((2 * NSA_GROUPS, CMP_PAD + nb, HEAD_DIM), F32),
        compiler_params=_cparams(("parallel",)),
        name="compress",
    )(kb, pos, cmp_w1.astype(BF16), cmp_w2.astype(BF16))


def _dot_split_rhs(w, x):
    hi, mid, lo = _split3(x)
    return _dot(w, hi) + _dot(w, mid) + _dot(w, lo)


def _nsa_kernel(q_ref, g_ref, kc_ref, vc_ref, vct_ref, ksa_ref, vst_ref, kw_ref, vwt_ref,
                bd_ref, bc_ref, ovl_ref, o_ref,
                qa_ref, s_ref, sc_ref, imp_ref, gate_ref, m_ref, acc_ref, out_ref):
    i = pl.program_id(1)
    rows = NSA_ROWS
    q = q_ref[...].reshape(rows, HEAD_DIM)
    tl_lane = lax.broadcasted_iota(jnp.int32, (Q_BLOCK, rows), 1) % Q_BLOCK
    key_row = lax.broadcasted_iota(jnp.int32, (Q_BLOCK, rows), 0)
    gate_ref[...] = jax.nn.sigmoid(g_ref[...]).T
    gate_row0 = 3 * NSA_HG * pl.program_id(0)

    def gate(branch):
        return jnp.concatenate([gate_ref[pl.ds(gate_row0 + 3 * hg + branch, 1), :] for hg in range(NSA_HG)], axis=1)

    def head_sum(p):
        acc = p[:, 0:Q_BLOCK]
        for hg in range(1, NSA_HG):
            acc = acc + p[:, hg * Q_BLOCK:(hg + 1) * Q_BLOCK]
        return acc

    n_first_near = 8 * i - (CMP_NEAR - 8)
    n_blocks = (jnp.maximum(n_first_near, 0) + CMP_ROWS - 1) // CMP_ROWS

    def far_rows(b):
        return pl.multiple_of(b * CMP_ROWS, CMP_ROWS)

    def scores_pass(b, mx):
        r0 = far_rows(b)
        kb = kc_ref[pl.ds(CMP_PAD + r0, CMP_ROWS), :].astype(BF16)
        row = r0 + lax.broadcasted_iota(jnp.int32, (CMP_ROWS, rows), 0)
        s = jnp.where(row < n_first_near, _dot_nt(kb, q), NEG_INF)
        sc_ref[pl.ds(r0, CMP_ROWS), :] = s
        return jnp.maximum(mx, jnp.max(s, axis=0, keepdims=True))

    mx_far = lax.fori_loop(0, n_blocks, scores_pass, jnp.full((1, rows), NEG_INF, F32))
    start = pl.multiple_of(8 * i + 8 + CMP_PAD - CMP_NEAR, 8)
    kcn = kc_ref[pl.ds(start, CMP_NEAR), :].astype(BF16)
    vcn_t = vc_ref[pl.ds(start, CMP_NEAR), :].T.astype(BF16)
    d_near = tl_lane - CMP_STRIDE * (key_row - (CMP_NEAR - 8)) - (CMP_BLOCK - 1)
    near_ok = (d_near >= 0) & (n_first_near + key_row >= 0)
    s_near = jnp.where(near_ok, _dot_nt(kcn, q) + bc_ref[...], NEG_INF)
    mx = jnp.maximum(mx_far, jnp.max(s_near, axis=0, keepdims=True))
    e_near = jnp.exp2(s_near - mx)

    def exp_pass(b, den):
        r0 = far_rows(b)
        e = jnp.exp2(sc_ref[pl.ds(r0, CMP_ROWS), :] - mx)
        sc_ref[pl.ds(r0, CMP_ROWS), :] = e
        return den + jnp.sum(e, axis=0, keepdims=True)

    den = lax.fori_loop(0, n_blocks, exp_pass, jnp.sum(e_near, axis=0, keepdims=True))
    inv = jnp.where(mx > 0.5 * NEG_INF, 1.0 / den, 0.0)
    p_near = e_near * inv
    sb = lax.broadcasted_iota(jnp.int32, (SEL_COLS, CMP_NEAR), 0)
    nn = n_first_near + lax.broadcasted_iota(jnp.int32, (SEL_COLS, CMP_NEAR), 1)
    ovl_near = jnp.where((nn >= 4 * sb - 1) & (nn <= 4 * sb + 3) & (nn >= 0), 1.0, 0.0).astype(BF16)
    out_ref[...] = _dot(vcn_t, p_near.astype(BF16))
    imp_ref[...] = _dot_split_rhs(ovl_near, head_sum(p_near))

    def out_pass(b, carry):
        r0 = far_rows(b)
        p = sc_ref[pl.ds(r0, CMP_ROWS), :] * inv
        out_ref[...] += _dot(vct_ref[:, pl.ds(CMP_PAD + r0, CMP_ROWS)].astype(BF16), p.astype(BF16))
        imp_ref[...] += _dot_split_rhs(ovl_ref[:, pl.ds(r0, CMP_ROWS)], head_sum(p))
        return carry

    lax.fori_loop(0, n_blocks, out_pass, 0)
    out_ref[...] = gate(0) * out_ref[...]

    imp = imp_ref[...]
    blk = lax.broadcasted_iota(jnp.int32, (SEL_COLS, Q_BLOCK), 0)
    tl = lax.broadcasted_iota(jnp.int32, (SEL_COLS, Q_BLOCK), 1)
    cur = 2 * i + (tl >= SEL_BLOCK).astype(jnp.int32)
    forced = (blk == 0) | (blk == cur) | (blk == cur - 1)
    causal_blk = blk * SEL_BLOCK <= Q_BLOCK * i + tl
    work = jnp.where(forced, -1.0, jnp.where(causal_blk, imp, -1.0))
    blk_f = blk.astype(F32)
    sel = jnp.where(forced, 1.0, 0.0)
    for _ in range(SEL_TOPK - 3):
        best = jnp.max(work, axis=0, keepdims=True)
        first = jnp.min(jnp.where(work == best, blk_f, float(SEL_COLS)), axis=0, keepdims=True)
        pick = blk_f == first
        sel = jnp.where(pick, 1.0, sel)
        work = jnp.where(pick, -2.0, work)
    amask = jnp.where(sel > 0.0, 0.0, NEG_INF)
    for half in range(SEL_COLS // LANE):
        a_t = amask[half * LANE:(half + 1) * LANE, :].T.astype(BF16)
        qa_ref[half, :, 0:HEAD_DIM] = q
        qa_ref[half, :, HEAD_DIM:2 * HEAD_DIM] = jnp.concatenate([a_t] * NSA_HG, axis=0)

    def sel_chunk(tile, n_keys):
        off = pl.multiple_of(tile * Q_BLOCK, Q_BLOCK)
        qa = qa_ref[tile // (LANE // 2)]
        return _dot_nt(ksa_ref[pl.ds(off, n_keys), :], qa), vst_ref[:, pl.ds(off, n_keys)]

    s_t, v_t = sel_chunk(i, Q_BLOCK)
    near = [(jnp.where(key_row <= tl_lane, s_t + bd_ref[Q_BLOCK:, :], NEG_INF), v_t)]
    s_t, v_t = sel_chunk(jnp.maximum(i - 1, 0), Q_BLOCK)
    near.append((jnp.where(i >= 1, s_t + bd_ref[:Q_BLOCK, :], NEG_INF), v_t))
    m_ref[...], acc_ref[...] = _softmax_tiles(near)
    n_far = jnp.maximum(i - 1, 0)
    n_chunks = n_far // FAR_TILES
    chunk_keys = FAR_TILES * Q_BLOCK

    def far_scores(c):
        tile = FAR_TILES * jnp.minimum(c, jnp.maximum(n_chunks - 1, 0))
        off = pl.multiple_of(tile * Q_BLOCK, chunk_keys)
        return _dot_nt(ksa_ref[pl.ds(off, chunk_keys), :], qa_ref[tile // (LANE // 2)])

    def far_consume(c, s_slot):
        off = pl.multiple_of(c * chunk_keys, chunk_keys)
        _flash_update_from(s_slot, vst_ref[:, pl.ds(off, chunk_keys)], m_ref, acc_ref)

    s_ref[0] = far_scores(0)
    _pipelined_chunks(n_chunks, far_scores, far_consume, s_ref)

    @pl.when(n_chunks % 2 == 1)
    def _():
        far_consume(n_chunks - 1, s_ref.at[0])

    left = n_far % FAR_TILES

    @pl.when(left >= 2)
    def _():
        s_t, v_t = sel_chunk(n_far - left, 2 * Q_BLOCK)
        _flash_update(s_t, v_t, m_ref, acc_ref)

    @pl.when(left % 2 == 1)
    def _():
        s_t, v_t = sel_chunk(n_far - 1, Q_BLOCK)
        _flash_update(s_t, v_t, m_ref, acc_ref)

    out_ref[...] += gate(1) * _normalized(acc_ref[...])

    def win_chunk(tile):
        off = pl.multiple_of(jnp.maximum(tile, 0) * Q_BLOCK, Q_BLOCK)
        return _dot_nt(kw_ref[pl.ds(off, Q_BLOCK), :], q), vwt_ref[:, pl.ds(off, Q_BLOCK)]

    s_t, v_t = win_chunk(i)
    win = [(jnp.where(key_row <= tl_lane, s_t + bd_ref[Q_BLOCK:, :], NEG_INF), v_t)]
    s_t, v_t = win_chunk(i - 1)
    win.append((jnp.where(i >= 1, s_t + bd_ref[:Q_BLOCK, :], NEG_INF), v_t))
    for back in (2, 3):
        s_t, v_t = win_chunk(i - back)
        win.append((jnp.where(i >= back, s_t, NEG_INF), v_t))
    s_t, v_t = win_chunk(i - 4)
    win.append((jnp.where((key_row > tl_lane) & (i >= 4), s_t, NEG_INF), v_t))
    _, acc_w = _softmax_tiles(win)
    o = out_ref[...] + gate(2) * _normalized(acc_w)
    for hg in range(NSA_HG):
        o_ref[:, hg * HEAD_DIM:(hg + 1) * HEAD_DIM] = o[:, hg * Q_BLOCK:(hg + 1) * Q_BLOCK].T.astype(o_ref.dtype)


def _nsa_attention(qkh, gates, kcv, kcv_t, ksa, vt, bd, bc, ovl):
    t = qkh.shape[1]
    ncp = t // CMP_STRIDE
    assert ncp % CMP_ROWS == 0
    resident = functools.partial(pl.BlockSpec, pipeline_mode=pl.Buffered(1))
    return pl.pallas_call(
        _nsa_kernel,
        grid=(NSA_GROUPS, t // Q_BLOCK),
        in_specs=[pl.BlockSpec((NSA_HG, Q_BLOCK, HEAD_DIM), lambda g, i: (g, i, 0)),
                  pl.BlockSpec((Q_BLOCK, LANE), lambda g, i: (i, 0)),
                  resident((None, CMP_PAD + ncp, HEAD_DIM), lambda g, i: (g, 0, 0)),
                  resident((None, CMP_PAD + ncp, HEAD_DIM), lambda g, i: (NSA_GROUPS + g, 0, 0)),
                  resident((None, HEAD_DIM, CMP_PAD + ncp), lambda g, i: (NSA_GROUPS + g, 0, 0)),
                  resident((None, t, 2 * HEAD_DIM), lambda g, i: (g, 0, 0)),
                  resident((None, V_ROWS, t), lambda g, i: (g, 0, 0)),
                  resident((None, t, HEAD_DIM), lambda g, i: (N_HEADS + 3 * NSA_GROUPS + g, 0, 0)),
                  resident((None, V_ROWS, t), lambda g, i: (NSA_GROUPS + g, 0, 0)),
                  resident((2 * Q_BLOCK, NSA_ROWS), lambda g, i: (0, g)),
                  resident((CMP_NEAR, NSA_ROWS), lambda g, i: (0, g)),
                  resident((SEL_COLS, ncp), lambda g, i: (0, 0))],
        out_specs=pl.BlockSpec((Q_BLOCK, NSA_HG * HEAD_DIM), lambda g, i: (i, g)),
        out_shape=jax.ShapeDtypeStruct((t, D_MODEL), BF16),
        scratch_shapes=[pltpu.VMEM((SEL_COLS // LANE, NSA_ROWS, 2 * HEAD_DIM), BF16),
                        pltpu.VMEM((2, FAR_TILES * Q_BLOCK, NSA_ROWS), F32),
                        pltpu.VMEM((ncp, NSA_ROWS), F32), pltpu.VMEM((SEL_COLS, Q_BLOCK), F32),
                        pltpu.VMEM((LANE, Q_BLOCK), F32), pltpu.VMEM((1, NSA_ROWS), F32),
                        pltpu.VMEM((V_ROWS, NSA_ROWS), F32), pltpu.VMEM((HEAD_DIM, NSA_ROWS), F32)],
        compiler_params=_cparams(("parallel", "arbitrary")),
        name="nsa_attention",
    )(qkh, gates, kcv, kcv, kcv_t, ksa, vt, qkh, vt, bd, bc, ovl)


def _nsa_layer(h, g_norm, w_in, w_o, layer, cmp_pos, cmp_w1, cmp_w2, bd, bc):
    t = h.shape[0]
    assert t // SEL_BLOCK <= SEL_COLS
    a = _rmsnorm(h, g_norm, BF16)
    tn = NSA_KV_DIM
    q_blocks = D_MODEL // tn
    qkh = _proj(a, w_in, layer, n_out=D_MODEL + 4 * NSA_KV_DIM, out_dtype=BF16, head_major=True, tn=tn,
                n_scaled=D_MODEL,
                col_block=lambda j: j + jnp.where(j >= q_blocks + 3, 1, 0))
    vt = _proj_t(a, w_in, layer, n_out=2 * NSA_KV_DIM, tn=tn, col_block=lambda j: q_blocks + 3 + 2 * j)
    gates = _proj(a, w_in, layer, n_out=LANE, out_dtype=F32, head_major=False, tn=LANE,
                  col_block=lambda j: (D_MODEL + 6 * NSA_KV_DIM) // LANE + j,
                  valid_cols=3 * N_HEADS)
    kcv = _compress(qkh[N_HEADS:N_HEADS + 2 * NSA_GROUPS], cmp_pos, cmp_w1, cmp_w2)
    kcv_t = jnp.swapaxes(kcv, 1, 2)
    key_blk = (jnp.arange(t) // SEL_BLOCK) % LANE
    onehot = (key_blk[:, None] == jnp.arange(LANE)[None, :]).astype(BF16)
    ksa = jnp.concatenate([qkh[N_HEADS + 2 * NSA_GROUPS:N_HEADS + 3 * NSA_GROUPS],
                           jnp.broadcast_to(onehot, (NSA_GROUPS, t, LANE))], axis=-1)
    n = jnp.arange(t // CMP_STRIDE)[None, :]
    sblk = jnp.arange(SEL_COLS)[:, None]
    ovl = ((n >= 4 * sblk - 1) & (n <= 4 * sblk + 3)).astype(BF16)
    o = _nsa_attention(qkh, gates, kcv, kcv_t, ksa, vt, bd, bc, ovl)
    return _mm_res(o, w_o, layer, h, tile=WO_TILE)


def kernel(x, norm_mix, norm_ffn, norm_final, rel_table, nsa_w_in, nsa_w_o, nsa_cmp_pos, nsa_cmp_w1,
           nsa_cmp_w2, fox_w_in, fox_b_f, fox_w_o, ffn_w_gate, ffn_w_up, ffn_w_down):
    depth = norm_mix.shape[0]
    bd, bc = _nsa_bias_templates(rel_table.astype(F32))
    nsa_w_in, fox_w_in = jnp.swapaxes(nsa_w_in.astype(F32), 1, 2), jnp.swapaxes(fox_w_in.astype(F32), 1, 2)
    ffn_w_gate, ffn_w_up = ffn_w_gate.astype(F32), ffn_w_up.astype(F32)
    nsa_w_o, fox_w_o, ffn_w_down = nsa_w_o.astype(BF16), fox_w_o.astype(BF16), ffn_w_down.astype(BF16)
    outs = []
    for bi in range(x.shape[0]):
        h = x[bi]
        for i in range(depth):
            j = i // 2
            if i % 2 == 0:
                h = _nsa_layer(h, norm_mix[i], nsa_w_in, nsa_w_o, j, nsa_cmp_pos[j], nsa_cmp_w1[j],
                               nsa_cmp_w2[j], bd, bc)
            else:
                h = _fox_layer(h, norm_mix[i], fox_w_in, fox_b_f[j], fox_w_o, j)
            h = _ffn(h, norm_ffn[i], ffn_w_gate, ffn_w_up, ffn_w_down, i)
        outs.append(_rmsnorm(h, norm_final, F32))
    return jnp.stack(outs, axis=0)
```

```python
import functools
import math

import jax
import jax.numpy as jnp
from jax import lax
from jax.experimental import pallas as pl
from jax.experimental.pallas import tpu as pltpu

F32 = jnp.float32
BF16 = jnp.bfloat16

D_MODEL = 4096
HEAD_DIM = 128
N_HEADS = D_MODEL // HEAD_DIM
NSA_GROUPS = 4
NSA_HG = N_HEADS // NSA_GROUPS
NSA_KV_DIM = NSA_GROUPS * HEAD_DIM
CMP_BLOCK = 32
CMP_STRIDE = 16
SEL_BLOCK = 64
SEL_TOPK = 16
WINDOW = 512
REL_BUCKETS = 32
REL_MAX_DIST = 128
Q_BLOCK = 128
RMS_EPS = 1e-6
NEG_INF = -1e30
LOG2E = 1.4426950408889634
Q_SCALE = HEAD_DIM ** -0.5 * LOG2E

LANE = 128
SEL_COLS = 256
CMP_PAD = 128
CMP_NEAR = 128
CMP_ROWS = 256
FAR_TILES = 4
NSA_ROWS = NSA_HG * Q_BLOCK
V_ROWS = HEAD_DIM + 16
KEY_BLOCK = 256
N_PIECES = 3

RMS_ROWS = 256
PROJ_TILE = (1024, 512)
GATEUP_TILE = (2048, 256)
DOWN_TILE = (512, 512)
WO_TILE = (1024, 512)
FOX_Q_TILE = 1024
CUM_ROWS = 512
VMEM_LIMIT = 56 * 1024 * 1024


def _cparams(sem):
    return pltpu.CompilerParams(dimension_semantics=sem, vmem_limit_bytes=VMEM_LIMIT)


def _dot(a, b):
    return jnp.dot(a, b, preferred_element_type=F32)


def _dot_nt(a, b):
    return lax.dot_general(a, b, (((1,), (1,)), ((), ())), preferred_element_type=F32)


def _rmsnorm_kernel(x_ref, g_ref, o_ref):
    x = x_ref[...]
    ms = jnp.mean(x * x, axis=-1, keepdims=True)
    o_ref[...] = (x * lax.rsqrt(ms + RMS_EPS) * g_ref[...]).astype(o_ref.dtype)


def _rmsnorm(x, g, out_dtype):
    t, d = x.shape
    tm = min(RMS_ROWS, t)
    return pl.pallas_call(
        _rmsnorm_kernel,
        grid=(t // tm,),
        in_specs=[pl.BlockSpec((tm, d), lambda i: (i, 0)),
                  pl.BlockSpec((1, d), lambda i: (0, 0))],
        out_specs=pl.BlockSpec((tm, d), lambda i: (i, 0)),
        out_shape=jax.ShapeDtypeStruct((t, d), out_dtype),
        compiler_params=_cparams(("parallel",)),
        name="rmsnorm",
    )(x, g.reshape(1, d))


def _proj_kernel(x_ref, w_ref, o_ref, *, head_major, n_scaled_blocks, valid_cols):
    r = _dot_nt(x_ref[...], w_ref[...].astype(BF16))
    if valid_cols is not None:
        r = jnp.where(lax.broadcasted_iota(jnp.int32, r.shape, 1) < valid_cols, r, 0.0)
    if n_scaled_blocks:
        r = r * jnp.where(pl.program_id(1) < n_scaled_blocks, Q_SCALE, 1.0)
    if head_major:
        for s in range(o_ref.shape[0]):
            o_ref[s] = r[:, s * LANE:(s + 1) * LANE].astype(o_ref.dtype)
    else:
        o_ref[...] = r.astype(o_ref.dtype)


def _identity(j):
    return j


_row_block = functools.partial(pl.BlockSpec, pipeline_mode=pl.Buffered(1))


def _proj(x, w, layer, *, n_out, out_dtype, head_major, tn=PROJ_TILE[1], col_block=_identity, n_scaled=0,
          valid_cols=None):
    t, k = x.shape
    n = n_out
    tm = min(PROJ_TILE[0], t)
    tn = min(tn, n)
    assert n_scaled % tn == 0 and n % tn == 0
    if head_major:
        out_shape = jax.ShapeDtypeStruct((n // LANE, t, LANE), out_dtype)
        out_spec = pl.BlockSpec((tn // LANE, tm, LANE), lambda i, j: (j, i, 0))
    else:
        out_shape = jax.ShapeDtypeStruct((t, n), out_dtype)
        out_spec = pl.BlockSpec((tm, tn), lambda i, j: (i, j))
    return pl.pallas_call(
        functools.partial(_proj_kernel, head_major=head_major, n_scaled_blocks=n_scaled // tn,
                          valid_cols=valid_cols),
        grid=(t // tm, n // tn),
        in_specs=[pl.BlockSpec((tm, k), lambda i, j: (i, 0)),
                  pl.BlockSpec((None, tn, k), lambda i, j: (layer, col_block(j), 0))],
        out_specs=out_spec,
        out_shape=out_shape,
        compiler_params=_cparams(("parallel", "parallel")),
        name="proj",
    )(x, w)


def _proj_t_kernel(x_ref, w_ref, o_ref):
    r = _dot_nt(x_ref[...], w_ref[...].astype(BF16))
    ones = _ones_rows(r.shape[0], o_ref.dtype)
    for s in range(o_ref.shape[0]):
        o_ref[s, 0:HEAD_DIM, :] = r[:, s * LANE:(s + 1) * LANE].T.astype(o_ref.dtype)
        o_ref[s, HEAD_DIM:, :] = ones


def _proj_t(x, w, layer, *, n_out, tn=PROJ_TILE[1], col_block=_identity):
    t, k = x.shape
    n = n_out
    tm = min(PROJ_TILE[0], t)
    tn = min(tn, n)
    return pl.pallas_call(
        _proj_t_kernel,
        grid=(t // tm, n // tn),
        in_specs=[pl.BlockSpec((tm, k), lambda i, j: (i, 0)),
                  pl.BlockSpec((None, tn, k), lambda i, j: (layer, col_block(j), 0))],
        out_specs=pl.BlockSpec((tn // LANE, V_ROWS, tm), lambda i, j: (j, 0, i)),
        out_shape=jax.ShapeDtypeStruct((n // LANE, V_ROWS, t), BF16),
        compiler_params=_cparams(("parallel", "parallel")),
        name="proj_t",
    )(x, w)


def _mm_res_kernel(x_ref, w_ref, r_ref, o_ref):
    o_ref[...] = r_ref[...] + _dot(x_ref[...], w_ref[...])


def _mm_res(x, w, layer, res, *, tile, x_spec=pl.BlockSpec):
    t, k = x.shape
    n = w.shape[2]
    tm, tn = min(tile[0], t), tile[1]
    return pl.pallas_call(
        _mm_res_kernel,
        grid=(t // tm, n // tn),
        in_specs=[x_spec((tm, k), lambda i, j: (i, 0)),
                  pl.BlockSpec((None, k, tn), lambda i, j: (layer, 0, j)),
                  pl.BlockSpec((tm, tn), lambda i, j: (i, j))],
        out_specs=pl.BlockSpec((tm, tn), lambda i, j: (i, j)),
        out_shape=jax.ShapeDtypeStruct((t, n), F32),
        compiler_params=_cparams(("parallel", "parallel")),
        name="mm_res",
    )(x, w, res)


def _gateup_kernel(x_ref, wg_ref, wu_ref, o_ref):
    x = x_ref[...]
    g = _dot(x, wg_ref[...].astype(BF16))
    u = _dot(x, wu_ref[...].astype(BF16))
    o_ref[...] = (g * jax.nn.sigmoid(g) * u).astype(o_ref.dtype)


def _gateup(x, wg, wu, layer):
    t, k = x.shape
    n = wg.shape[2]
    tm, tn = min(GATEUP_TILE[0], t), GATEUP_TILE[1]
    return pl.pallas_call(
        _gateup_kernel,
        grid=(t // tm, n // tn),
        in_specs=[_row_block((tm, k), lambda i, j: (i, 0)),
                  pl.BlockSpec((None, k, tn), lambda i, j: (layer, 0, j)),
                  pl.BlockSpec((None, k, tn), lambda i, j: (layer, 0, j))],
        out_specs=pl.BlockSpec((tm, tn), lambda i, j: (i, j)),
        out_shape=jax.ShapeDtypeStruct((t, n), BF16),
        compiler_params=_cparams(("parallel", "parallel")),
        name="gateup",
    )(x, wg, wu)


def _ffn(h, g_norm, w_gate, w_up, w_down, layer):
    a = _rmsnorm(h, g_norm, BF16)
    h1 = _gateup(a, w_gate, w_up, layer)
    return _mm_res(h1, w_down, layer, h, tile=DOWN_TILE)


def _ones_rows(n_cols, dtype):
    row = lax.broadcasted_iota(jnp.int32, (V_ROWS - HEAD_DIM, n_cols), 0)
    return jnp.where(row == 0, 1.0, 0.0).astype(dtype)


def _flash_init(m_ref, acc_ref):
    m_ref[...] = jnp.full(m_ref.shape, NEG_INF, F32)
    acc_ref[...] = jnp.zeros(acc_ref.shape, F32)


def _flash_update(s_t, v_t, m_ref, acc_ref):
    m_prev = m_ref[...]
    m_new = jnp.maximum(m_prev, jnp.max(s_t, axis=0, keepdims=True))
    alpha = jnp.exp2(m_prev - m_new)
    p = jnp.exp2(s_t - m_new)
    acc_ref[...] = alpha * acc_ref[...] + _dot(v_t, p.astype(BF16))
    m_ref[...] = m_new


def _flash_update_from(s_slot, v_t, m_ref, acc_ref):
    n_keys = s_slot.shape[0]
    m_prev = m_ref[...]
    m_new = jnp.maximum(m_prev, jnp.max(s_slot[...], axis=0, keepdims=True))
    acc = jnp.exp2(m_prev - m_new) * acc_ref[...]
    for r in range(0, n_keys, KEY_BLOCK):
        p = jnp.exp2(s_slot[r:r + KEY_BLOCK, :] - m_new)
        acc = acc + _dot(v_t[:, r:r + KEY_BLOCK], p.astype(BF16))
    acc_ref[...] = acc
    m_ref[...] = m_new


def _softmax_tiles(tiles):
    mx = functools.reduce(jnp.maximum, [jnp.max(s_t, axis=0, keepdims=True) for s_t, _ in tiles])
    acc = None
    for s_t, v_t in tiles:
        a = _dot(v_t, jnp.exp2(s_t - mx).astype(BF16))
        acc = a if acc is None else acc + a
    return mx, acc


def _normalized(acc):
    return acc[0:HEAD_DIM] / acc[HEAD_DIM:HEAD_DIM + 1]


def _split3(x):
    hi = x.astype(BF16)
    r = x - hi.astype(F32)
    mid = r.astype(BF16)
    lo = (r - mid.astype(F32)).astype(BF16)
    return hi, mid, lo


def _cum_kernel(f_ref, b_ref, tri_ref, place_ref, qx_ref, kx_ref, carry_ref):
    @pl.when(pl.program_id(0) == 0)
    def _():
        carry_ref[...] = jnp.zeros_like(carry_ref)

    head_lane = lax.broadcasted_iota(jnp.int32, f_ref.shape, 1) < N_HEADS
    x = jnp.where(head_lane, f_ref[...] + b_ref[...], 0.0)
    log_f = jnp.minimum(x, 0.0) - jnp.log1p(jnp.exp(-jnp.abs(x)))
    c = jnp.dot(tri_ref[...], log_f, precision=lax.Precision.HIGHEST,
                preferred_element_type=F32) + carry_ref[...]
    carry_ref[...] = c[c.shape[0] - 1:, :]
    pieces = jnp.concatenate(_split3(c * LOG2E), axis=1)
    lane = lax.broadcasted_iota(jnp.int32, (c.shape[0], LANE), 1)
    q_const = jnp.where((lane >= N_PIECES) & (lane < 2 * N_PIECES), -1.0, 0.0)
    k_const = jnp.where(lane < N_PIECES, 1.0, 0.0)
    for h in range(N_HEADS):
        placed = _dot(pieces, place_ref[h])
        qx_ref[h] = (placed[:, :LANE] + q_const).astype(BF16)
        kx_ref[h] = (placed[:, LANE:] + k_const).astype(BF16)


def _forget_extras(f, b):
    t, n = f.shape
    tb = min(CUM_ROWS, t)
    tri = jnp.tril(jnp.ones((tb, tb), F32))
    h = jnp.arange(N_HEADS)[:, None, None]
    row = jnp.arange(N_PIECES * LANE)[None, :, None]
    col = jnp.arange(2 * LANE)[None, None, :]
    piece, head = row // LANE, row % LANE
    place = ((head == h) & ((col == piece) | (col == LANE + N_PIECES + piece))).astype(BF16)
    out = jax.ShapeDtypeStruct((N_HEADS, t, LANE), BF16)
    return pl.pallas_call(
        _cum_kernel,
        grid=(t // tb,),
        in_specs=[pl.BlockSpec((tb, n), lambda i: (i, 0)),
                  pl.BlockSpec((1, n), lambda i: (0, 0)),
                  pl.BlockSpec((tb, tb), lambda i: (0, 0)),
                  pl.BlockSpec((N_HEADS, N_PIECES * LANE, 2 * LANE), lambda i: (0, 0, 0))],
        out_specs=[pl.BlockSpec((N_HEADS, tb, LANE), lambda i: (0, i, 0)),
                   pl.BlockSpec((N_HEADS, tb, LANE), lambda i: (0, i, 0))],
        out_shape=[out, out],
        scratch_shapes=[pltpu.VMEM((1, n), F32)],
        compiler_params=_cparams(("arbitrary",)),
        name="forget_extras",
    )(f, b, tri, place)


def _pipelined_loop(first, trips, per_trip, scores, consume, s_ref):
    def body(t, carry):
        c = first + t * per_trip
        for u in range(per_trip):
            s_next = scores(c + u + 1)
            consume(c + u, s_ref.at[u % 2])
            s_ref[(u + 1) % 2] = s_next
        return carry

    lax.fori_loop(0, trips, body, 0)


def _pipelined_chunks(n_chunks, scores, consume, s_ref):
    done = 0
    for per_trip in (16, 8, 4, 2):
        trips = (n_chunks - done) // per_trip
        _pipelined_loop(done, trips, per_trip, scores, consume, s_ref)
        done = done + trips * per_trip


def _fox_kernel(q_ref, qx_ref, k_ref, kx_ref, vt_ref, o_ref, ka_ref, qa_ref, s_ref, m_ref, acc_ref, *, tq, tk):
    i = pl.program_id(1)
    assert tq == 2 * tk

    @pl.when(i == 0)
    def _():
        ka_ref[:, 0:HEAD_DIM] = k_ref[...]
        ka_ref[:, HEAD_DIM:] = kx_ref[...]

    qa_ref[:, 0:HEAD_DIM] = q_ref[...]
    qa_ref[:, HEAD_DIM:] = qx_ref[...]
    _flash_init(m_ref, acc_ref)

    def scores(c):
        off = pl.multiple_of(c * tk, tk)
        return _dot_nt(ka_ref[pl.ds(off, tk), :], qa_ref[...])

    def values(c):
        return vt_ref[:, pl.ds(pl.multiple_of(c * tk, tk), tk)]

    def consume(c, s_slot):
        _flash_update_from(s_slot, values(c), m_ref, acc_ref)

    n_full = 2 * i
    s_ref[0] = scores(0)
    _pipelined_chunks(n_full, scores, consume, s_ref)
    key = lax.broadcasted_iota(jnp.int32, (tk, tq), 0)
    qry = lax.broadcasted_iota(jnp.int32, (tk, tq), 1)
    s_last = scores(n_full + 1)
    _flash_update(jnp.where(key <= qry, s_ref[0], NEG_INF), values(n_full), m_ref, acc_ref)
    _flash_update(jnp.where(key + tk <= qry, s_last, NEG_INF), values(n_full + 1), m_ref, acc_ref)
    o_ref[...] = _normalized(acc_ref[...]).T.astype(o_ref.dtype)


def _fox_attention(qkh, qx, kx, vt):
    t = qkh.shape[1]
    tq = min(FOX_Q_TILE, t)
    tk = tq // 2
    return pl.pallas_call(
        functools.partial(_fox_kernel, tq=tq, tk=tk),
        grid=(N_HEADS, t // tq),
        in_specs=[pl.BlockSpec((None, tq, HEAD_DIM), lambda h, i: (h, i, 0)),
                  pl.BlockSpec((None, tq, LANE), lambda h, i: (h, i, 0)),
                  pl.BlockSpec((None, t, HEAD_DIM), lambda h, i: (N_HEADS + h, 0, 0)),
                  pl.BlockSpec((None, t, LANE), lambda h, i: (h, 0, 0)),
                  pl.BlockSpec((None, V_ROWS, t), lambda h, i: (h, 0, 0))],
        out_specs=pl.BlockSpec((tq, HEAD_DIM), lambda h, i: (i, h)),
        out_shape=jax.ShapeDtypeStruct((t, D_MODEL), BF16),
        scratch_shapes=[pltpu.VMEM((t, 2 * HEAD_DIM), BF16),
                        pltpu.VMEM((tq, 2 * HEAD_DIM), BF16), pltpu.VMEM((2, tk, tq), F32),
                        pltpu.VMEM((1, tq), F32), pltpu.VMEM((V_ROWS, tq), F32)],
        compiler_params=_cparams(("parallel", "arbitrary")),
        name="fox_attention",
    )(qkh, qx, qkh, kx, vt)


def _fox_layer(h, g_norm, w_in, b_f, w_o, layer):
    a = _rmsnorm(h, g_norm, BF16)
    tn = PROJ_TILE[1]
    qkh = _proj(a, w_in, layer, n_out=2 * D_MODEL, out_dtype=BF16, head_major=True, tn=tn,
                n_scaled=D_MODEL)
    vt = _proj_t(a, w_in, layer, n_out=D_MODEL, tn=tn, col_block=lambda j: 2 * D_MODEL // tn + j)
    f = _proj(a, w_in, layer, n_out=LANE, out_dtype=F32, head_major=False, tn=LANE,
              col_block=lambda j: 3 * D_MODEL // LANE + j, valid_cols=N_HEADS)
    b = jnp.pad(b_f.astype(F32), (0, LANE - N_HEADS)).reshape(1, LANE)
    qx, kx = _forget_extras(f, b)
    o = _fox_attention(qkh, qx, kx, vt)
    return _mm_res(o, w_o, layer, h, tile=WO_TILE)


def _rel_bucket_const(dist):
    n = jnp.maximum(dist, 0)
    max_exact = REL_BUCKETS // 2
    nf = jnp.maximum(n, max_exact).astype(F32)
    large = max_exact + (jnp.log(nf / max_exact) / math.log(REL_MAX_DIST / max_exact)
                         * (REL_BUCKETS - max_exact)).astype(jnp.int32)
    large = jnp.minimum(large, REL_BUCKETS - 1)
    return jnp.where(n < max_exact, n, large)


def _bias_kernel(tab_ref, bkt_ref, o_ref):
    h = pl.program_id(0)
    bkt = bkt_ref[...]
    acc = jnp.zeros(bkt.shape, F32)
    for b in range(REL_BUCKETS):
        acc = jnp.where(bkt == b, tab_ref[b, h], acc)
    o_ref[...] = (acc - tab_ref[REL_BUCKETS - 1, h]) * LOG2E


def _bias_template(rel_table, bkt):
    r, c = bkt.shape
    return pl.pallas_call(
        _bias_kernel,
        grid=(N_HEADS,),
        in_specs=[pl.BlockSpec(memory_space=pltpu.SMEM),
                  pl.BlockSpec((r, c), lambda h: (0, 0))],
        out_specs=pl.BlockSpec((r, c), lambda h: (0, h)),
        out_shape=jax.ShapeDtypeStruct((r, N_HEADS * c), F32),
        compiler_params=_cparams(("arbitrary",)),
        name="bias_template",
    )(rel_table, bkt)


def _nsa_bias_templates(rel_table):
    tl = jnp.arange(Q_BLOCK)[None, :]
    bd = _bias_template(rel_table, _rel_bucket_const(tl + Q_BLOCK - jnp.arange(2 * Q_BLOCK)[:, None]))
    m = jnp.arange(CMP_NEAR)[:, None]
    bc = _bias_template(rel_table, _rel_bucket_const(
        tl - CMP_STRIDE * (m - (CMP_NEAR - 8)) - (CMP_BLOCK - 1)))
    return bd, bc


def _gelu_tanh(x):
    return 0.5 * x * (1.0 + jnp.tanh(math.sqrt(2.0 / math.pi) * (x + 0.044715 * (x * x * x))))


def _cmp_kernel(kb_ref, pos_ref, w1_ref, w2_ref, o_ref, *, nb):
    half = CMP_STRIDE * HEAD_DIM
    kb = kb_ref[...].astype(F32)
    xa = (kb + pos_ref[0:1, :]).astype(BF16)
    xb = (kb + pos_ref[1:2, :]).astype(BF16)
    a = _dot(xa, w1_ref[0:half, :])
    b = _dot(xb, w1_ref[half:2 * half, :])
    pre = a + pltpu.roll(b, nb - 1, axis=0)
    out = _dot(_gelu_tanh(pre).astype(BF16), w2_ref[...])
    row = lax.broadcasted_iota(jnp.int32, out.shape, 0)
    out = jnp.where(row < nb - 1, out, 0.0)
    o_ref[0:CMP_PAD, :] = jnp.zeros((CMP_PAD, HEAD_DIM), F32)
    o_ref[CMP_PAD:CMP_PAD + nb, :] = out


def _compress(kvh, cmp_pos, cmp_w1, cmp_w2):
    t = kvh.shape[1]
    nb = t // CMP_STRIDE
    kb = kvh[:2 * NSA_GROUPS].reshape(2 * NSA_GROUPS, nb, CMP_STRIDE * HEAD_DIM)
    pos = cmp_pos.astype(F32).reshape(2, 2, CMP_STRIDE * HEAD_DIM)
    return pl.pallas_call(
        functools.partial(_cmp_kernel, nb=nb),
        grid=(2 * NSA_GROUPS,),
        in_specs=[pl.BlockSpec((None, nb, CMP_STRIDE * HEAD_DIM), lambda j: (j, 0, 0)),
                  pl.BlockSpec((None, 2, CMP_STRIDE * HEAD_DIM), lambda j: (j // NSA_GROUPS, 0, 0)),
                  pl.BlockSpec((None, CMP_BLOCK * HEAD_DIM, HEAD_DIM), lambda j: (j // NSA_GROUPS, 0, 0)),
                  pl.BlockSpec((None, HEAD_DIM, HEAD_DIM), lambda j: (j // NSA_GROUPS, 0, 0))],
        out_specs=pl.BlockSpec((None, CMP_PAD + nb, HEAD_DIM), lambda j: (j, 0, 0)),
        out_shape=jax.ShapeDtypeStruct((2 * NSA_GROUPS, CMP_PAD + nb, HEAD_DIM), F32),
        compiler_params=_cparams(("parallel",)),
        name="compress",
    )(kb, pos, cmp_w1.astype(BF16), cmp_w2.astype(BF16))


def _dot_split_rhs(w, x):
    hi, mid, lo = _split3(x)
    return _dot(w, hi) + _dot(w, mid) + _dot(w, lo)


def _nsa_kernel(q_ref, g_ref, kc_ref, vc_ref, vct_ref, ksa_ref, vst_ref, kw_ref, vwt_ref,
                bd_ref, bc_ref, ovl_ref, o_ref,
                qa_ref, s_ref, sc_ref, imp_ref, gate_ref, m_ref, acc_ref, out_ref):
    i = pl.program_id(1)
    rows = NSA_ROWS
    q = q_ref[...].reshape(rows, HEAD_DIM)
    tl_lane = lax.broadcasted_iota(jnp.int32, (Q_BLOCK, rows), 1) % Q_BLOCK
    key_row = lax.broadcasted_iota(jnp.int32, (Q_BLOCK, rows), 0)
    gate_ref[...] = jax.nn.sigmoid(g_ref[...]).T
    gate_row0 = 3 * NSA_HG * pl.program_id(0)

    def gate(branch):
        return jnp.concatenate([gate_ref[pl.ds(gate_row0 + 3 * hg + branch, 1), :] for hg in range(NSA_HG)], axis=1)

    def head_sum(p):
        acc = p[:, 0:Q_BLOCK]
        for hg in range(1, NSA_HG):
            acc = acc + p[:, hg * Q_BLOCK:(hg + 1) * Q_BLOCK]
        return acc

    n_first_near = 8 * i - (CMP_NEAR - 8)
    n_blocks = (jnp.maximum(n_first_near, 0) + CMP_ROWS - 1) // CMP_ROWS

    def far_rows(b):
        return pl.multiple_of(b * CMP_ROWS, CMP_ROWS)

    def scores_pass(b, mx):
        r0 = far_rows(b)
        kb = kc_ref[pl.ds(CMP_PAD + r0, CMP_ROWS), :].astype(BF16)
        row = r0 + lax.broadcasted_iota(jnp.int32, (CMP_ROWS, rows), 0)
        s = jnp.where(row < n_first_near, _dot_nt(kb, q), NEG_INF)
        sc_ref[pl.ds(r0, CMP_ROWS), :] = s
        return jnp.maximum(mx, jnp.max(s, axis=0, keepdims=True))

    mx_far = lax.fori_loop(0, n_blocks, scores_pass, jnp.full((1, rows), NEG_INF, F32))
    start = pl.multiple_of(8 * i + 8 + CMP_PAD - CMP_NEAR, 8)
    kcn = kc_ref[pl.ds(start, CMP_NEAR), :].astype(BF16)
    vcn_t = vc_ref[pl.ds(start, CMP_NEAR), :].T.astype(BF16)
    d_near = tl_lane - CMP_STRIDE * (key_row - (CMP_NEAR - 8)) - (CMP_BLOCK - 1)
    near_ok = (d_near >= 0) & (n_first_near + key_row >= 0)
    s_near = jnp.where(near_ok, _dot_nt(kcn, q) + bc_ref[...], NEG_INF)
    mx = jnp.maximum(mx_far, jnp.max(s_near, axis=0, keepdims=True))
    e_near = jnp.exp2(s_near - mx)

    def exp_pass(b, den):
        r0 = far_rows(b)
        e = jnp.exp2(sc_ref[pl.ds(r0, CMP_ROWS), :] - mx)
        sc_ref[pl.ds(r0, CMP_ROWS), :] = e
        return den + jnp.sum(e, axis=0, keepdims=True)

    den = lax.fori_loop(0, n_blocks, exp_pass, jnp.sum(e_near, axis=0, keepdims=True))
    inv = jnp.where(mx > 0.5 * NEG_INF, 1.0 / den, 0.0)
    p_near = e_near * inv
    sb = lax.broadcasted_iota(jnp.int32, (SEL_COLS, CMP_NEAR), 0)
    nn = n_first_near + lax.broadcasted_iota(jnp.int32, (SEL_COLS, CMP_NEAR), 1)
    ovl_near = jnp.where((nn >= 4 * sb - 1) & (nn <= 4 * sb + 3) & (nn >= 0), 1.0, 0.0).astype(BF16)
    out_ref[...] = _dot(vcn_t, p_near.astype(BF16))
    imp_ref[...] = _dot_split_rhs(ovl_near, head_sum(p_near))

    def out_pass(b, carry):
        r0 = far_rows(b)
        p = sc_ref[pl.ds(r0, CMP_ROWS), :] * inv
        out_ref[...] += _dot(vct_ref[:, pl.ds(CMP_PAD + r0, CMP_ROWS)].astype(BF16), p.astype(BF16))
        imp_ref[...] += _dot_split_rhs(ovl_ref[:, pl.ds(r0, CMP_ROWS)], head_sum(p))
        return carry

    lax.fori_loop(0, n_blocks, out_pass, 0)
    out_ref[...] = gate(0) * out_ref[...]

    imp = imp_ref[...]
    blk = lax.broadcasted_iota(jnp.int32, (SEL_COLS, Q_BLOCK), 0)
    tl = lax.broadcasted_iota(jnp.int32, (SEL_COLS, Q_BLOCK), 1)
    cur = 2 * i + (tl >= SEL_BLOCK).astype(jnp.int32)
    forced = (blk == 0) | (blk == cur) | (blk == cur - 1)
    causal_blk = blk * SEL_BLOCK <= Q_BLOCK * i + tl
    work = jnp.where(forced, -1.0, jnp.where(causal_blk, imp, -1.0))
    blk_f = blk.astype(F32)
    sel = jnp.where(forced, 1.0, 0.0)
    for _ in range(SEL_TOPK - 3):
        best = jnp.max(work, axis=0, keepdims=True)
        first = jnp.min(jnp.where(work == best, blk_f, float(SEL_COLS)), axis=0, keepdims=True)
        pick = blk_f == first
        sel = jnp.where(pick, 1.0, sel)
        work = jnp.where(pick, -2.0, work)
    amask = jnp.where(sel > 0.0, 0.0, NEG_INF)
    for half in range(SEL_COLS // LANE):
        a_t = amask[half * LANE:(half + 1) * LANE, :].T.astype(BF16)
        qa_ref[half, :, 0:HEAD_DIM] = q
        qa_ref[half, :, HEAD_DIM:2 * HEAD_DIM] = jnp.concatenate([a_t] * NSA_HG, axis=0)

    def sel_chunk(tile, n_keys):
        off = pl.multiple_of(tile * Q_BLOCK, Q_BLOCK)
        qa = qa_ref[tile // (LANE // 2)]
        return _dot_nt(ksa_ref[pl.ds(off, n_keys), :], qa), vst_ref[:, pl.ds(off, n_keys)]

    s_t, v_t = sel_chunk(i, Q_BLOCK)
    near = [(jnp.where(key_row <= tl_lane, s_t + bd_ref[Q_BLOCK:, :], NEG_INF), v_t)]
    s_t, v_t = sel_chunk(jnp.maximum(i - 1, 0), Q_BLOCK)
    near.append((jnp.where(i >= 1, s_t + bd_ref[:Q_BLOCK, :], NEG_INF), v_t))
    m_ref[...], acc_ref[...] = _softmax_tiles(near)
    n_far = jnp.maximum(i - 1, 0)
    n_chunks = n_far // FAR_TILES
    chunk_keys = FAR_TILES * Q_BLOCK

    def far_scores(c):
        tile = FAR_TILES * jnp.minimum(c, jnp.maximum(n_chunks - 1, 0))
        off = pl.multiple_of(tile * Q_BLOCK, chunk_keys)
        return _dot_nt(ksa_ref[pl.ds(off, chunk_keys), :], qa_ref[tile // (LANE // 2)])

    def far_consume(c, s_slot):
        off = pl.multiple_of(c * chunk_keys, chunk_keys)
        _flash_update_from(s_slot, vst_ref[:, pl.ds(off, chunk_keys)], m_ref, acc_ref)

    s_ref[0] = far_scores(0)
    _pipelined_chunks(n_chunks, far_scores, far_consume, s_ref)

    @pl.when(n_chunks % 2 == 1)
    def _():
        far_consume(n_chunks - 1, s_ref.at[0])

    left = n_far % FAR_TILES

    @pl.when(left >= 2)
    def _():
        s_t, v_t = sel_chunk(n_far - left, 2 * Q_BLOCK)
        _flash_update(s_t, v_t, m_ref, acc_ref)

    @pl.when(left % 2 == 1)
    def _():
        s_t, v_t = sel_chunk(n_far - 1, Q_BLOCK)
        _flash_update(s_t, v_t, m_ref, acc_ref)

    out_ref[...] += gate(1) * _normalized(acc_ref[...])

    def win_chunk(tile):
        off = pl.multiple_of(jnp.maximum(tile, 0) * Q_BLOCK, Q_BLOCK)
        return _dot_nt(kw_ref[pl.ds(off, Q_BLOCK), :], q), vwt_ref[:, pl.ds(off, Q_BLOCK)]

    s_t, v_t = win_chunk(i)
    win = [(jnp.where(key_row <= tl_lane, s_t + bd_ref[Q_BLOCK:, :], NEG_INF), v_t)]
    s_t, v_t = win_chunk(i - 1)
    win.append((jnp.where(i >= 1, s_t + bd_ref[:Q_BLOCK, :], NEG_INF), v_t))
    for back in (2, 3):
        s_t, v_t = win_chunk(i - back)
        win.append((jnp.where(i >= back, s_t, NEG_INF), v_t))
    s_t, v_t = win_chunk(i - 4)
    win.append((jnp.where((key_row > tl_lane) & (i >= 4), s_t, NEG_INF), v_t))
    _, acc_w = _softmax_tiles(win)
    o = out_ref[...] + gate(2) * _normalized(acc_w)
    for hg in range(NSA_HG):
        o_ref[:, hg * HEAD_DIM:(hg + 1) * HEAD_DIM] = o[:, hg * Q_BLOCK:(hg + 1) * Q_BLOCK].T.astype(o_ref.dtype)


def _nsa_attention(qkh, gates, kcv, kcv_t, ksa, vt, bd, bc, ovl):
    t = qkh.shape[1]
    ncp = t // CMP_STRIDE
    assert ncp % CMP_ROWS == 0
    resident = functools.partial(pl.BlockSpec, pipeline_mode=pl.Buffered(1))
    return pl.pallas_call(
        _nsa_kernel,
        grid=(NSA_GROUPS, t // Q_BLOCK),
        in_specs=[pl.BlockSpec((NSA_HG, Q_BLOCK, HEAD_DIM), lambda g, i: (g, i, 0)),
                  pl.BlockSpec((Q_BLOCK, LANE), lambda g, i: (i, 0)),
                  resident((None, CMP_PAD + ncp, HEAD_DIM), lambda g, i: (g, 0, 0)),
                  resident((None, CMP_PAD + ncp, HEAD_DIM), lambda g, i: (NSA_GROUPS + g, 0, 0)),
                  resident((None, HEAD_DIM, CMP_PAD + ncp), lambda g, i: (NSA_GROUPS + g, 0, 0)),
                  resident((None, t, 2 * HEAD_DIM), lambda g, i: (g, 0, 0)),
                  resident((None, V_ROWS, t), lambda g, i: (g, 0, 0)),
                  resident((None, t, HEAD_DIM), lambda g, i: (N_HEADS + 3 * NSA_GROUPS + g, 0, 0)),
                  resident((None, V_ROWS, t), lambda g, i: (NSA_GROUPS + g, 0, 0)),
                  resident((2 * Q_BLOCK, NSA_ROWS), lambda g, i: (0, g)),
                  resident((CMP_NEAR, NSA_ROWS), lambda g, i: (0, g)),
                  resident((SEL_COLS, ncp), lambda g, i: (0, 0))],
        out_specs=pl.BlockSpec((Q_BLOCK, NSA_HG * HEAD_DIM), lambda g, i: (i, g)),
        out_shape=jax.ShapeDtypeStruct((t, D_MODEL), BF16),
        scratch_shapes=[pltpu.VMEM((SEL_COLS // LANE, NSA_ROWS, 2 * HEAD_DIM), BF16),
                        pltpu.VMEM((2, FAR_TILES * Q_BLOCK, NSA_ROWS), F32),
                        pltpu.VMEM((ncp, NSA_ROWS), F32), pltpu.VMEM((SEL_COLS, Q_BLOCK), F32),
                        pltpu.VMEM((LANE, Q_BLOCK), F32), pltpu.VMEM((1, NSA_ROWS), F32),
                        pltpu.VMEM((V_ROWS, NSA_ROWS), F32), pltpu.VMEM((HEAD_DIM, NSA_ROWS), F32)],
        compiler_params=_cparams(("parallel", "arbitrary")),
        name="nsa_attention",
    )(qkh, gates, kcv, kcv, kcv_t, ksa, vt, qkh, vt, bd, bc, ovl)


def _nsa_layer(h, g_norm, w_in, w_o, layer, cmp_pos, cmp_w1, cmp_w2, bd, bc):
    t = h.shape[0]
    assert t // SEL_BLOCK <= SEL_COLS
    a = _rmsnorm(h, g_norm, BF16)
    tn = NSA_KV_DIM
    q_blocks = D_MODEL // tn
    qkh = _proj(a, w_in, layer, n_out=D_MODEL + 4 * NSA_KV_DIM, out_dtype=BF16, head_major=True, tn=tn,
                n_scaled=D_MODEL,
                col_block=lambda j: j + jnp.where(j >= q_blocks + 3, 1, 0))
    vt = _proj_t(a, w_in, layer, n_out=2 * NSA_KV_DIM, tn=tn, col_block=lambda j: q_blocks + 3 + 2 * j)
    gates = _proj(a, w_in, layer, n_out=LANE, out_dtype=F32, head_major=False, tn=LANE,
                  col_block=lambda j: (D_MODEL + 6 * NSA_KV_DIM) // LANE + j,
                  valid_cols=3 * N_HEADS)
    kcv = _compress(qkh[N_HEADS:N_HEADS + 2 * NSA_GROUPS], cmp_pos, cmp_w1, cmp_w2)
    kcv_t = jnp.swapaxes(kcv, 1, 2)
    key_blk = (jnp.arange(t) // SEL_BLOCK) % LANE
    onehot = (key_blk[:, None] == jnp.arange(LANE)[None, :]).astype(BF16)
    ksa = jnp.concatenate([qkh[N_HEADS + 2 * NSA_GROUPS:N_HEADS + 3 * NSA_GROUPS],
                           jnp.broadcast_to(onehot, (NSA_GROUPS, t, LANE))], axis=-1)
    n = jnp.arange(t // CMP_STRIDE)[None, :]
    sblk = jnp.arange(SEL_COLS)[:, None]
    ovl = ((n >= 4 * sblk - 1) & (n <= 4 * sblk + 3)).astype(BF16)
    o = _nsa_attention(qkh, gates, kcv, kcv_t, ksa, vt, bd, bc, ovl)
    return _mm_res(o, w_o, layer, h, tile=WO_TILE)


def kernel(x, norm_mix, norm_ffn, norm_final, rel_table, nsa_w_in, nsa_w_o, nsa_cmp_pos, nsa_cmp_w1,
           nsa_cmp_w2, fox_w_in, fox_b_f, fox_w_o, ffn_w_gate, ffn_w_up, ffn_w_down):
    depth = norm_mix.shape[0]
    bd, bc = _nsa_bias_templates(rel_table.astype(F32))
    nsa_w_in, fox_w_in = jnp.swapaxes(nsa_w_in.astype(F32), 1, 2), jnp.swapaxes(fox_w_in.astype(F32), 1, 2)
    ffn_w_gate, ffn_w_up = ffn_w_gate.astype(F32), ffn_w_up.astype(F32)
    nsa_w_o, fox_w_o, ffn_w_down = nsa_w_o.astype(BF16), fox_w_o.astype(BF16), ffn_w_down.astype(BF16)
    outs = []
    for bi in range(x.shape[0]):
        h = x[bi]
        for i in range(depth):
            j = i // 2
            if i % 2 == 0:
                h = _nsa_layer(h, norm_mix[i], nsa_w_in, nsa_w_o, j, nsa_cmp_pos[j], nsa_cmp_w1[j],
                               nsa_cmp_w2[j], bd, bc)
            else:
                h = _fox_layer(h, norm_mix[i], fox_w_in, fox_b_f[j], fox_w_o, j)
            h = _ffn(h, norm_ffn[i], ffn_w_gate, ffn_w_up, ffn_w_down, i)
        outs.append(_rmsnorm(h, norm_final, F32))
    return jnp.stack(outs, axis=0)
```
